```python
import jax
import jax.numpy as jnp
from jax import lax
import numpy as np

D_MODEL = 1024
BATCH = 8
SEQ = 4096
DEPTH = 2
DEC_BATCH = 32
DEC_SEQ = 4
PAST_LEN = 16384
PAGE_SIZE = 128

HEAD_DIM = 64
NSA_HEADS = 8
NSA_KV = 2
NSA_GROUP = NSA_HEADS // NSA_KV
CMP_STRIDE = 16
CMP_BLOCK = 32
CMP_RATIO = CMP_BLOCK // CMP_STRIDE
SLC_BLOCK = 64
SLC_TOP = 16
WINDOW = 512
NSA_QBLK = 64
FORCE_SCORE = 1.0e4
SB_HEADS = 8
SB_QBLK = 128
D_RNN = 512
LRU_BLOCKS = 8
LRU_BW = D_RNN // LRU_BLOCKS
CONV_W = 4
LRU_C = 8.0
N_MEM = 256
MEM_HEADS = 4
MEM_HEAD_DIM = 128
D_FF = 2816
N_EXP = 8
TOP_K = 2
D_EXP = 1408
N_DENSE = (DEPTH + 1) // 2
N_MOE = DEPTH // 2
ALPHA = (2.0 * DEPTH) ** 0.25
BETA = (8.0 * DEPTH) ** -0.25
LN_EPS = 1e-5
NEG = -1.0e30
TINY = 1.0e-30
NSA_W = NSA_HEADS * HEAD_DIM
KV_W = NSA_KV * HEAD_DIM
SB_W = SB_HEADS * HEAD_DIM
MEM_W = MEM_HEADS * MEM_HEAD_DIM
N_BRANCH = 4
BRANCH_W = (NSA_W, D_RNN, SB_W, MEM_W)
MIX_W = NSA_W + D_RNN + SB_W + MEM_W
IN_SIZES = (NSA_W, KV_W, KV_W, KV_W, KV_W, KV_W, KV_W, 3 * NSA_HEADS, D_RNN, D_RNN,
            SB_W, SB_W, SB_W, MEM_W, N_BRANCH * D_MODEL)
IN_W = sum(IN_SIZES)

kernel_name = 'hybrid_nsa_rglru_stickbreak_step'


def layer_norm(x, g, b):
    xf = x.astype(jnp.float32)
    mu = jnp.mean(xf, axis=-1, keepdims=True)
    var = jnp.mean(jnp.square(xf - mu), axis=-1, keepdims=True)
    y = (xf - mu) * lax.rsqrt(var + LN_EPS)
    return (y * g.astype(jnp.float32) + b.astype(jnp.float32)).astype(x.dtype)


def masked_softmax(s, mask):
    s = jnp.where(mask, s, NEG)
    m = jnp.max(s, axis=-1, keepdims=True)
    e = jnp.where(mask, jnp.exp(s - m), 0.0)
    return e / jnp.maximum(jnp.sum(e, axis=-1, keepdims=True), TINY)


def alibi_slopes(n_heads):
    return jnp.asarray(2.0 ** (-8.0 * np.arange(1, n_heads + 1) / n_heads), dtype=jnp.float32)


def split_cols(h, sizes):
    return jnp.split(h, [int(c) for c in np.cumsum(sizes)[:-1]], axis=-1)


def project(x, w_in):
    B, T, _ = x.shape
    (nq, ck, cv, sk, sv, wk, wv, ng, lx, lg, sq, sbk, sbv, mq, mg) = split_cols(x @ w_in, IN_SIZES)
    kv = lambda t: t.reshape(B, T, NSA_KV, HEAD_DIM)
    sb = lambda t: t.reshape(B, T, SB_HEADS, HEAD_DIM)
    gates = jax.nn.sigmoid(ng).reshape(B, T, NSA_HEADS, 3)
    return (nq.reshape(B, T, NSA_HEADS, HEAD_DIM), kv(ck), kv(cv), kv(sk), kv(sv), kv(wk), kv(wv), gates,
            lx, lg, sb(sq), sb(sbk), sb(sbv), mq.reshape(B, T, MEM_HEADS, MEM_HEAD_DIM), mg)


def mem_kv(mem, w_mem_kv):
    B, N, _ = mem.shape
    mk, mv = jnp.split(mem @ w_mem_kv, 2, axis=-1)
    return mk.reshape(B, N, MEM_HEADS, MEM_HEAD_DIM), mv.reshape(B, N, MEM_HEADS, MEM_HEAD_DIM)


def nsa_compress(k, w, pe):
    B, L, G, dh = k.shape
    nc = L // CMP_STRIDE
    n_cmp = nc - CMP_RATIO + 1
    chunks = k[:, :nc * CMP_STRIDE].reshape(B, nc, CMP_STRIDE, G, dh)
    w = w.reshape(CMP_RATIO, CMP_STRIDE, dh, dh)
    pe = pe.reshape(CMP_RATIO, CMP_STRIDE, 1, dh)
    return sum(jnp.einsum('bncgd,cde->bnge', chunks[:, r:r + n_cmp] + pe[r], w[r]) for r in range(CMP_RATIO))


def slc_blocks(k):
    B, L, G, dh = k.shape
    ns = -(-L // SLC_BLOCK)
    k = jnp.pad(k, ((0, 0), (0, ns * SLC_BLOCK - L), (0, 0), (0, 0)))
    return k.reshape(B, ns, SLC_BLOCK, G, dh).transpose(0, 3, 1, 2, 4)


def nsa_prepare(ck_raw, cv_raw, sk_raw, sv_raw, cmp_w, cmp_pe):
    L = ck_raw.shape[1]
    ck = nsa_compress(ck_raw, cmp_w[0], cmp_pe)
    cv = nsa_compress(cv_raw, cmp_w[1], cmp_pe)
    n_cmp = ck.shape[1]
    n_slc = -(-L // SLC_BLOCK)
    c_start = np.arange(n_cmp) * CMP_STRIDE
    s_start = np.arange(n_slc) * SLC_BLOCK
    overlap = ((c_start[:, None] < s_start[None, :] + SLC_BLOCK)
               & (c_start[:, None] + CMP_BLOCK > s_start[None, :]))
    c_end = jnp.asarray(c_start + CMP_BLOCK - 1, dtype=jnp.int32)
    return (ck, cv, c_end, jnp.asarray(overlap, dtype=jnp.float32), slc_blocks(sk_raw), slc_blocks(sv_raw))


def nsa_block(q, q_pos, gates, ck, cv, c_end, overlap, skb, svb, wk, wv, w_pos, slopes):
    B, Q = q.shape[:2]
    f32 = jnp.float32
    scale = HEAD_DIM ** -0.5
    qg = q.reshape(B, Q, NSA_KV, NSA_GROUP, HEAD_DIM)
    sl = slopes.reshape(1, NSA_KV, NSA_GROUP, 1, 1)
    dist = q_pos[:, None] - c_end[None, :]
    s = jnp.einsum('bqgrd,bngd->bgrqn', qg, ck, preferred_element_type=f32) * scale
    p_cmp = masked_softmax(s - sl * dist.astype(f32), dist >= 0)
    o_cmp = jnp.einsum('bgrqn,bngd->bqgrd', p_cmp.astype(cv.dtype), cv)
    n_slc = overlap.shape[1]
    imp = jnp.einsum('bgrqn,nj->bgqj', p_cmp, overlap)
    j = jnp.arange(n_slc)[None, :]
    cur = (q_pos // SLC_BLOCK)[:, None]
    forced = (j == 0) | (j == cur) | (j == cur - 1)
    score = jnp.where(j * SLC_BLOCK <= q_pos[:, None], jnp.where(forced, FORCE_SCORE, imp), -1.0)
    _, idx = lax.top_k(score, min(SLC_TOP, n_slc))
    n_top = idx.shape[-1]
    bi = jnp.arange(B)[:, None, None, None]
    gi = jnp.arange(NSA_KV)[None, :, None, None]
    kk = skb[bi, gi, idx].reshape(B, NSA_KV, Q, n_top * SLC_BLOCK, HEAD_DIM)
    vv = svb[bi, gi, idx].reshape(B, NSA_KV, Q, n_top * SLC_BLOCK, HEAD_DIM)
    kpos = (idx[..., None] * SLC_BLOCK + jnp.arange(SLC_BLOCK)).reshape(B, NSA_KV, 1, Q, n_top * SLC_BLOCK)
    dist = q_pos[:, None] - kpos
    s = jnp.einsum('bqgrd,bgqmd->bgrqm', qg, kk, preferred_element_type=f32) * scale
    p = masked_softmax(s - sl * dist.astype(f32), dist >= 0)
    o_slc = jnp.einsum('bgrqm,bgqmd->bqgrd', p.astype(vv.dtype), vv)
    dist = q_pos[:, None] - w_pos[None, :]
    s = jnp.einsum('bqgrd,bkgd->bgrqk', qg, wk, preferred_element_type=f32) * scale
    mask = (dist >= 0) & (dist <= WINDOW) & (w_pos >= 0)[None, :]
    p = masked_softmax(s - sl * dist.astype(f32), mask)
    o_win = jnp.einsum('bgrqk,bkgd->bqgrd', p.astype(wv.dtype), wv)
    g = gates.reshape(B, Q, NSA_KV, NSA_GROUP, 3, 1)
    o = g[..., 0, :] * o_cmp + g[..., 1, :] * o_slc + g[..., 2, :] * o_win
    return o.reshape(B, Q, NSA_W)


def nsa_prompt(q, gates, prep, wk, wv, slopes):
    B, T = q.shape[:2]
    ck, cv, c_end, overlap, skb, svb = prep
    pad = ((0, 0), (WINDOW, 0), (0, 0), (0, 0))
    wk_pad, wv_pad = jnp.pad(wk, pad), jnp.pad(wv, pad)
    nb = T // NSA_QBLK
    qb = q.reshape(B, nb, NSA_QBLK, NSA_HEADS, HEAD_DIM).swapaxes(0, 1)
    gb = gates.reshape(B, nb, NSA_QBLK, NSA_HEADS, 3).swapaxes(0, 1)

    def body(args):
        qi, gi, blk = args
        q0 = blk * NSA_QBLK
        q_pos = q0 + jnp.arange(NSA_QBLK)
        wki = lax.dynamic_slice_in_dim(wk_pad, q0, WINDOW + NSA_QBLK, axis=1)
        wvi = lax.dynamic_slice_in_dim(wv_pad, q0, WINDOW + NSA_QBLK, axis=1)
        w_pos = q0 - WINDOW + jnp.arange(WINDOW + NSA_QBLK)
        return nsa_block(qi, q_pos, gi, ck, cv, c_end, overlap, skb, svb, wki, wvi, w_pos, slopes)

    o = lax.map(body, (qb, gb, jnp.arange(nb)))
    return o.swapaxes(0, 1).reshape(B, T, NSA_W)


def sb_block(q, q_pos, k, v, k_pos):
    s = jnp.einsum('bqhd,bkhd->bhqk', q, k, preferred_element_type=jnp.float32) * (HEAD_DIM ** -0.5)
    mask = k_pos[None, :] < q_pos[:, None]
    log_1m = jnp.where(mask, jax.nn.log_sigmoid(-s), 0.0)
    after = lax.cumsum(log_1m, axis=log_1m.ndim - 1, reverse=True) - log_1m
    w = jnp.where(mask, jnp.exp(jax.nn.log_sigmoid(s) + after), 0.0)
    return jnp.einsum('bhqk,bkhd->bqhd', w.astype(v.dtype), v)


def sb_prompt(q, k, v):
    B, T = q.shape[:2]
    nb = T // SB_QBLK
    k_pos = jnp.arange(T)
    qb = q.reshape(B, nb, SB_QBLK, SB_HEADS, HEAD_DIM).swapaxes(0, 1)

    def body(args):
        qi, blk = args
        return sb_block(qi, blk * SB_QBLK + jnp.arange(SB_QBLK), k, v, k_pos)

    o = lax.map(body, (qb, jnp.arange(nb)))
    return o.swapaxes(0, 1).reshape(B, T, SB_W)


def mem_attend(q, mk, mv):
    B, Q = q.shape[:2]
    s = jnp.einsum('bqhd,bmhd->bhqm', q, mk, preferred_element_type=jnp.float32) * (MEM_HEAD_DIM ** -0.5)
    p = jax.nn.softmax(s, axis=-1)
    return jnp.einsum('bhqm,bmhd->bqhd', p.astype(mv.dtype), mv).reshape(B, Q, MEM_W)


def rg_lru(x, gate_in, conv_buf, h0, conv_w, conv_b, wa, ba, wx, bx, lam):
    B, T, _ = x.shape
    f32 = jnp.float32
    xp = jnp.concatenate([conv_buf, x], axis=1)
    xc = sum(xp[:, k:k + T] * conv_w[k] for k in range(CONV_W)) + conv_b
    xb = xc.reshape(B, T, LRU_BLOCKS, LRU_BW)
    r = jax.nn.sigmoid(jnp.einsum('btnc,ncd->btnd', xb, wa) + ba).reshape(B, T, D_RNN)
    i = jax.nn.sigmoid(jnp.einsum('btnc,ncd->btnd', xb, wx) + bx).reshape(B, T, D_RNN)
    log_a = (-LRU_C * jax.nn.softplus(-lam.astype(f32))) * r.astype(f32)
    a = jnp.exp(log_a)
    b = jnp.sqrt(-jnp.expm1(2.0 * log_a)) * (i * xc).astype(f32)
    b = b.at[:, 0].add(a[:, 0] * h0.astype(f32))

    def combine(lhs, rhs):
        a1, b1 = lhs
        a2, b2 = rhs
        return a1 * a2, a2 * b1 + b2

    _, h = lax.associative_scan(combine, (a, b), axis=1)
    y = h.astype(x.dtype) * jax.nn.gelu(gate_in)
    return y, xp[:, T:], h[:, -1].astype(x.dtype)


def merge_branches(branches, gate_logits, w_branch, w_out):
    gates = jax.nn.sigmoid(gate_logits).reshape(*gate_logits.shape[:-1], N_BRANCH, D_MODEL)
    offs = [0] + [int(c) for c in np.cumsum(BRANCH_W)]
    acc = sum(gates[..., n, :] * (branches[n] @ w_branch[offs[n]:offs[n + 1]]) for n in range(N_BRANCH))
    return acc @ w_out


def swiglu(x, w13, w2):
    a, b = jnp.split(x @ w13, 2, axis=-1)
    return (jax.nn.silu(a) * b) @ w2


def moe_swiglu(x, router_w, router_b, w13, w2):
    logits = (x @ router_w + router_b).astype(jnp.float32)
    probs = jax.nn.softmax(logits, axis=-1)
    top_p, top_i = lax.top_k(probs, TOP_K)
    top_p = top_p / jnp.sum(top_p, axis=-1, keepdims=True)
    combine = jnp.einsum('...k,...ke->...e', top_p, jax.nn.one_hot(top_i, N_EXP, dtype=jnp.float32))
    return sum(combine[..., e:e + 1].astype(x.dtype) * swiglu(x, w13[e], w2[e]) for e in range(N_EXP))


def mixer_prompt(x, mem, lw, slopes):
    (w_in, cmp_w, cmp_pe, conv_w, conv_b, wa, ba, wx, bx, lam, w_mem_kv, w_branch, w_out) = lw
    B, T, _ = x.shape
    (nq, ck, cv, sk, sv, wk, wv, gates, lx, lg, sq, sbk, sbv, mq, mg) = project(x, w_in)
    o_nsa = nsa_prompt(nq, gates, nsa_prepare(ck, cv, sk, sv, cmp_w, cmp_pe), wk, wv, slopes)
    conv0 = jnp.zeros((B, CONV_W - 1, D_RNN), x.dtype)
    h0 = jnp.zeros((B, D_RNN), x.dtype)
    o_lru, conv_new, h_new = rg_lru(lx, lg, conv0, h0, conv_w, conv_b, wa, ba, wx, bx, lam)
    o_sb = sb_prompt(sq, sbk, sbv)
    mk, mv = mem_kv(mem, w_mem_kv)
    o_mem = mem_attend(mq, mk, mv)
    y = merge_branches((o_nsa, o_lru, o_sb, o_mem), mg, w_branch, w_out)
    keep = min(WINDOW, T)
    return y, (ck, cv, sk, sv, wk[:, T - keep:], wv[:, T - keep:], sbk, sbv, h_new, conv_new, mk, mv)


def mixer_sample(x, l, page_table, c_cmp_k, c_cmp_v, c_slc_k, c_slc_v, c_win_k, c_win_v, c_sb_k, c_sb_v,
                 c_mem_k, c_mem_v, s_lru_h, s_lru_conv, lw, slopes):
    (w_in, cmp_w, cmp_pe, conv_w, conv_b, wa, ba, wx, bx, lam, w_mem_kv, w_branch, w_out) = lw
    B, T, _ = x.shape
    past = page_table.shape[1] * PAGE_SIZE
    q_pos = past + jnp.arange(T)
    (nq, ck, cv, sk, sv, wk, wv, gates, lx, lg, sq, sbk, sbv, mq, mg) = project(x, w_in)

    def full(pool, new):
        rows = pool[l, page_table].reshape(B, past, *pool.shape[3:])
        return jnp.concatenate([rows, new], axis=1)

    prep = nsa_prepare(full(c_cmp_k, ck), full(c_cmp_v, cv), full(c_slc_k, sk), full(c_slc_v, sv), cmp_w, cmp_pe)
    win_buf = c_win_k.shape[2]
    wk_all = jnp.concatenate([c_win_k[l], wk], axis=1)
    wv_all = jnp.concatenate([c_win_v[l], wv], axis=1)
    w_pos = past - win_buf + jnp.arange(win_buf + T)
    o_nsa = nsa_block(nq, q_pos, gates, *prep, wk_all, wv_all, w_pos, slopes)
    o_lru, conv_new, h_new = rg_lru(lx, lg, s_lru_conv[l], s_lru_h[l], conv_w, conv_b, wa, ba, wx, bx, lam)
    o_sb = sb_block(sq, q_pos, full(c_sb_k, sbk), full(c_sb_v, sbv), jnp.arange(past + T)).reshape(B, T, SB_W)
    o_mem = mem_attend(mq, c_mem_k[l], c_mem_v[l])
    y = merge_branches((o_nsa, o_lru, o_sb, o_mem), mg, w_branch, w_out)
    return y, (ck, cv, sk, sv, wk, wv, sbk, sbv, h_new, conv_new)


def channel_block(x, mix, l, ln1_g, ln1_b, ln2_g, ln2_b, ffn_w13, ffn_w2,
                  moe_router_w, moe_router_b, moe_w13, moe_w2):
    x = layer_norm(ALPHA * x + mix, ln1_g[l], ln1_b[l])
    if l % 2 == 0:
        f = swiglu(x, ffn_w13[l // 2], ffn_w2[l // 2])
    else:
        f = moe_swiglu(x, moe_router_w[l // 2], moe_router_b[l // 2], moe_w13[l // 2], moe_w2[l // 2])
    return layer_norm(ALPHA * x + f, ln2_g[l], ln2_b[l])


def setup_inputs(seed: int = 0) -> dict:
    key = jax.random.key(seed)
    keys = jax.random.split(key, 64)
    counter = iter(range(64))

    def nrm(shape, scale):
        return jax.random.normal(keys[next(counter)], shape, jnp.float32) * scale

    n_pages = PAST_LEN // PAGE_SIZE
    n_pool = (DEC_BATCH * n_pages * 5) // 4
    win_buf = min(WINDOW, PAST_LEN)
    perm = jax.random.permutation(keys[next(counter)], n_pool)
    page_table = perm[:DEC_BATCH * n_pages].reshape(DEC_BATCH, n_pages).astype(jnp.int32)
    u = jax.random.uniform(keys[next(counter)], (DEPTH, D_RNN), jnp.float32, minval=0.9, maxval=0.999)
    a_base = u ** (1.0 / LRU_C)
    lru_lambda = jnp.log(a_base) - jnp.log1p(-a_base)
    branch_scale = jnp.concatenate([jnp.full((w, 1), w ** -0.5, jnp.float32) for w in BRANCH_W], axis=0)
    return {
        'x_prompt': nrm((BATCH, SEQ, D_MODEL), 1.0),
        'x_sample': nrm((DEC_BATCH, DEC_SEQ, D_MODEL), 1.0),
        'mem_prompt': nrm((BATCH, N_MEM, D_MODEL), 1.0),
        'cache_nsa_cmp_k': nrm((DEPTH, n_pool, PAGE_SIZE, NSA_KV, HEAD_DIM), 1.0),
        'cache_nsa_cmp_v': nrm((DEPTH, n_pool, PAGE_SIZE, NSA_KV, HEAD_DIM), 1.0),
        'cache_nsa_slc_k': nrm((DEPTH, n_pool, PAGE_SIZE, NSA_KV, HEAD_DIM), 1.0),
        'cache_nsa_slc_v': nrm((DEPTH, n_pool, PAGE_SIZE, NSA_KV, HEAD_DIM), 1.0),
        'cache_nsa_win_k': nrm((DEPTH, DEC_BATCH, win_buf, NSA_KV, HEAD_DIM), 1.0),
        'cache_nsa_win_v': nrm((DEPTH, DEC_BATCH, win_buf, NSA_KV, HEAD_DIM), 1.0),
        'cache_sb_k': nrm((DEPTH, n_pool, PAGE_SIZE, SB_HEADS, HEAD_DIM), 1.0),
        'cache_sb_v': nrm((DEPTH, n_pool, PAGE_SIZE, SB_HEADS, HEAD_DIM), 1.0),
        'cache_mem_k': nrm((DEPTH, DEC_BATCH, N_MEM, MEM_HEADS, MEM_HEAD_DIM), 1.0),
        'cache_mem_v': nrm((DEPTH, DEC_BATCH, N_MEM, MEM_HEADS, MEM_HEAD_DIM), 1.0),
        'state_lru_h': nrm((DEPTH, DEC_BATCH, D_RNN), 0.5),
        'state_lru_conv': nrm((DEPTH, DEC_BATCH, CONV_W - 1, D_RNN), 1.0),
        'page_table': page_table,
        'w_in': nrm((DEPTH, D_MODEL, IN_W), D_MODEL ** -0.5),
        'nsa_cmp_w': nrm((DEPTH, 2, CMP_BLOCK, HEAD_DIM, HEAD_DIM), (CMP_BLOCK * HEAD_DIM) ** -0.5),
        'nsa_cmp_pe': nrm((DEPTH, CMP_BLOCK, HEAD_DIM), 0.1),
        'lru_conv_w': nrm((DEPTH, CONV_W, D_RNN), CONV_W ** -0.5),
        'lru_conv_b': nrm((DEPTH, D_RNN), 0.01),
        'lru_wa': nrm((DEPTH, LRU_BLOCKS, LRU_BW, LRU_BW), LRU_BW ** -0.5),
        'lru_ba': nrm((DEPTH, LRU_BLOCKS, LRU_BW), 0.01),
        'lru_wx': nrm((DEPTH, LRU_BLOCKS, LRU_BW, LRU_BW), LRU_BW ** -0.5),
        'lru_bx': nrm((DEPTH, LRU_BLOCKS, LRU_BW), 0.01),
        'lru_lambda': lru_lambda,
        'w_mem_kv': nrm((DEPTH, D_MODEL, 2 * MEM_W), D_MODEL ** -0.5),
        'w_branch': nrm((DEPTH, MIX_W, D_MODEL), 1.0) * branch_scale * BETA,
        'w_out': nrm((DEPTH, D_MODEL, D_MODEL), (D_MODEL ** -0.5) * BETA),
        'ln1_g': 1.0 + nrm((DEPTH, D_MODEL), 0.02),
        'ln1_b': nrm((DEPTH, D_MODEL), 0.02),
        'ln2_g': 1.0 + nrm((DEPTH, D_MODEL), 0.02),
        'ln2_b': nrm((DEPTH, D_MODEL), 0.02),
        'ffn_w13': nrm((N_DENSE, D_MODEL, 2 * D_FF), D_MODEL ** -0.5),
        'ffn_w2': nrm((N_DENSE, D_FF, D_MODEL), (D_FF ** -0.5) * BETA),
        'moe_router_w': nrm((N_MOE, D_MODEL, N_EXP), D_MODEL ** -0.5),
        'moe_router_b': nrm((N_MOE, N_EXP), 0.01),
        'moe_w13': nrm((N_MOE, N_EXP, D_MODEL, 2 * D_EXP), D_MODEL ** -0.5),
        'moe_w2': nrm((N_MOE, N_EXP, D_EXP, D_MODEL), (D_EXP ** -0.5) * BETA),
    }


def reference(x_prompt, x_sample, mem_prompt, cache_nsa_cmp_k, cache_nsa_cmp_v, cache_nsa_slc_k, cache_nsa_slc_v,
              cache_nsa_win_k, cache_nsa_win_v, cache_sb_k, cache_sb_v, cache_mem_k, cache_mem_v,
              state_lru_h, state_lru_conv, page_table, w_in, nsa_cmp_w, nsa_cmp_pe, lru_conv_w, lru_conv_b,
              lru_wa, lru_ba, lru_wx, lru_bx, lru_lambda, w_mem_kv, w_branch, w_out, ln1_g, ln1_b, ln2_g, ln2_b,
              ffn_w13, ffn_w2, moe_router_w, moe_router_b, moe_w13, moe_w2):
    slopes = alibi_slopes(NSA_HEADS)
    xp, xs = x_prompt, x_sample
    p_layers, s_layers = [], []
    for l in range(DEPTH):
        lw = (w_in[l], nsa_cmp_w[l], nsa_cmp_pe[l], lru_conv_w[l], lru_conv_b[l], lru_wa[l], lru_ba[l],
              lru_wx[l], lru_bx[l], lru_lambda[l], w_mem_kv[l], w_branch[l], w_out[l])
        mix_p, st_p = mixer_prompt(xp, mem_prompt, lw, slopes)
        mix_s, st_s = mixer_sample(xs, l, page_table, cache_nsa_cmp_k, cache_nsa_cmp_v, cache_nsa_slc_k,
                                   cache_nsa_slc_v, cache_nsa_win_k, cache_nsa_win_v, cache_sb_k, cache_sb_v,
                                   cache_mem_k, cache_mem_v, state_lru_h, state_lru_conv, lw, slopes)
        xp = channel_block(xp, mix_p, l, ln1_g, ln1_b, ln2_g, ln2_b, ffn_w13, ffn_w2,
                           moe_router_w, moe_router_b, moe_w13, moe_w2)
        xs = channel_block(xs, mix_s, l, ln1_g, ln1_b, ln2_g, ln2_b, ffn_w13, ffn_w2,
                           moe_router_w, moe_router_b, moe_w13, moe_w2)
        p_layers.append(st_p)
        s_layers.append(st_s)
    prompt_state = [jnp.stack(rows, axis=0) for rows in zip(*p_layers)]
    sample_state = [jnp.stack(rows, axis=0) for rows in zip(*s_layers)]
    return (xp, xs, *prompt_state, *sample_state)
```

```python
import functools

import numpy as np
import jax
import jax.numpy as jnp
from jax import lax
from jax.experimental import pallas as pl
from jax.experimental.pallas import tpu as pltpu

F32 = jnp.float32
BF16 = jnp.bfloat16

D_MODEL = 1024
HEAD_DIM = 64
NSA_HEADS = 8
NSA_KV = 2
NSA_GROUP = NSA_HEADS // NSA_KV
CMP_STRIDE = 16
CMP_BLOCK = 32
SLC_BLOCK = 64
SLC_TOP = 16
WINDOW = 512
FORCE_SCORE = 1.0e4
SB_HEADS = 8
D_RNN = 512
LRU_BLOCKS = 8
CONV_W = 4
LRU_C = 8.0
MEM_HEADS = 4
MEM_HEAD_DIM = 128
N_EXP = 8
LN_EPS = 1e-5
NEG = -1.0e30
TINY = 1.0e-30
KV_W = NSA_KV * HEAD_DIM
SB_W = SB_HEADS * HEAD_DIM
LANES = 128
VMEM_LIMIT = 56 * 1024 * 1024
SLOPES = tuple(float(2.0 ** (-8.0 * (h + 1) / NSA_HEADS)) for h in range(NSA_HEADS))


def _cparams(sem, big=False):
    return pltpu.CompilerParams(dimension_semantics=sem, vmem_limit_bytes=VMEM_LIMIT if big else None)


def _dot(a, b):
    return jnp.dot(a, b, preferred_element_type=F32)


def _dot_nt(a, b):
    return lax.dot_general(a, b, (((1,), (1,)), ((), ())), preferred_element_type=F32)


def _split2(x):
    hi = x.astype(BF16)
    lo = (x - hi.astype(F32)).astype(BF16)
    return hi, lo


def _split3(x):
    hi = x.astype(BF16)
    r = x - hi.astype(F32)
    mid = r.astype(BF16)
    lo = (r - mid.astype(F32)).astype(BF16)
    return hi, mid, lo


def _softplus(x):
    return jnp.maximum(x, 0.0) + jnp.log1p(jnp.exp(-jnp.abs(x)))


def _layer_norm(z, g, b):
    mu = jnp.mean(z, axis=-1, keepdims=True)
    zc = z - mu
    var = jnp.mean(zc * zc, axis=-1, keepdims=True)
    return zc * lax.rsqrt(var + LN_EPS) * g + b


def _masked_softmax(s, mask):
    s = jnp.where(mask, s, NEG)
    m = jnp.max(s, axis=-1, keepdims=True)
    e = jnp.where(mask, jnp.exp(s - m), 0.0)
    return e / jnp.maximum(jnp.sum(e, axis=-1, keepdims=True), TINY)


def _topk_mask(score, lanef, k):
    sel = jnp.zeros_like(score)
    s = score
    for _ in range(k):
        m = jnp.max(s, axis=-1, keepdims=True)
        jm = jnp.min(jnp.where(s == m, lanef, 1.0e9), axis=-1, keepdims=True)
        hit = lanef == jm
        sel = jnp.where(hit, 1.0, sel)
        s = jnp.where(hit, -3.0, s)
    return sel


def _proj_body(*refs, ncols, tcols, has_t):
    x_ref, wn_ref = refs[0], refs[1]
    o_refs = refs[3:] if has_t else refs[2:]
    xb = x_ref[...].astype(BF16)
    k = 0
    for c0, wd in ncols:
        o_refs[k][...] = _dot(xb, wn_ref[:, c0:c0 + wd]).astype(o_refs[k].dtype)
        k += 1
    if has_t:
        wt_ref = refs[2]
        for c0, wd, kinds in tcols:
            r = _dot_nt(wt_ref[c0:c0 + wd, :], xb)
            for kind in kinds:
                if kind == "full":
                    o_refs[k][...] = r
                else:
                    rb = r.astype(BF16)
                    for c in range(rb.shape[1] // LANES):
                        o_refs[k][c] = rb[:, c * LANES:(c + 1) * LANES]
                k += 1


def _proj(x, wn, ncols, wt=None, tcols=(), tm=512):
    B, T, K = x.shape
    tm = min(tm, T)
    has_t = wt is not None
    in_specs = [pl.BlockSpec((None, tm, K), lambda b, t: (b, t, 0)),
                pl.BlockSpec(wn.shape, lambda b, t: (0, 0))]
    args = [x, wn]
    if has_t:
        in_specs.append(pl.BlockSpec(wt.shape, lambda b, t: (0, 0)))
        args.append(wt)
    out_specs, out_shape = [], []
    for _, wd in ncols:
        out_specs.append(pl.BlockSpec((None, tm, wd), lambda b, t: (b, t, 0)))
        out_shape.append(jax.ShapeDtypeStruct((B, T, wd), F32))
    for _, wd, kinds in tcols:
        for kind in kinds:
            if kind == "full":
                out_specs.append(pl.BlockSpec((None, wd, tm), lambda b, t: (b, 0, t)))
                out_shape.append(jax.ShapeDtypeStruct((B, wd, T), F32))
            else:
                nb = tm // LANES
                out_specs.append(pl.BlockSpec((None, nb, wd, LANES), lambda b, t: (b, t, 0, 0)))
                out_shape.append(jax.ShapeDtypeStruct((B, T // LANES, wd, LANES), BF16))
    return pl.pallas_call(
        functools.partial(_proj_body, ncols=tuple(ncols), tcols=tuple(tcols), has_t=has_t),
        grid=(B, T // tm), in_specs=in_specs, out_specs=out_specs, out_shape=out_shape,
        compiler_params=_cparams(("parallel", "parallel"), big=True), name="proj")(*args)


def _compress_body(xk_ref, xv_ref, wk_ref, wv_ref, pe_ref, ok_ref, ov_ref):
    n = xk_ref.shape[0]
    colid = lax.broadcasted_iota(jnp.int32, (KV_W, n), 1)
    for x_ref, w_ref, o_ref in ((xk_ref, wk_ref, ok_ref), (xv_ref, wv_ref, ov_ref)):
        x = x_ref[...]
        a = _dot_nt(w_ref[0], (x + pe_ref[0]).astype(BF16))
        b = _dot_nt(w_ref[1], (x + pe_ref[1]).astype(BF16))
        o_ref[...] = jnp.where(colid < n - 1, a + pltpu.roll(b, n - 1, 1), 0.0)


def _compress(xk, xv, wk, wv, pe):
    B, n, cw = xk.shape
    xspec = pl.BlockSpec((None, n, cw), lambda b: (b, 0, 0))
    wspec = pl.BlockSpec(wk.shape, lambda b: (0, 0, 0))
    ospec = pl.BlockSpec((None, KV_W, n), lambda b: (b, 0, 0))
    return pl.pallas_call(
        _compress_body, grid=(B,),
        in_specs=[xspec, xspec, wspec, wspec, pl.BlockSpec(pe.shape, lambda b: (0, 0, 0))],
        out_specs=[ospec, ospec], out_shape=[jax.ShapeDtypeStruct((B, KV_W, n), F32)] * 2,
        compiler_params=_cparams(("parallel",), big=True), name="compress")(xk, xv, wk, wv, pe)


def _compress_paged_body(pt_ref, *refs, P):
    x_refs = refs[:P]
    w_ref, pe_ref, a_ref, b_ref = refs[P:]
    x = jnp.concatenate([r[...] for r in x_refs], axis=0)
    a_ref[...] = _dot((x + pe_ref[0]).astype(BF16), w_ref[0])
    b_ref[...] = _dot((x + pe_ref[1]).astype(BF16), w_ref[1])


def _compress_paged(cache, layer, page_table, w, pe, P):
    Bd, NP = page_table.shape
    cw = cache.shape[-1]
    rows = cache.shape[-2]
    in_specs = [pl.BlockSpec((None, None, rows, cw), lambda b, s, pt, i=i: (layer, pt[b, s * P + i], 0, 0))
                for i in range(P)]
    in_specs += [pl.BlockSpec(w.shape, lambda b, s, pt: (0, 0, 0)), pl.BlockSpec(pe.shape, lambda b, s, pt: (0, 0, 0))]
    ospec = pl.BlockSpec((None, rows * P, KV_W), lambda b, s, pt: (b, s, 0))
    grid_spec = pltpu.PrefetchScalarGridSpec(num_scalar_prefetch=1, grid=(Bd, NP // P), in_specs=in_specs,
                                             out_specs=[ospec, ospec])
    return pl.pallas_call(
        functools.partial(_compress_paged_body, P=P), grid_spec=grid_spec,
        out_shape=[jax.ShapeDtypeStruct((Bd, rows * NP, KV_W), F32)] * 2,
        compiler_params=_cparams(("parallel", "parallel"), big=True), name="compress_paged")(
            page_table, *([cache] * P), w, pe)


def _nsa_prompt_body(q_ref, ng_ref, ckt_ref, cvt_ref, sk_ref, sv_ref, wk_ref, wv_ref, ovl_ref, o_ref,
                     *, Q, KBN, n_slc):
    i = pl.program_id(1)
    q0 = i * Q
    KB = KBN * LANES
    QB = Q // LANES
    lane = lax.broadcasted_iota(jnp.int32, (Q, LANES), 1)
    lanef = lane.astype(F32)
    lo_half = lane < HEAD_DIM
    qrow = lax.broadcasted_iota(jnp.int32, (Q, 1), 0)
    qpos = q0 + qrow
    qpos4 = jnp.concatenate([qpos] * NSA_GROUP, axis=0)
    gate = jax.nn.sigmoid(ng_ref[...])
    slabs = [q_ref[:, LANES * s:LANES * (s + 1)] * (HEAD_DIM ** -0.5) for s in range(4)]

    def placed(h, g):
        sl = slabs[h // 2]
        if (h % 2) != g:
            sl = pltpu.roll(sl, HEAD_DIM, 1)
        return jnp.where(lo_half if g == 0 else jnp.logical_not(lo_half), sl, 0.0)

    ckt = ckt_ref[...].astype(BF16)
    cvt = cvt_ref[...].astype(BF16)
    NC = ckt.shape[1]
    cend = CMP_STRIDE * lax.broadcasted_iota(jnp.int32, (1, NC), 1) + (CMP_BLOCK - 1)
    dist_c = qpos4 - cend
    vis_c = dist_c >= 0
    dist_cf = dist_c.astype(F32)
    cur = qpos >> 6
    valid = (lane * SLC_BLOCK <= qpos) & (lane < n_slc)
    forced = (lane == 0) | (lane == cur) | (lane == cur - 1)
    outs = [None] * NSA_HEADS
    for g in range(NSA_KV):
        qg = jnp.concatenate([placed(NSA_GROUP * g + r, g) for r in range(NSA_GROUP)], axis=0).astype(BF16)
        slope4 = jnp.concatenate([jnp.full((Q, 1), SLOPES[NSA_GROUP * g + r], F32) for r in range(NSA_GROUP)], axis=0)
        p = _masked_softmax(_dot(qg, ckt) - slope4 * dist_cf, vis_c)
        o_cmp = _dot_nt(p.astype(BF16), cvt)
        psum = (p[0:Q] + p[Q:2 * Q]) + (p[2 * Q:3 * Q] + p[3 * Q:4 * Q])
        ph, pl_ = _split2(psum)
        imp = _dot(ph, ovl_ref[...]) + _dot(pl_, ovl_ref[...])
        score = jnp.where(valid, jnp.where(forced, FORCE_SCORE, imp), jnp.where(lane < n_slc, -1.0, -2.0))
        sel_b = _topk_mask(score, lanef, SLC_TOP).astype(BF16)
        blk_row = lax.broadcasted_iota(jnp.int32, (LANES, KB), 0)

        def kt_body(t, carry):
            m, l, acc = carry
            kt = jnp.concatenate([sk_ref[t * KBN + c] for c in range(KBN)], axis=1)
            vt = jnp.concatenate([sv_ref[t * KBN + c] for c in range(KBN)], axis=1)
            kpos = t * KB + lax.broadcasted_iota(jnp.int32, (1, KB), 1)
            expand = ((kpos >> 6) == blk_row).astype(BF16)
            selx = _dot(sel_b, expand)
            selx4 = jnp.concatenate([selx] * NSA_GROUP, axis=0)
            dist = qpos4 - kpos
            mask = (selx4 > 0.5) & (dist >= 0)
            s = jnp.where(mask, _dot(qg, kt) - slope4 * dist.astype(F32), NEG)
            m_new = jnp.maximum(m, jnp.max(s, axis=-1, keepdims=True))
            alpha = jnp.exp(m - m_new)
            e = jnp.where(mask, jnp.exp(s - m_new), 0.0)
            l = alpha * l + jnp.sum(e, axis=-1, keepdims=True)
            acc = alpha * acc + _dot_nt(e.astype(BF16), vt)
            return m_new, l, acc

        nkt = (q0 + Q + KB - 1) // KB
        init = (jnp.full((NSA_GROUP * Q, 1), NEG, F32), jnp.zeros((NSA_GROUP * Q, 1), F32),
                jnp.zeros((NSA_GROUP * Q, LANES), F32))
        _, l_s, acc_s = lax.fori_loop(0, nkt, kt_body, init)
        o_slc = acc_s / jnp.maximum(l_s, TINY)
        sb0 = jnp.maximum(i * QB - WINDOW // LANES, 0)
        nwb = WINDOW // LANES + QB
        kt = jnp.concatenate([wk_ref[sb0 + c] for c in range(nwb)], axis=1)
        vt = jnp.concatenate([wv_ref[sb0 + c] for c in range(nwb)], axis=1)
        kpos = sb0 * LANES + lax.broadcasted_iota(jnp.int32, (1, nwb * LANES), 1)
        dist = qpos4 - kpos
        pw = _masked_softmax(_dot(qg, kt) - slope4 * dist.astype(F32), (dist >= 0) & (dist <= WINDOW))
        o_win = _dot_nt(pw.astype(BF16), vt)
        for r in range(NSA_GROUP):
            h = NSA_GROUP * g + r
            rows = slice(r * Q, (r + 1) * Q)
            oh = (gate[:, 3 * h:3 * h + 1] * o_cmp[rows] + gate[:, 3 * h + 1:3 * h + 2] * o_slc[rows]
                  + gate[:, 3 * h + 2:3 * h + 3] * o_win[rows])
            if (h % 2) != g:
                oh = pltpu.roll(oh, HEAD_DIM, 1)
            outs[h] = oh
    for s in range(4):
        o_ref[:, LANES * s:LANES * (s + 1)] = jnp.where(lo_half, outs[2 * s], outs[2 * s + 1])


def _nsa_prompt(q, ng, ckt, cvt, skb, svb, wkb, wvb, ovl, Q=128, KBN=4):
    B, T, _ = q.shape
    n_slc = T // SLC_BLOCK
    NC = ckt.shape[2]
    nblk = T // LANES
    kvspec = pl.BlockSpec((None, nblk, KV_W, LANES), lambda b, i: (b, 0, 0, 0))
    cspec = pl.BlockSpec((None, KV_W, NC), lambda b, i: (b, 0, 0))
    return pl.pallas_call(
        functools.partial(_nsa_prompt_body, Q=Q, KBN=KBN, n_slc=n_slc), grid=(B, T // Q),
        in_specs=[pl.BlockSpec((None, Q, NSA_HEADS * HEAD_DIM), lambda b, i: (b, i, 0)),
                  pl.BlockSpec((None, Q, LANES), lambda b, i: (b, i, 0)),
                  cspec, cspec, kvspec, kvspec, kvspec, kvspec,
                  pl.BlockSpec(ovl.shape, lambda b, i: (0, 0))],
        out_specs=pl.BlockSpec((None, Q, NSA_HEADS * HEAD_DIM), lambda b, i: (b, i, 0)),
        out_shape=jax.ShapeDtypeStruct((B, T, NSA_HEADS * HEAD_DIM), F32),
        compiler_params=_cparams(("parallel", "parallel"), big=True), name="nsa_prompt")(
            q, ng, ckt, cvt, skb, svb, wkb, wvb, ovl)


def _sb_block(qh, kt, vt, u, carry, acc, mask):
    n = qh.shape[0]
    s = _dot(qh, kt)
    sp = _softplus(s)
    l1m = -sp if mask is None else jnp.where(mask, -sp, 0.0)
    hi, lo = _split2(l1m)
    r2 = _dot(jnp.concatenate([hi, lo], axis=0), u)
    after = r2[:n] + r2[n:]
    w = jnp.exp((s - sp) + after + carry)
    if mask is not None:
        w = jnp.where(mask, w, 0.0)
    acc = acc + _dot_nt(w.astype(BF16), vt)
    carry = carry + jnp.sum(l1m, axis=-1, keepdims=True)
    return carry, acc


def _sb_prompt_body(q_ref, kt_ref, vt_ref, u_ref, o_ref, *, Q):
    i = pl.program_id(1)
    lane = lax.broadcasted_iota(jnp.int32, (Q, LANES), 1)
    tri = lax.broadcasted_iota(jnp.int32, (Q, Q), 1) < lax.broadcasted_iota(jnp.int32, (Q, Q), 0)
    u = u_ref[...]
    for s in range(SB_W // LANES):
        qs = q_ref[:, LANES * s:LANES * (s + 1)] * (HEAD_DIM ** -0.5)
        rs = slice(LANES * s, LANES * (s + 1))
        res = []
        for half in range(2):
            hm = (lane < HEAD_DIM) if half == 0 else (lane >= HEAD_DIM)
            qh = jnp.where(hm, qs, 0.0).astype(BF16)
            carry, acc = _sb_block(qh, kt_ref[i, rs, :], vt_ref[i, rs, :], u,
                                   jnp.zeros((Q, 1), F32), jnp.zeros((Q, LANES), F32), tri)

            def body(jj, c, qh=qh, rs=rs):
                j = i - 1 - jj
                return _sb_block(qh, kt_ref[j, rs, :], vt_ref[j, rs, :], u, c[0], c[1], None)

            carry, acc = lax.fori_loop(0, i, body, (carry, acc))
            res.append(acc)
        o_ref[:, rs] = jnp.where(lane < HEAD_DIM, res[0], res[1])


def _sb_prompt(q, ktb, vtb, u):
    B, T, _ = q.shape
    Q = LANES
    nblk = T // LANES
    kvspec = pl.BlockSpec((None, nblk, SB_W, LANES), lambda b, i: (b, 0, 0, 0))
    return pl.pallas_call(
        functools.partial(_sb_prompt_body, Q=Q), grid=(B, T // Q),
        in_specs=[pl.BlockSpec((None, Q, SB_W), lambda b, i: (b, i, 0)), kvspec, kvspec,
                  pl.BlockSpec(u.shape, lambda b, i: (0, 0))],
        out_specs=pl.BlockSpec((None, Q, SB_W), lambda b, i: (b, i, 0)),
        out_shape=jax.ShapeDtypeStruct((B, T, SB_W), F32),
        compiler_params=_cparams(("parallel", "parallel"), big=True), name="sb_prompt")(q, ktb, vtb, u)


def _mem_body(q_ref, k_ref, v_ref, o_ref):
    for h in range(MEM_HEADS):
        sl = slice(MEM_HEAD_DIM * h, MEM_HEAD_DIM * (h + 1))
        s = _dot_nt(q_ref[:, sl].astype(BF16), k_ref[:, sl].astype(BF16)) * (MEM_HEAD_DIM ** -0.5)
        e = jnp.exp(s - jnp.max(s, axis=-1, keepdims=True))
        p = e / jnp.sum(e, axis=-1, keepdims=True)
        o_ref[:, sl] = _dot(p.astype(BF16), v_ref[:, sl].astype(BF16))


def _mem_attn(q, mk, mv, tq=512):
    B, T, W = q.shape
    tq = min(tq, T)
    N = mk.shape[1]
    kvspec = pl.BlockSpec((None, N, W), lambda b, i: (b, 0, 0))
    return pl.pallas_call(
        _mem_body, grid=(B, T // tq),
        in_specs=[pl.BlockSpec((None, tq, W), lambda b, i: (b, i, 0)), kvspec, kvspec],
        out_specs=pl.BlockSpec((None, tq, W), lambda b, i: (b, i, 0)),
        out_shape=jax.ShapeDtypeStruct((B, T, W), F32),
        compiler_params=_cparams(("parallel", "parallel")), name="mem_attn")(q, mk, mv)


def _rglru_body(x_ref, g_ref, cb_ref, h0_ref, cw_ref, cbias_ref, wbd_ref, gb_ref, lam_ref,
                y_ref, hn_ref, cn_ref, xbuf, a_s, b_s, h_s, hc, *, n_t):
    t = pl.program_id(1)
    tt = x_ref.shape[0]
    hist = CONV_W - 1

    @pl.when(t == 0)
    def _():
        xbuf[8 - hist:8, :] = cb_ref[...]
        hc[...] = h0_ref[...]

    xbuf[8:8 + tt, :] = x_ref[...]
    xc = cw_ref[0:1, :] * xbuf[8 - hist:8 - hist + tt, :]
    for k in range(1, CONV_W):
        xc = xc + cw_ref[k:k + 1, :] * xbuf[8 - hist + k:8 - hist + k + tt, :]
    xc = xc + cbias_ref[...]
    gates = _dot(xc.astype(BF16), wbd_ref[...]) + gb_ref[...]
    r = jax.nn.sigmoid(gates[:, :D_RNN])
    ig = jax.nn.sigmoid(gates[:, D_RNN:])
    log_a = (-LRU_C * _softplus(-lam_ref[...])) * r
    a = jnp.exp(log_a)
    a_s[...] = a
    b_s[...] = jnp.sqrt(-jnp.tanh(log_a) * (a * a + 1.0)) * (ig * xc)

    def step(k, h):
        h = a_s[pl.ds(k, 1), :] * h + b_s[pl.ds(k, 1), :]
        h_s[pl.ds(k, 1), :] = h
        return h

    h = lax.fori_loop(0, tt, step, hc[...], unroll=8 if tt % 8 == 0 else tt)
    hc[...] = h
    y_ref[...] = h_s[...] * jax.nn.gelu(g_ref[...])
    hn_ref[...] = h
    cn_ref[...] = xbuf[8 + tt - hist:8 + tt, :]
    if n_t > 1:
        xbuf[0:8, :] = xbuf[tt:tt + 8, :]


def _rglru(x, gate_in, conv_buf, h0, conv_w, conv_b, wbd, gbias, lam, tt=256):
    B, T, W = x.shape
    tt = min(tt, T)
    n_t = T // tt
    xspec = pl.BlockSpec((None, tt, W), lambda b, t: (b, t, 0))
    const = lambda a: pl.BlockSpec(a.shape, lambda b, t: (0,) * a.ndim)
    return pl.pallas_call(
        functools.partial(_rglru_body, n_t=n_t), grid=(B, n_t),
        in_specs=[xspec, xspec, pl.BlockSpec((None, CONV_W - 1, W), lambda b, t: (b, 0, 0)),
                  pl.BlockSpec((None, 1, W), lambda b, t: (b, 0, 0)),
                  const(conv_w), const(conv_b), const(wbd), const(gbias), const(lam)],
        out_specs=[xspec, pl.BlockSpec((None, 1, W), lambda b, t: (b, 0, 0)),
                   pl.BlockSpec((None, CONV_W - 1, W), lambda b, t: (b, 0, 0))],
        out_shape=[jax.ShapeDtypeStruct((B, T, W), F32), jax.ShapeDtypeStruct((B, 1, W), F32),
                   jax.ShapeDtypeStruct((B, CONV_W - 1, W), F32)],
        scratch_shapes=[pltpu.VMEM((tt + 8, W), F32), pltpu.VMEM((tt, W), F32), pltpu.VMEM((tt, W), F32),
                        pltpu.VMEM((tt, W), F32), pltpu.VMEM((1, W), F32)],
        compiler_params=_cparams(("parallel", "arbitrary")), name="rglru")(
            x, gate_in, conv_buf, h0, conv_w, conv_b, wbd, gbias, lam)


def _merge_body(x_ref, b0, b1, b2, b3, wmg_ref, wbr_ref, wo_ref, g_ref, be_ref, o_ref, *, alpha):
    x = x_ref[...]
    xb = x.astype(BF16)
    acc = None
    off = 0
    for n, br in enumerate((b0, b1, b2, b3)):
        wdt = br.shape[-1]
        gate = jax.nn.sigmoid(_dot(xb, wmg_ref[:, n * D_MODEL:(n + 1) * D_MODEL]))
        term = gate * _dot(br[...].astype(BF16), wbr_ref[off:off + wdt, :])
        acc = term if acc is None else acc + term
        off += wdt
    mix = _dot(acc.astype(BF16), wo_ref[...])
    o_ref[...] = _layer_norm(alpha * x + mix, g_ref[...], be_ref[...])


def _merge_ln(x, branches, wmg, wbr, wo, g, be, alpha, tm=256):
    M = x.shape[0]
    tm = min(tm, M)
    row = lambda w: pl.BlockSpec((tm, w), lambda i: (i, 0))
    const = lambda a: pl.BlockSpec(a.shape, lambda i: (0, 0))
    return pl.pallas_call(
        functools.partial(_merge_body, alpha=alpha), grid=(M // tm,),
        in_specs=[row(D_MODEL)] + [row(b.shape[1]) for b in branches] + [const(wmg), const(wbr), const(wo),
                                                                         const(g), const(be)],
        out_specs=row(D_MODEL), out_shape=jax.ShapeDtypeStruct((M, D_MODEL), F32),
        compiler_params=_cparams(("parallel",), big=True), name="merge_ln")(x, *branches, wmg, wbr, wo, g, be)


def _ffn_body(x_ref, wa_ref, wb_ref, w2_ref, g_ref, be_ref, o_ref, acc, *, alpha, nf):
    j = pl.program_id(1)
    x = x_ref[...]
    xb = x.astype(BF16)
    a = _dot(xb, wa_ref[...])
    b = _dot(xb, wb_ref[...])
    part = _dot(((a * jax.nn.sigmoid(a)) * b).astype(BF16), w2_ref[...])

    @pl.when(j == 0)
    def _():
        acc[...] = part

    @pl.when(j > 0)
    def _():
        acc[...] += part

    @pl.when(j == nf - 1)
    def _():
        o_ref[...] = _layer_norm(alpha * x + acc[...], g_ref[...], be_ref[...])


def _ffn_ln(x, w13, w2, g, be, alpha, tm=512, tf=1408):
    M = x.shape[0]
    tm = min(tm, M)
    F = w2.shape[0]
    nf = F // tf
    const = lambda a: pl.BlockSpec(a.shape, lambda i, j: (0, 0))
    return pl.pallas_call(
        functools.partial(_ffn_body, alpha=alpha, nf=nf), grid=(M // tm, nf),
        in_specs=[pl.BlockSpec((tm, D_MODEL), lambda i, j: (i, 0)),
                  pl.BlockSpec((D_MODEL, tf), lambda i, j: (0, j)),
                  pl.BlockSpec((D_MODEL, tf), lambda i, j: (0, j + nf)),
                  pl.BlockSpec((tf, D_MODEL), lambda i, j: (j, 0)), const(g), const(be)],
        out_specs=pl.BlockSpec((tm, D_MODEL), lambda i, j: (i, 0)),
        out_shape=jax.ShapeDtypeStruct((M, D_MODEL), F32),
        scratch_shapes=[pltpu.VMEM((tm, D_MODEL), F32)],
        compiler_params=_cparams(("parallel", "arbitrary"), big=True), name="ffn_ln")(x, w13, w13, w2, g, be)


def _moe_body(x_ref, rwh_ref, rwl_ref, rb_ref, wa_ref, wb_ref, w2_ref, g_ref, be_ref, o_ref, acc, comb,
              *, alpha, n_exp):
    e = pl.program_id(1)
    x = x_ref[...]
    xh = x.astype(BF16)
    tm = x.shape[0]
    lane = lax.broadcasted_iota(jnp.int32, (tm, LANES), 1)

    @pl.when(e == 0)
    def _():
        xl = (x - xh.astype(F32)).astype(BF16)
        logits = (_dot(xh, rwh_ref[...]) + _dot(xh, rwl_ref[...])) + _dot(xl, rwh_ref[...]) + rb_ref[...]
        valid = lane < n_exp
        lanef = lane.astype(F32)
        logits = jnp.where(valid, logits, NEG)
        ex = jnp.where(valid, jnp.exp(logits - jnp.max(logits, axis=-1, keepdims=True)), 0.0)
        p = ex / jnp.sum(ex, axis=-1, keepdims=True)
        p = jnp.where(valid, p, -1.0)
        m1 = jnp.max(p, axis=-1, keepdims=True)
        hit1 = lanef == jnp.min(jnp.where(p == m1, lanef, 1.0e9), axis=-1, keepdims=True)
        p2 = jnp.where(hit1, -1.0, p)
        m2 = jnp.max(p2, axis=-1, keepdims=True)
        hit2 = lanef == jnp.min(jnp.where(p2 == m2, lanef, 1.0e9), axis=-1, keepdims=True)
        den = m1 + m2
        comb[...] = jnp.where(hit1, m1 / den, 0.0) + jnp.where(hit2, m2 / den, 0.0)
        acc[...] = jnp.zeros_like(acc)

    w_e = jnp.sum(jnp.where(lane == e, comb[...], 0.0), axis=-1, keepdims=True)
    a = _dot(xh, wa_ref[...])
    b = _dot(xh, wb_ref[...])
    acc[...] += w_e * _dot(((a * jax.nn.sigmoid(a)) * b).astype(BF16), w2_ref[...])

    @pl.when(e == n_exp - 1)
    def _():
        o_ref[...] = _layer_norm(alpha * x + acc[...], g_ref[...], be_ref[...])


def _moe_ln(x, rwh, rwl, rb, w13, w2, g, be, alpha, tm=512):
    M = x.shape[0]
    tm = min(tm, M)
    n_exp, de, _ = w2.shape
    const = lambda a: pl.BlockSpec(a.shape, lambda i, e: (0, 0))
    return pl.pallas_call(
        functools.partial(_moe_body, alpha=alpha, n_exp=n_exp), grid=(M // tm, n_exp),
        in_specs=[pl.BlockSpec((tm, D_MODEL), lambda i, e: (i, 0)), const(rwh), const(rwl), const(rb),
                  pl.BlockSpec((None, D_MODEL, de), lambda i, e: (e, 0, 0)),
                  pl.BlockSpec((None, D_MODEL, de), lambda i, e: (e, 0, 1)),
                  pl.BlockSpec((None, de, D_MODEL), lambda i, e: (e, 0, 0)), const(g), const(be)],
        out_specs=pl.BlockSpec((tm, D_MODEL), lambda i, e: (i, 0)),
        out_shape=jax.ShapeDtypeStruct((M, D_MODEL), F32),
        scratch_shapes=[pltpu.VMEM((tm, D_MODEL), F32), pltpu.VMEM((tm, LANES), F32)],
        compiler_params=_cparams(("parallel", "arbitrary"), big=True), name="moe_ln")(
            x, rwh, rwl, rb, w13, w13, w2, g, be)


def _row_consts(rows, Td, past):
    row = lax.broadcasted_iota(jnp.int32, (rows, 1), 0)
    head = row // Td
    qpos = past + (row - head * Td)
    slope = jnp.zeros((rows, 1), F32)
    for h in range(NSA_HEADS):
        slope = jnp.where(head == h, SLOPES[h], slope)
    return qpos, slope


def _nsa_s_cmp_body(qz_ref, ak_ref, bk_ref, av_ref, bv_ref, ovl_ref, hs_ref, ocmp_ref, sel_ref, *, Td, past, n_slc):
    rows = qz_ref.shape[0]
    NC = ak_ref.shape[0]
    qpos, slope = _row_consts(rows, Td, past)
    ck = (ak_ref[...] + pltpu.roll(bk_ref[...], NC - 1, 0)).astype(BF16)
    cv = (av_ref[...] + pltpu.roll(bv_ref[...], NC - 1, 0)).astype(BF16)
    qz = (qz_ref[...] * (HEAD_DIM ** -0.5)).astype(BF16)
    cend = CMP_STRIDE * lax.broadcasted_iota(jnp.int32, (1, NC), 1) + (CMP_BLOCK - 1)
    dist = qpos - cend
    p = _masked_softmax(_dot_nt(qz, ck) - slope * dist.astype(F32), dist >= 0)
    ocmp_ref[...] = _dot(p.astype(BF16), cv)
    ph, pl_ = _split2(p)
    imp_h = _dot(ph, ovl_ref[...]) + _dot(pl_, ovl_ref[...])
    i1, i2, i3 = _split3(imp_h)
    hs = hs_ref[...]
    imp = (_dot(hs, i1) + _dot(hs, i2)) + _dot(hs, i3)
    NSP = imp.shape[1]
    lane = lax.broadcasted_iota(jnp.int32, (rows, NSP), 1)
    cur = qpos >> 6
    valid = (lane * SLC_BLOCK <= qpos) & (lane < n_slc)
    forced = (lane == 0) | (lane == cur) | (lane == cur - 1)
    score = jnp.where(valid, jnp.where(forced, FORCE_SCORE, imp), jnp.where(lane < n_slc, -1.0, -2.0))
    sel_ref[...] = _topk_mask(score, lane.astype(F32), SLC_TOP)


def _nsa_s_cmp(qz, ak, bk, av, bv, ovl, hs, Td, past, n_slc):
    Bd, rows, _ = qz.shape
    NC = ak.shape[1]
    NSP = ovl.shape[1]
    cspec = pl.BlockSpec((None, NC, KV_W), lambda b: (b, 0, 0))
    return pl.pallas_call(
        functools.partial(_nsa_s_cmp_body, Td=Td, past=past, n_slc=n_slc), grid=(Bd,),
        in_specs=[pl.BlockSpec((None, rows, KV_W), lambda b: (b, 0, 0)), cspec, cspec, cspec, cspec,
                  pl.BlockSpec(ovl.shape, lambda b: (0, 0)), pl.BlockSpec(hs.shape, lambda b: (0, 0))],
        out_specs=[pl.BlockSpec((None, rows, KV_W), lambda b: (b, 0, 0)),
                   pl.BlockSpec((None, rows, NSP), lambda b: (b, 0, 0))],
        out_shape=[jax.ShapeDtypeStruct((Bd, rows, KV_W), F32), jax.ShapeDtypeStruct((Bd, rows, NSP), F32)],
        compiler_params=_cparams(("parallel",), big=True), name="nsa_sample_cmp")(qz, ak, bk, av, bv, ovl, hs)


def _nsa_s_slc_body(pt_ref, *refs, P, Td, past, n_steps):
    kp = refs[:P]
    vp = refs[P:2 * P]
    (qz_ref, g_ref, ocmp_ref, sel_ref, nk_ref, nv_ref, wk_ref, wv_ref, o_ref, m_s, l_s, acc_s) = refs[2 * P:]
    s_id = pl.program_id(1)
    rows = qz_ref.shape[0]
    NSP = sel_ref.shape[1]
    qpos, slope = _row_consts(rows, Td, past)
    qz = (qz_ref[...] * (HEAD_DIM ** -0.5)).astype(BF16)
    sel_b = sel_ref[...].astype(BF16)

    @pl.when(s_id == 0)
    def _():
        m_s[...] = jnp.full_like(m_s, NEG)
        l_s[...] = jnp.zeros_like(l_s)
        acc_s[...] = jnp.zeros_like(acc_s)

    def tile(kt, vt, kpos, extra):
        nk = kt.shape[1]
        expand = ((kpos >> 6) == lax.broadcasted_iota(jnp.int32, (NSP, nk), 0)).astype(BF16)
        dist = qpos - kpos
        mask = (_dot(sel_b, expand) > 0.5) & (dist >= 0)
        if extra is not None:
            mask = mask & extra
        s = jnp.where(mask, _dot(qz, kt) - slope * dist.astype(F32), NEG)
        m_old = m_s[...]
        m_new = jnp.maximum(m_old, jnp.max(s, axis=-1, keepdims=True))
        alpha = jnp.exp(m_old - m_new)
        e = jnp.where(mask, jnp.exp(s - m_new), 0.0)
        l_s[...] = alpha * l_s[...] + jnp.sum(e, axis=-1, keepdims=True)
        acc_s[...] = alpha * acc_s[...] + _dot_nt(e.astype(BF16), vt)
        m_s[...] = m_new

    kt = jnp.concatenate([r[...] for r in kp], axis=1).astype(BF16)
    vt = jnp.concatenate([r[...] for r in vp], axis=1).astype(BF16)
    tile(kt, vt, s_id * (P * LANES) + lax.broadcasted_iota(jnp.int32, (1, P * LANES), 1), None)

    @pl.when(s_id == n_steps - 1)
    def _():
        nl = lax.broadcasted_iota(jnp.int32, (1, LANES), 1)
        tile(nk_ref[...].astype(BF16), nv_ref[...].astype(BF16), past + nl, nl < Td)
        o_slc = acc_s[...] / jnp.maximum(l_s[...], TINY)
        nw = wk_ref.shape[1]
        wl = lax.broadcasted_iota(jnp.int32, (1, nw), 1)
        dist = qpos - (past - (nw - LANES) + wl)
        okw = (wl < nw - LANES + Td) & (dist >= 0) & (dist <= WINDOW)
        pw = _masked_softmax(_dot(qz, wk_ref[...].astype(BF16)) - slope * dist.astype(F32), okw)
        o_win = _dot_nt(pw.astype(BF16), wv_ref[...].astype(BF16))
        gate = jax.nn.sigmoid(g_ref[...])
        o_ref[...] = gate[:, 0:1] * ocmp_ref[...] + gate[:, 1:2] * o_slc + gate[:, 2:3] * o_win


def _nsa_s_slc(page_table, layer, skc, svc, qz, graw, ocmp, sel, nks, nvs, wkx, wvx, Td, past, P):
    Bd, NP = page_table.shape
    rows = qz.shape[1]
    n_steps = NP // P
    pspec = [pl.BlockSpec((None, None, KV_W, LANES), lambda b, s, pt, i=i: (layer, pt[b, s * P + i], 0, 0))
             for i in range(P)]
    per_b = lambda a: pl.BlockSpec((None,) + a.shape[1:], lambda b, s, pt: (b, 0, 0))
    grid_spec = pltpu.PrefetchScalarGridSpec(
        num_scalar_prefetch=1, grid=(Bd, n_steps),
        in_specs=pspec + pspec + [per_b(a) for a in (qz, graw, ocmp, sel, nks, nvs, wkx, wvx)],
        out_specs=pl.BlockSpec((None, rows, KV_W), lambda b, s, pt: (b, 0, 0)),
        scratch_shapes=[pltpu.VMEM((rows, 1), F32), pltpu.VMEM((rows, 1), F32), pltpu.VMEM((rows, KV_W), F32)])
    return pl.pallas_call(
        functools.partial(_nsa_s_slc_body, P=P, Td=Td, past=past, n_steps=n_steps), grid_spec=grid_spec,
        out_shape=jax.ShapeDtypeStruct((Bd, rows, KV_W), F32),
        compiler_params=_cparams(("parallel", "arbitrary"), big=True), name="nsa_sample_slc")(
            page_table, *([skc] * P), *([svc] * P), qz, graw, ocmp, sel, nks, nvs, wkx, wvx)


def _sb_s_body(pt_ref, *refs, P, Td, n_steps):
    kp = refs[:P]
    vp = refs[P:2 * P]
    q_ref, nk_ref, nv_ref, u_ref, o_ref, carry_s, acc_s = refs[2 * P:]
    s_id = pl.program_id(1)
    rows = q_ref.shape[0]
    qb = (q_ref[...] * (HEAD_DIM ** -0.5)).astype(BF16)
    u = u_ref[...]

    @pl.when(s_id == 0)
    def _():
        row = lax.broadcasted_iota(jnp.int32, (rows, LANES), 0)
        lane = lax.broadcasted_iota(jnp.int32, (rows, LANES), 1)
        newer = lane < (row - (row // Td) * Td)
        carry, acc = _sb_block(qb, nk_ref[...].astype(BF16), nv_ref[...].astype(BF16), u,
                               jnp.zeros((rows, 1), F32), jnp.zeros((rows, SB_W), F32), newer)
        carry_s[...] = carry
        acc_s[...] = acc

    carry, acc = carry_s[...], acc_s[...]
    for i in range(P - 1, -1, -1):
        carry, acc = _sb_block(qb, kp[i][...].astype(BF16), vp[i][...].astype(BF16), u, carry, acc, None)
    carry_s[...] = carry
    acc_s[...] = acc

    @pl.when(s_id == n_steps - 1)
    def _():
        o_ref[...] = acc


def _sb_sample(page_table, layer, kc, vc, qbd, nk, nv, u, Td, P):
    Bd, NP = page_table.shape
    rows = qbd.shape[1]
    n_steps = NP // P
    pspec = [pl.BlockSpec((None, None, SB_W, LANES),
                          lambda b, s, pt, i=i: (layer, pt[b, NP - (s + 1) * P + i], 0, 0)) for i in range(P)]
    per_b = lambda a: pl.BlockSpec((None,) + a.shape[1:], lambda b, s, pt: (b, 0, 0))
    grid_spec = pltpu.PrefetchScalarGridSpec(
        num_scalar_prefetch=1, grid=(Bd, n_steps),
        in_specs=pspec + pspec + [per_b(qbd), per_b(nk), per_b(nv), pl.BlockSpec(u.shape, lambda b, s, pt: (0, 0))],
        out_specs=pl.BlockSpec((None, rows, SB_W), lambda b, s, pt: (b, 0, 0)),
        scratch_shapes=[pltpu.VMEM((rows, 1), F32), pltpu.VMEM((rows, SB_W), F32)])
    return pl.pallas_call(
        functools.partial(_sb_s_body, P=P, Td=Td, n_steps=n_steps), grid_spec=grid_spec,
        out_shape=jax.ShapeDtypeStruct((Bd, rows, SB_W), F32),
        compiler_params=_cparams(("parallel", "arbitrary"), big=True), name="sb_sample")(
            page_table, *([kc] * P), *([vc] * P), qbd, nk, nv, u)


def _cmp_weights(w, pe):
    ratio = CMP_BLOCK // CMP_STRIDE
    w5 = w.reshape(2, ratio, CMP_STRIDE, HEAD_DIM, HEAD_DIM)
    eye = jnp.eye(NSA_KV, dtype=w.dtype)
    wbd = jnp.einsum('krcde,gh->krcgdhe', w5, eye).reshape(2, ratio, CMP_STRIDE * KV_W, KV_W)
    pe2 = jnp.broadcast_to(pe.reshape(ratio, CMP_STRIDE, 1, HEAD_DIM), (ratio, CMP_STRIDE, NSA_KV, HEAD_DIM))
    return wbd.astype(BF16), jnp.swapaxes(wbd, 2, 3).astype(BF16), pe2.reshape(ratio, 1, CMP_STRIDE * KV_W)


def _overlap(n_cmp_pad, n_slc_pad):
    n = np.arange(n_cmp_pad)[:, None] * CMP_STRIDE
    j = np.arange(n_slc_pad)[None, :] * SLC_BLOCK
    return jnp.asarray((n < j + SLC_BLOCK) & (n + CMP_BLOCK > j), dtype=BF16)


def _suffix_matrix(n):
    return jnp.asarray(np.arange(n)[:, None] > np.arange(n)[None, :], dtype=BF16)


def _feature_major_state(xt, heads):
    d, b, _, t = xt.shape
    return xt.reshape(d, b, heads, HEAD_DIM, t).transpose(0, 1, 4, 2, 3)


def kernel(x_prompt, x_sample, mem_prompt, cache_nsa_cmp_k, cache_nsa_cmp_v, cache_nsa_slc_k, cache_nsa_slc_v, cache_nsa_win_k, cache_nsa_win_v, cache_sb_k, cache_sb_v, cache_mem_k, cache_mem_v, state_lru_h, state_lru_conv, page_table, w_in, nsa_cmp_w, nsa_cmp_pe, lru_conv_w, lru_conv_b, lru_wa, lru_ba, lru_wx, lru_bx, lru_lambda, w_mem_kv, w_branch, w_out, ln1_g, ln1_b, ln2_g, ln2_b, ffn_w13, ffn_w2, moe_router_w, moe_router_b, moe_w13, moe_w2):
    depth = w_in.shape[0]
    B, T, _ = x_prompt.shape
    Bd, Td, _ = x_sample.shape
    NP = page_table.shape[1]
    n_pool = cache_nsa_cmp_k.shape[1]
    page = cache_nsa_cmp_k.shape[2]
    past = NP * page
    win_buf = cache_nsa_win_k.shape[2]
    n_mem = mem_prompt.shape[1]
    alpha = (2.0 * depth) ** 0.25
    assert page == LANES and T % 512 == 0 and T // SLC_BLOCK >= SLC_TOP and T // SLC_BLOCK <= LANES
    assert win_buf == WINDOW and Td < CMP_STRIDE and NP % 8 == 0
    P = 8
    rows = NSA_HEADS * Td

    fm = lambda c: c.transpose(0, 1, 3, 4, 2).reshape(c.shape[0], c.shape[1], c.shape[3] * c.shape[4], c.shape[2])
    slc_kc, slc_vc, sb_kc, sb_vc = fm(cache_nsa_slc_k), fm(cache_nsa_slc_v), fm(cache_sb_k), fm(cache_sb_v)
    win_kc, win_vc = fm(cache_nsa_win_k), fm(cache_nsa_win_v)
    chunks = lambda c: c.reshape(depth, n_pool, page // CMP_STRIDE, CMP_STRIDE * KV_W)
    cmp_kc, cmp_vc = chunks(cache_nsa_cmp_k), chunks(cache_nsa_cmp_v)

    n_cmp_p = T // CMP_STRIDE
    ovl_p = _overlap(n_cmp_p, LANES)
    n_slc_s = -(-(past + Td) // SLC_BLOCK)
    nsp_s = -(-n_slc_s // LANES) * LANES
    n_cmp_s = past // CMP_STRIDE
    ovl_s = _overlap(n_cmp_s, nsp_s)
    u_blk = _suffix_matrix(LANES)
    hs = jnp.asarray((np.arange(rows)[:, None] // (NSA_GROUP * Td) == np.arange(rows)[None, :] // (NSA_GROUP * Td))
                     & (np.arange(rows)[:, None] % Td == np.arange(rows)[None, :] % Td), dtype=BF16)

    xp = x_prompt
    xs = x_sample.reshape(1, Bd * Td, D_MODEL)
    mem2 = mem_prompt.reshape(1, B * n_mem, D_MODEL)
    p_layers, s_layers = [], []
    for l in range(depth):
        wt = jnp.swapaxes(w_in[l], 0, 1)
        c = np.cumsum([0, 512, 128, 128, 128, 128, 128, 128, 24, 512, 512, 512, 512, 512, 512, 4096])
        seg = lambda a, b: wt[c[a]:c[b]]
        ng_w = jnp.pad(seg(7, 8), ((0, LANES - 24), (0, 0)))
        wn = jnp.concatenate([seg(0, 1), ng_w, seg(8, 10), seg(10, 11), seg(13, 14), seg(1, 3)], axis=0)
        wn = jnp.swapaxes(wn, 0, 1).astype(BF16)
        wtr = jnp.concatenate([seg(1, 7), seg(11, 13)], axis=0).astype(BF16)
        w_all = jnp.swapaxes(jnp.concatenate([seg(0, 7), ng_w, seg(8, 14)], axis=0), 0, 1).astype(BF16)
        wmg = jnp.swapaxes(seg(14, 15), 0, 1).astype(BF16)
        cw_tok, cw_feat, pe2 = _cmp_weights(nsa_cmp_w[l], nsa_cmp_pe[l])
        eye8 = jnp.eye(LRU_BLOCKS, dtype=F32)
        bd = lambda w: jnp.einsum('ncd,nm->ncmd', w, eye8).reshape(D_RNN, D_RNN)
        wbd = jnp.concatenate([bd(lru_wa[l]), bd(lru_wx[l])], axis=1).astype(BF16)
        gbias = jnp.concatenate([lru_ba[l].reshape(1, D_RNN), lru_bx[l].reshape(1, D_RNN)], axis=1)
        lam = lru_lambda[l].reshape(1, D_RNN)
        conv_w, conv_b = lru_conv_w[l], lru_conv_b[l].reshape(1, D_RNN)
        wmem = w_mem_kv[l].astype(BF16)
        wbr, wo = w_branch[l].astype(BF16), w_out[l].astype(BF16)
        g1, b1 = ln1_g[l].reshape(1, D_MODEL), ln1_b[l].reshape(1, D_MODEL)
        g2, b2 = ln2_g[l].reshape(1, D_MODEL), ln2_b[l].reshape(1, D_MODEL)

        ncols = [(0, 512), (512, 128), (640, 512), (1152, 512), (1664, 512), (2176, 512), (2688, 128), (2816, 128)]
        tcols = [(0, 128, ("full",)), (128, 128, ("full",)), (256, 128, ("full", "blk")), (384, 128, ("full", "blk")),
                 (512, 128, ("full", "blk")), (640, 128, ("full", "blk")), (768, 512, ("full", "blk")),
                 (1280, 512, ("full", "blk"))]
        (nq, ng, lx, lg, sq, mq, ck_tok, cv_tok, ck_t, cv_t, sk_t, sk_b, sv_t, sv_b, wk_t, wk_b, wv_t, wv_b,
         sbk_t, sbk_b, sbv_t, sbv_b) = _proj(xp, wn, ncols, wtr, tcols)
        ckc, cvc = _compress(ck_tok.reshape(B, n_cmp_p, CMP_STRIDE * KV_W), cv_tok.reshape(B, n_cmp_p, CMP_STRIDE * KV_W),
                             cw_feat[0], cw_feat[1], pe2)
        o_nsa = _nsa_prompt(nq, ng, ckc, cvc, sk_b, sv_b, wk_b, wv_b, ovl_p)
        o_lru, h_new, conv_new = _rglru(lx, lg, jnp.zeros((B, CONV_W - 1, D_RNN), F32), jnp.zeros((B, 1, D_RNN), F32),
                                        conv_w, conv_b, wbd, gbias, lam)
        o_sb = _sb_prompt(sq, sbk_b, sbv_b, u_blk)
        mk, mv = _proj(mem2, wmem, [(0, 512), (512, 512)])
        mk, mv = mk.reshape(B, n_mem, 512), mv.reshape(B, n_mem, 512)
        o_mem = _mem_attn(mq, mk, mv)
        M = B * T
        x1 = _merge_ln(xp.reshape(M, D_MODEL), [o.reshape(M, 512) for o in (o_nsa, o_lru, o_sb, o_mem)],
                       wmg, wbr, wo, g1, b1, alpha)
        p_layers.append((ck_t, cv_t, sk_t, sv_t, wk_t[:, :, T - min(WINDOW, T):], wv_t[:, :, T - min(WINDOW, T):],
                         sbk_t, sbv_t, h_new.reshape(B, D_RNN), conv_new,
                         mk.reshape(B, n_mem, MEM_HEADS, MEM_HEAD_DIM), mv.reshape(B, n_mem, MEM_HEADS, MEM_HEAD_DIM)))

        ms = Bd * Td
        scols = [(0, 512), (512, 128), (640, 128), (768, 128), (896, 128), (1024, 128), (1152, 128), (1280, 128),
                 (1408, 512), (1920, 512), (2432, 512), (2944, 512), (3456, 512), (3968, 512)]
        (nq_s, ck_s, cv_s, sk_s, sv_s, wk_s, wv_s, ng_s, lx_s, lg_s, sq_s, sbk_s, sbv_s, mq_s) = [
            a.reshape(Bd, Td, a.shape[-1]) for a in _proj(xs, w_all, scols)]
        ak, bk = _compress_paged(cmp_kc, l, page_table, cw_tok[0], pe2, P)
        av, bv = _compress_paged(cmp_vc, l, page_table, cw_tok[1], pe2, P)
        q8 = nq_s.reshape(Bd, Td, NSA_HEADS, HEAD_DIM).transpose(0, 2, 1, 3)
        gsel = jnp.asarray(np.arange(NSA_HEADS)[:, None] // NSA_GROUP == np.arange(NSA_KV)[None, :], dtype=F32)
        qz = (q8[:, :, :, None, :] * gsel[None, :, None, :, None]).reshape(Bd, rows, KV_W)
        hsel = jnp.eye(SB_HEADS, dtype=F32)
        s8 = sq_s.reshape(Bd, Td, SB_HEADS, HEAD_DIM).transpose(0, 2, 1, 3)
        qbd = (s8[:, :, :, None, :] * hsel[None, :, None, :, None]).reshape(Bd, rows, SB_W)
        graw = jnp.pad(ng_s[:, :, :24].reshape(Bd, Td, NSA_HEADS, 3).transpose(0, 2, 1, 3).reshape(Bd, rows, 3),
                       ((0, 0), (0, 0), (0, LANES - 3)))
        new_fm = lambda a: jnp.pad(jnp.swapaxes(a, 1, 2), ((0, 0), (0, 0), (0, LANES - Td)))
        ocmp, sel = _nsa_s_cmp(qz, ak, bk, av, bv, ovl_s, hs, Td, past, n_slc_s)
        wkx = jnp.concatenate([win_kc[l], new_fm(wk_s)], axis=2)
        wvx = jnp.concatenate([win_vc[l], new_fm(wv_s)], axis=2)
        o_rows = _nsa_s_slc(page_table, l, slc_kc, slc_vc, qz, graw, ocmp, sel, new_fm(sk_s), new_fm(sv_s),
                            wkx, wvx, Td, past, P)
        o6 = o_rows.reshape(Bd, NSA_KV, NSA_GROUP, Td, NSA_KV, HEAD_DIM)
        o_nsa_s = jnp.stack([o6[:, g, :, :, g, :] for g in range(NSA_KV)], axis=1)
        o_nsa_s = o_nsa_s.reshape(Bd, NSA_HEADS, Td, HEAD_DIM).transpose(0, 2, 1, 3).reshape(Bd, Td, NSA_HEADS * HEAD_DIM)
        o_lru_s, h_new_s, conv_new_s = _rglru(lx_s, lg_s, state_lru_conv[l], state_lru_h[l].reshape(Bd, 1, D_RNN),
                                              conv_w, conv_b, wbd, gbias, lam)
        sb_rows = _sb_sample(page_table, l, sb_kc, sb_vc, qbd, new_fm(sbk_s), new_fm(sbv_s), u_blk, Td, P)
        sb5 = sb_rows.reshape(Bd, SB_HEADS, Td, SB_HEADS, HEAD_DIM)
        o_sb_s = jnp.stack([sb5[:, h, :, h, :] for h in range(SB_HEADS)], axis=2).reshape(Bd, Td, SB_W)
        o_mem_s = _mem_attn(mq_s, cache_mem_k[l].reshape(Bd, n_mem, 512), cache_mem_v[l].reshape(Bd, n_mem, 512))
        x1s = _merge_ln(xs.reshape(ms, D_MODEL), [o.reshape(ms, 512) for o in (o_nsa_s, o_lru_s, o_sb_s, o_mem_s)],
                        wmg, wbr, wo, g1, b1, alpha)
        kv4 = lambda a: a.reshape(Bd, Td, NSA_KV, HEAD_DIM)
        s_layers.append((kv4(ck_s), kv4(cv_s), kv4(sk_s), kv4(sv_s), kv4(wk_s), kv4(wv_s),
                         sbk_s.reshape(Bd, Td, SB_HEADS, HEAD_DIM), sbv_s.reshape(Bd, Td, SB_HEADS, HEAD_DIM),
                         h_new_s.reshape(Bd, D_RNN), conv_new_s))

        if l % 2 == 0:
            w13, w2 = ffn_w13[l // 2].astype(BF16), ffn_w2[l // 2].astype(BF16)
            x2 = _ffn_ln(x1, w13, w2, g2, b2, alpha)
            x2s = _ffn_ln(x1s, w13, w2, g2, b2, alpha)
        else:
            rw = jnp.pad(moe_router_w[l // 2], ((0, 0), (0, LANES - N_EXP)))
            rwh = rw.astype(BF16)
            rwl = (rw - rwh.astype(F32)).astype(BF16)
            rb = jnp.pad(moe_router_b[l // 2].reshape(1, N_EXP), ((0, 0), (0, LANES - N_EXP)))
            w13, w2 = moe_w13[l // 2].astype(BF16), moe_w2[l // 2].astype(BF16)
            x2 = _moe_ln(x1, rwh, rwl, rb, w13, w2, g2, b2, alpha)
            x2s = _moe_ln(x1s, rwh, rwl, rb, w13, w2, g2, b2, alpha)
        xp = x2.reshape(B, T, D_MODEL)
        xs = x2s.reshape(1, ms, D_MODEL)

    pst = [jnp.stack(r, axis=0) for r in zip(*p_layers)]
    prompt_state = ([_feature_major_state(a, NSA_KV) for a in pst[:6]] + [_feature_major_state(a, SB_HEADS) for a in pst[6:8]]
                    + pst[8:])
    sample_state = [jnp.stack(r, axis=0) for r in zip(*s_layers)]
    return (xp, xs.reshape(Bd, Td, D_MODEL), *prompt_state, *sample_state)
```

```python
import functools

import numpy as np
import jax
import jax.numpy as jnp
from jax import lax
from jax.experimental import pallas as pl
from jax.experimental.pallas import tpu as pltpu

F32 = jnp.float32
BF16 = jnp.bfloat16

D_MODEL = 1024
HEAD_DIM = 64
NSA_HEADS = 8
NSA_KV = 2
NSA_GROUP = NSA_HEADS // NSA_KV
CMP_STRIDE = 16
CMP_BLOCK = 32
SLC_BLOCK = 64
SLC_TOP = 16
WINDOW = 512
FORCE_SCORE = 1.0e4
SB_HEADS = 8
D_RNN = 512
LRU_BLOCKS = 8
CONV_W = 4
LRU_C = 8.0
MEM_HEADS = 4
MEM_HEAD_DIM = 128
N_EXP = 8
LN_EPS = 1e-5
NEG = -1.0e30
TINY = 1.0e-30
KV_W = NSA_KV * HEAD_DIM
SB_W = SB_HEADS * HEAD_DIM
LANES = 128
VMEM_LIMIT = 56 * 1024 * 1024
SLOPES = tuple(float(2.0 ** (-8.0 * (h + 1) / NSA_HEADS)) for h in range(NSA_HEADS))


def _cparams(sem, big=False):
    return pltpu.CompilerParams(dimension_semantics=sem, vmem_limit_bytes=VMEM_LIMIT if big else None)


def _dot(a, b):
    return jnp.dot(a, b, preferred_element_type=F32)


def _dot_nt(a, b):
    return lax.dot_general(a, b, (((1,), (1,)), ((), ())), preferred_element_type=F32)


def _split2(x):
    hi = x.astype(BF16)
    lo = (x - hi.astype(F32)).astype(BF16)
    return hi, lo


def _split3(x):
    hi = x.astype(BF16)
    r = x - hi.astype(F32)
    mid = r.astype(BF16)
    lo = (r - mid.astype(F32)).astype(BF16)
    return hi, mid, lo


def _softplus(x):
    return jnp.maximum(x, 0.0) + jnp.log1p(jnp.exp(-jnp.abs(x)))


def _layer_norm(z, g, b):
    mu = jnp.mean(z, axis=-1, keepdims=True)
    zc = z - mu
    var = jnp.mean(zc * zc, axis=-1, keepdims=True)
    return zc * lax.rsqrt(var + LN_EPS) * g + b


def _masked_softmax(s, mask):
    s = jnp.where(mask, s, NEG)
    m = jnp.max(s, axis=-1, keepdims=True)
    e = jnp.where(mask, jnp.exp(s - m), 0.0)
    return e / jnp.maximum(jnp.sum(e, axis=-1, keepdims=True), TINY)


def _topk_mask(score, lanef, k):
    sel = jnp.zeros_like(score)
    s = score
    for _ in range(k):
        m = jnp.max(s, axis=-1, keepdims=True)
        jm = jnp.min(jnp.where(s == m, lanef, 1.0e9), axis=-1, keepdims=True)
        hit = lanef == jm
        sel = jnp.where(hit, 1.0, sel)
        s = jnp.where(hit, -3.0, s)
    return sel


def _topk_rank(score_t, k):
    R, N = score_t.shape
    nv = R // 8
    groups = [score_t[8 * v:8 * v + 8] for v in range(nv)]
    sub = lax.broadcasted_iota(jnp.int32, (8, N), 0)
    ranks = [jnp.zeros((8, N), F32) for _ in range(nv)]
    for jp in range(R):
        sj = score_t[jp:jp + 1, :]
        vj, rj = divmod(jp, 8)
        for v in range(nv):
            if v > vj:
                beats = sj >= groups[v]
            elif v < vj:
                beats = sj > groups[v]
            else:
                beats = (sj > groups[v]) | ((sj == groups[v]) & (sub > rj))
            ranks[v] = ranks[v] + jnp.where(beats, 1.0, 0.0)
    return jnp.concatenate([jnp.where(r < k, 1.0, 0.0) for r in ranks], axis=0)


def _proj_body(*refs, ncols, tcols, has_t):
    x_ref, wn_ref = refs[0], refs[1]
    o_refs = refs[3:] if has_t else refs[2:]
    xb = x_ref[...].astype(BF16)
    k = 0
    for c0, wd in ncols:
        o_refs[k][...] = _dot(xb, wn_ref[:, c0:c0 + wd]).astype(o_refs[k].dtype)
        k += 1
    if has_t:
        wt_ref = refs[2]
        for c0, wd, kinds in tcols:
            r = _dot_nt(wt_ref[c0:c0 + wd, :], xb)
            for kind in kinds:
                if kind == "full":
                    o_refs[k][...] = r
                else:
                    rb = r.astype(BF16)
                    for c in range(rb.shape[1] // LANES):
                        o_refs[k][c] = rb[:, c * LANES:(c + 1) * LANES]
                k += 1


def _proj(x, wn, ncols, wt=None, tcols=(), tm=512):
    B, T, K = x.shape
    tm = min(tm, T)
    has_t = wt is not None
    in_specs = [pl.BlockSpec((None, tm, K), lambda b, t: (b, t, 0)),
                pl.BlockSpec(wn.shape, lambda b, t: (0, 0))]
    args = [x, wn]
    if has_t:
        in_specs.append(pl.BlockSpec(wt.shape, lambda b, t: (0, 0)))
        args.append(wt)
    out_specs, out_shape = [], []
    for _, wd in ncols:
        out_specs.append(pl.BlockSpec((None, tm, wd), lambda b, t: (b, t, 0)))
        out_shape.append(jax.ShapeDtypeStruct((B, T, wd), F32))
    for _, wd, kinds in tcols:
        for kind in kinds:
            if kind == "full":
                out_specs.append(pl.BlockSpec((None, wd, tm), lambda b, t: (b, 0, t)))
                out_shape.append(jax.ShapeDtypeStruct((B, wd, T), F32))
            else:
                nb = tm // LANES
                out_specs.append(pl.BlockSpec((None, nb, wd, LANES), lambda b, t: (b, t, 0, 0)))
                out_shape.append(jax.ShapeDtypeStruct((B, T // LANES, wd, LANES), BF16))
    return pl.pallas_call(
        functools.partial(_proj_body, ncols=tuple(ncols), tcols=tuple(tcols), has_t=has_t),
        grid=(B, T // tm), in_specs=in_specs, out_specs=out_specs, out_shape=out_shape,
        compiler_params=_cparams(("parallel", "parallel"), big=True), name="proj")(*args)


def _compress_body(xk_ref, xv_ref, wk_ref, wv_ref, pe_ref, ok_ref, ov_ref):
    n = xk_ref.shape[0]
    colid = lax.broadcasted_iota(jnp.int32, (KV_W, n), 1)
    for x_ref, w_ref, o_ref in ((xk_ref, wk_ref, ok_ref), (xv_ref, wv_ref, ov_ref)):
        x = x_ref[...]
        a = _dot_nt(w_ref[0], (x + pe_ref[0]).astype(BF16))
        b = _dot_nt(w_ref[1], (x + pe_ref[1]).astype(BF16))
        o_ref[...] = jnp.where(colid < n - 1, a + pltpu.roll(b, n - 1, 1), 0.0)


def _compress(xk, xv, wk, wv, pe):
    B, n, cw = xk.shape
    xspec = pl.BlockSpec((None, n, cw), lambda b: (b, 0, 0))
    wspec = pl.BlockSpec(wk.shape, lambda b: (0, 0, 0))
    ospec = pl.BlockSpec((None, KV_W, n), lambda b: (b, 0, 0))
    return pl.pallas_call(
        _compress_body, grid=(B,),
        in_specs=[xspec, xspec, wspec, wspec, pl.BlockSpec(pe.shape, lambda b: (0, 0, 0))],
        out_specs=[ospec, ospec], out_shape=[jax.ShapeDtypeStruct((B, KV_W, n), F32)] * 2,
        compiler_params=_cparams(("parallel",), big=True), name="compress")(xk, xv, wk, wv, pe)


def _compress_paged_body(pt_ref, *refs, P):
    x_refs = refs[:P]
    w_ref, pe_ref, a_ref, b_ref, tok = refs[P:]
    page = x_refs[0].shape[1]
    for i in range(P):
        tok[i * page:(i + 1) * page, :] = x_refs[i][...].T
    nch = P * page // CMP_STRIDE
    x = jnp.concatenate([tok[pl.ds(c, nch, stride=CMP_STRIDE), :] for c in range(CMP_STRIDE)], axis=1)
    a_ref[...] = _dot((x + pe_ref[0]).astype(BF16), w_ref[0])
    b_ref[...] = _dot((x + pe_ref[1]).astype(BF16), w_ref[1])


def _compress_paged(cache, layer, page_table, w, pe, P):
    Bd, NP = page_table.shape
    page = cache.shape[-1]
    nch = P * page // CMP_STRIDE
    in_specs = [pl.BlockSpec((None, None, KV_W, page), lambda b, s, pt, i=i: (layer, pt[b, s * P + i], 0, 0))
                for i in range(P)]
    in_specs += [pl.BlockSpec(w.shape, lambda b, s, pt: (0, 0, 0)), pl.BlockSpec(pe.shape, lambda b, s, pt: (0, 0, 0))]
    ospec = pl.BlockSpec((None, nch, KV_W), lambda b, s, pt: (b, s, 0))
    grid_spec = pltpu.PrefetchScalarGridSpec(num_scalar_prefetch=1, grid=(Bd, NP // P), in_specs=in_specs,
                                             out_specs=[ospec, ospec],
                                             scratch_shapes=[pltpu.VMEM((P * page, KV_W), F32)])
    return pl.pallas_call(
        functools.partial(_compress_paged_body, P=P), grid_spec=grid_spec,
        out_shape=[jax.ShapeDtypeStruct((Bd, NP * page // CMP_STRIDE, KV_W), F32)] * 2,
        compiler_params=_cparams(("parallel", "parallel"), big=True), name="compress_paged")(
            page_table, *([cache] * P), w, pe)


def _nsa_prompt_body(q_ref, ng_ref, ckt_ref, cvt_ref, sk_ref, sv_ref, wk_ref, wv_ref, ovl_ref, o_ref,
                     *, Q, KBN, n_slc):
    i = pl.program_id(1)
    q0 = i * Q
    KB = KBN * LANES
    QB = Q // LANES
    lane = lax.broadcasted_iota(jnp.int32, (Q, LANES), 1)
    lo_half = lane < HEAD_DIM
    qpos = q0 + lax.broadcasted_iota(jnp.int32, (Q, 1), 0)
    gate = jax.nn.sigmoid(ng_ref[...])
    slabs = [q_ref[:, LANES * s:LANES * (s + 1)] * (HEAD_DIM ** -0.5) for s in range(4)]

    def placed(h, g):
        sl = slabs[h // 2]
        if (h % 2) != g:
            sl = pltpu.roll(sl, HEAD_DIM, 1)
        return jnp.where(lo_half if g == 0 else jnp.logical_not(lo_half), sl, 0.0)

    ckt = ckt_ref[...].astype(BF16)
    cvt = cvt_ref[...].astype(BF16)
    NC = ckt.shape[1]
    cend = CMP_STRIDE * lax.broadcasted_iota(jnp.int32, (1, NC), 1) + (CMP_BLOCK - 1)
    dist_c = qpos - cend
    vis_c = dist_c >= 0
    dist_cf = dist_c.astype(F32)
    blk_t = lax.broadcasted_iota(jnp.int32, (n_slc, NSA_KV * Q), 0)
    col_t = lax.broadcasted_iota(jnp.int32, (n_slc, NSA_KV * Q), 1)
    qpos_t = q0 + (col_t & (Q - 1))
    cur_t = qpos_t >> 6
    valid_t = blk_t * SLC_BLOCK <= qpos_t
    forced_t = (blk_t == 0) | (blk_t == cur_t) | (blk_t == cur_t - 1)
    rows_of = [slice(h * Q, (h + 1) * Q) for h in range(NSA_HEADS)]
    group_of = [h // NSA_GROUP for h in range(NSA_HEADS)]
    qall = jnp.concatenate([placed(h, group_of[h]) for h in range(NSA_HEADS)], axis=0).astype(BF16)

    def softmax_rows(sd, distf, mask):
        ps = []
        for h in range(NSA_HEADS):
            s = jnp.where(mask, sd[rows_of[h]] - SLOPES[h] * distf, NEG)
            e = jnp.where(mask, jnp.exp(s - jnp.max(s, axis=-1, keepdims=True)), 0.0)
            ps.append(e / jnp.maximum(jnp.sum(e, axis=-1, keepdims=True), TINY))
        return ps

    ps = softmax_rows(_dot(qall, ckt), dist_cf, vis_c)
    o_cmp = _dot_nt(jnp.concatenate(ps, axis=0).astype(BF16), cvt)
    psum = jnp.concatenate([(ps[NSA_GROUP * g] + ps[NSA_GROUP * g + 1]) + (ps[NSA_GROUP * g + 2] + ps[NSA_GROUP * g + 3])
                            for g in range(NSA_KV)], axis=0)
    ph, pl_ = _split2(psum)
    imp = _dot(ph, ovl_ref[...]) + _dot(pl_, ovl_ref[...])
    score_t = jnp.where(valid_t, jnp.where(forced_t, FORCE_SCORE, imp.T[:n_slc]), -1.0)
    sel_t = _topk_rank(score_t, SLC_TOP)
    if n_slc < LANES:
        sel_t = jnp.concatenate([sel_t, jnp.zeros((LANES - n_slc, NSA_KV * Q), F32)], axis=0)
    sel_b = sel_t.T.astype(BF16)
    blk_row = lax.broadcasted_iota(jnp.int32, (LANES, KB), 0)

    half_rows = lax.broadcasted_iota(jnp.int32, (LANES, KB), 0) < HEAD_DIM

    def kt_body(t, carry):
        m, acc = carry
        kt = jnp.concatenate([sk_ref[t * KBN + c] for c in range(KBN)], axis=1)
        vt = jnp.concatenate([sv_ref[t * KBN + c] for c in range(KBN)], axis=1)
        one = jnp.ones_like(vt)
        vts = [jnp.where(half_rows, vt, one), jnp.where(half_rows, one, vt)]
        kpos = t * KB + lax.broadcasted_iota(jnp.int32, (1, KB), 1)
        expand = ((kpos >> 6) == blk_row).astype(BF16)
        sd = _dot(qall, kt)
        selx = _dot(sel_b, expand)
        dist = qpos - kpos
        causal = dist >= 0
        masks = [(selx[g * Q:(g + 1) * Q] > 0.5) & causal for g in range(NSA_KV)]
        distf = dist.astype(F32)
        es, alphas, ms = [], [], []
        for h in range(NSA_HEADS):
            mask = masks[group_of[h]]
            s = jnp.where(mask, sd[rows_of[h]] - SLOPES[h] * distf, NEG)
            m_old = m[rows_of[h]]
            m_new = jnp.maximum(m_old, jnp.max(s, axis=-1, keepdims=True))
            alphas.append(jnp.exp(m_old - m_new))
            ms.append(m_new)
            es.append(jnp.where(mask, jnp.exp(s - m_new), 0.0).astype(BF16))
        pv = jnp.concatenate([_dot_nt(jnp.concatenate(es[NSA_GROUP * g:NSA_GROUP * (g + 1)], axis=0), vts[g])
                              for g in range(NSA_KV)], axis=0)
        return jnp.concatenate(ms, axis=0), jnp.concatenate(alphas, axis=0) * acc + pv

    nkt = (q0 + Q + KB - 1) // KB
    init = (jnp.full((NSA_HEADS * Q, 1), NEG, F32), jnp.zeros((NSA_HEADS * Q, LANES), F32))
    _, acc_s = lax.fori_loop(0, nkt, kt_body, init)
    o_slc = acc_s / jnp.maximum(pltpu.roll(acc_s, HEAD_DIM, 1), TINY)
    sb0 = jnp.maximum(i * QB - WINDOW // LANES, 0)
    nwb = WINDOW // LANES + QB
    kt = jnp.concatenate([wk_ref[sb0 + c] for c in range(nwb)], axis=1)
    vt = jnp.concatenate([wv_ref[sb0 + c] for c in range(nwb)], axis=1)
    dist = qpos - (sb0 * LANES + lax.broadcasted_iota(jnp.int32, (1, nwb * LANES), 1))
    pw = softmax_rows(_dot(qall, kt), dist.astype(F32), (dist >= 0) & (dist <= WINDOW))
    o_win = _dot_nt(jnp.concatenate(pw, axis=0).astype(BF16), vt)
    outs = [None] * NSA_HEADS
    for h in range(NSA_HEADS):
        rows = rows_of[h]
        oh = (gate[:, 3 * h:3 * h + 1] * o_cmp[rows] + gate[:, 3 * h + 1:3 * h + 2] * o_slc[rows]
              + gate[:, 3 * h + 2:3 * h + 3] * o_win[rows])
        if (h % 2) != group_of[h]:
            oh = pltpu.roll(oh, HEAD_DIM, 1)
        outs[h] = oh
    for s in range(4):
        o_ref[:, LANES * s:LANES * (s + 1)] = jnp.where(lo_half, outs[2 * s], outs[2 * s + 1])


def _nsa_prompt(q, ng, ckt, cvt, skb, svb, wkb, wvb, ovl, Q=128, KBN=4):
    B, T, _ = q.shape
    n_slc = T // SLC_BLOCK
    NC = ckt.shape[2]
    nblk = T // LANES
    kvspec = pl.BlockSpec((None, nblk, KV_W, LANES), lambda b, i: (b, 0, 0, 0))
    cspec = pl.BlockSpec((None, KV_W, NC), lambda b, i: (b, 0, 0))
    return pl.pallas_call(
        functools.partial(_nsa_prompt_body, Q=Q, KBN=KBN, n_slc=n_slc), grid=(B, T // Q),
        in_specs=[pl.BlockSpec((None, Q, NSA_HEADS * HEAD_DIM), lambda b, i: (b, i, 0)),
                  pl.BlockSpec((None, Q, LANES), lambda b, i: (b, i, 0)),
                  cspec, cspec, kvspec, kvspec, kvspec, kvspec,
                  pl.BlockSpec(ovl.shape, lambda b, i: (0, 0))],
        out_specs=pl.BlockSpec((None, Q, NSA_HEADS * HEAD_DIM), lambda b, i: (b, i, 0)),
        out_shape=jax.ShapeDtypeStruct((B, T, NSA_HEADS * HEAD_DIM), F32),
        compiler_params=_cparams(("parallel", "parallel"), big=True), name="nsa_prompt")(
            q, ng, ckt, cvt, skb, svb, wkb, wvb, ovl)


def _sb_block(qh, kt, vt, u, carry, acc, mask):
    n = qh.shape[0]
    s = _dot(qh, kt)
    sp = _softplus(s)
    l1m = -sp if mask is None else jnp.where(mask, -sp, 0.0)
    hi, lo = _split2(l1m)
    r2 = _dot(jnp.concatenate([hi, lo], axis=0), u)
    after = r2[:n] + r2[n:]
    w = jnp.exp((s - sp) + after + carry)
    if mask is not None:
        w = jnp.where(mask, w, 0.0)
    acc = acc + _dot_nt(w.astype(BF16), vt)
    carry = carry + jnp.sum(l1m, axis=-1, keepdims=True)
    return carry, acc


def _sb_pairs_step(qs, kts, vts, u2_ref, row_lo, carries, accs, mask2):
    n = qs[0].shape[0]
    zero = jnp.zeros_like(kts[0])
    bd = lambda t: jnp.concatenate([jnp.where(row_lo, t, zero), jnp.where(row_lo, zero, t)], axis=1)
    ss = [_dot(q, bd(kt)) for q, kt in zip(qs, kts)]
    sps, cats = [], []
    for s in ss:
        sp = jnp.maximum(s, 0.0) + jnp.log(1.0 + jnp.exp(-jnp.abs(s)))
        l1m = -sp if mask2 is None else jnp.where(mask2, -sp, 0.0)
        hi, lo = _split2(l1m)
        sps.append(sp)
        cats.append(jnp.concatenate([hi, lo], axis=0))
    rs = [_dot(c, u2_ref[...]) for c in cats]
    ws, new_c = [], []
    for s, sp, r, carry2 in zip(ss, sps, rs, carries):
        r = r[:n] + r[n:]
        kb2 = s.shape[1]
        w = jnp.exp((s - sp) + r[:, :kb2] + carry2)
        if mask2 is not None:
            w = jnp.where(mask2, w, 0.0)
        ws.append(w.astype(BF16))
        new_c.append(carry2 + r[:, kb2:])
    return new_c, [acc + _dot_nt(w, bd(vt)) for acc, w, vt in zip(accs, ws, vts)]


def _sb_prompt_body(q_ref, kt_ref, vt_ref, u2_ref, o_ref, carry_s, acc_s, *, Q):
    i = pl.program_id(1)
    npair = SB_W // LANES
    tri = lax.broadcasted_iota(jnp.int32, (Q, Q), 1) < lax.broadcasted_iota(jnp.int32, (Q, Q), 0)
    mask2 = jnp.concatenate([tri, tri], axis=1)
    row_lo = lax.broadcasted_iota(jnp.int32, (LANES, Q), 0) < HEAD_DIM
    slab = [slice(LANES * p, LANES * (p + 1)) for p in range(npair)]

    def step(j, carries, accs, mask):
        qs = [(q_ref[:, slab[p]] * (HEAD_DIM ** -0.5)).astype(BF16) for p in range(npair)]
        new_c, new_a = _sb_pairs_step(qs, [kt_ref[j, slab[p], :] for p in range(npair)],
                                      [vt_ref[j, slab[p], :] for p in range(npair)], u2_ref, row_lo,
                                      carries, accs, mask)
        for p in range(npair):
            carry_s[p] = new_c[p]
            acc_s[p] = new_a[p]

    step(i, [jnp.zeros((Q, 2 * Q), F32)] * npair, [jnp.zeros((Q, LANES), F32)] * npair, mask2)

    def body(jj, _):
        step(i - 1 - jj, [carry_s[p] for p in range(npair)], [acc_s[p] for p in range(npair)], None)
        return 0

    lax.fori_loop(0, i, body, 0)
    for p in range(npair):
        o_ref[:, slab[p]] = acc_s[p]


def _sb_prompt(q, ktb, vtb, u2):
    B, T, _ = q.shape
    Q = LANES
    nblk = T // LANES
    kvspec = pl.BlockSpec((None, nblk, SB_W, LANES), lambda b, i: (b, 0, 0, 0))
    return pl.pallas_call(
        functools.partial(_sb_prompt_body, Q=Q), grid=(B, T // Q),
        in_specs=[pl.BlockSpec((None, Q, SB_W), lambda b, i: (b, i, 0)), kvspec, kvspec,
                  pl.BlockSpec(u2.shape, lambda b, i: (0, 0))],
        out_specs=pl.BlockSpec((None, Q, SB_W), lambda b, i: (b, i, 0)),
        out_shape=jax.ShapeDtypeStruct((B, T, SB_W), F32),
        scratch_shapes=[pltpu.VMEM((SB_W // LANES, Q, 2 * Q), F32), pltpu.VMEM((SB_W // LANES, Q, LANES), F32)],
        compiler_params=_cparams(("parallel", "parallel"), big=True), name="sb_prompt")(q, ktb, vtb, u2)


def _mem_body(q_ref, k_ref, v_ref, o_ref):
    for h in range(MEM_HEADS):
        sl = slice(MEM_HEAD_DIM * h, MEM_HEAD_DIM * (h + 1))
        s = _dot_nt(q_ref[:, sl].astype(BF16), k_ref[:, sl].astype(BF16)) * (MEM_HEAD_DIM ** -0.5)
        e = jnp.exp(s - jnp.max(s, axis=-1, keepdims=True))
        p = e / jnp.sum(e, axis=-1, keepdims=True)
        o_ref[:, sl] = _dot(p.astype(BF16), v_ref[:, sl].astype(BF16))


def _mem_attn(q, mk, mv, tq=512):
    B, T, W = q.shape
    tq = min(tq, T)
    N = mk.shape[1]
    kvspec = pl.BlockSpec((None, N, W), lambda b, i: (b, 0, 0))
    return pl.pallas_call(
        _mem_body, grid=(B, T // tq),
        in_specs=[pl.BlockSpec((None, tq, W), lambda b, i: (b, i, 0)), kvspec, kvspec],
        out_specs=pl.BlockSpec((None, tq, W), lambda b, i: (b, i, 0)),
        out_shape=jax.ShapeDtypeStruct((B, T, W), F32),
        compiler_params=_cparams(("parallel", "parallel")), name="mem_attn")(q, mk, mv)


def _rglru_body(x_ref, g_ref, cb_ref, h0_ref, cw_ref, cbias_ref, wbd_ref, gb_ref, lam_ref,
                y_ref, hn_ref, cn_ref, xbuf, a_s, b_s, h_s, hc, *, n_t):
    t = pl.program_id(1)
    tt = x_ref.shape[0]
    hist = CONV_W - 1

    @pl.when(t == 0)
    def _():
        xbuf[8 - hist:8, :] = cb_ref[...]
        hc[...] = h0_ref[...]

    xbuf[8:8 + tt, :] = x_ref[...]
    xc = cw_ref[0:1, :] * xbuf[8 - hist:8 - hist + tt, :]
    for k in range(1, CONV_W):
        xc = xc + cw_ref[k:k + 1, :] * xbuf[8 - hist + k:8 - hist + k + tt, :]
    xc = xc + cbias_ref[...]
    gates = _dot(xc.astype(BF16), wbd_ref[...]) + gb_ref[...]
    r = jax.nn.sigmoid(gates[:, :D_RNN])
    ig = jax.nn.sigmoid(gates[:, D_RNN:])
    log_a = (-LRU_C * _softplus(-lam_ref[...])) * r
    a = jnp.exp(log_a)
    a_s[...] = a
    b_s[...] = jnp.sqrt(-jnp.tanh(log_a) * (a * a + 1.0)) * (ig * xc)

    def step(k, h):
        h = a_s[pl.ds(k, 1), :] * h + b_s[pl.ds(k, 1), :]
        h_s[pl.ds(k, 1), :] = h
        return h

    h = lax.fori_loop(0, tt, step, hc[...], unroll=8 if tt % 8 == 0 else tt)
    hc[...] = h
    y_ref[...] = h_s[...] * jax.nn.gelu(g_ref[...])
    hn_ref[...] = h
    cn_ref[...] = xbuf[8 + tt - hist:8 + tt, :]
    if n_t > 1:
        xbuf[0:8, :] = xbuf[tt:tt + 8, :]


def _rglru(x, gate_in, conv_buf, h0, conv_w, conv_b, wbd, gbias, lam, tt=256):
    B, T, W = x.shape
    tt = min(tt, T)
    n_t = T // tt
    xspec = pl.BlockSpec((None, tt, W), lambda b, t: (b, t, 0))
    const = lambda a: pl.BlockSpec(a.shape, lambda b, t: (0,) * a.ndim)
    return pl.pallas_call(
        functools.partial(_rglru_body, n_t=n_t), grid=(B, n_t),
        in_specs=[xspec, xspec, pl.BlockSpec((None, CONV_W - 1, W), lambda b, t: (b, 0, 0)),
                  pl.BlockSpec((None, 1, W), lambda b, t: (b, 0, 0)),
                  const(conv_w), const(conv_b), const(wbd), const(gbias), const(lam)],
        out_specs=[xspec, pl.BlockSpec((None, 1, W), lambda b, t: (b, 0, 0)),
                   pl.BlockSpec((None, CONV_W - 1, W), lambda b, t: (b, 0, 0))],
        out_shape=[jax.ShapeDtypeStruct((B, T, W), F32), jax.ShapeDtypeStruct((B, 1, W), F32),
                   jax.ShapeDtypeStruct((B, CONV_W - 1, W), F32)],
        scratch_shapes=[pltpu.VMEM((tt + 8, W), F32), pltpu.VMEM((tt, W), F32), pltpu.VMEM((tt, W), F32),
                        pltpu.VMEM((tt, W), F32), pltpu.VMEM((1, W), F32)],
        compiler_params=_cparams(("parallel", "arbitrary")), name="rglru")(
            x, gate_in, conv_buf, h0, conv_w, conv_b, wbd, gbias, lam)


def _merge_body(x_ref, b0, b1, b2, b3, wmg_ref, wbr_ref, wo_ref, g_ref, be_ref, o_ref, *, alpha):
    x = x_ref[...]
    xb = x.astype(BF16)
    acc = None
    off = 0
    for n, br in enumerate((b0, b1, b2, b3)):
        wdt = br.shape[-1]
        gate = jax.nn.sigmoid(_dot(xb, wmg_ref[:, n * D_MODEL:(n + 1) * D_MODEL]))
        term = gate * _dot(br[...].astype(BF16), wbr_ref[off:off + wdt, :])
        acc = term if acc is None else acc + term
        off += wdt
    mix = _dot(acc.astype(BF16), wo_ref[...])
    o_ref[...] = _layer_norm(alpha * x + mix, g_ref[...], be_ref[...])


def _merge_ln(x, branches, wmg, wbr, wo, g, be, alpha, tm=256):
    M = x.shape[0]
    tm = min(tm, M)
    row = lambda w: pl.BlockSpec((tm, w), lambda i: (i, 0))
    const = lambda a: pl.BlockSpec(a.shape, lambda i: (0, 0))
    return pl.pallas_call(
        functools.partial(_merge_body, alpha=alpha), grid=(M // tm,),
        in_specs=[row(D_MODEL)] + [row(b.shape[1]) for b in branches] + [const(wmg), const(wbr), const(wo),
                                                                         const(g), const(be)],
        out_specs=row(D_MODEL), out_shape=jax.ShapeDtypeStruct((M, D_MODEL), F32),
        compiler_params=_cparams(("parallel",), big=True), name="merge_ln")(x, *branches, wmg, wbr, wo, g, be)


def _ffn_body(x_ref, wa_ref, wb_ref, w2_ref, g_ref, be_ref, o_ref, acc, *, alpha, nf):
    j = pl.program_id(1)
    x = x_ref[...]
    xb = x.astype(BF16)
    a = _dot(xb, wa_ref[...])
    b = _dot(xb, wb_ref[...])
    part = _dot(((a * jax.nn.sigmoid(a)) * b).astype(BF16), w2_ref[...])

    @pl.when(j == 0)
    def _():
        acc[...] = part

    @pl.when(j > 0)
    def _():
        acc[...] += part

    @pl.when(j == nf - 1)
    def _():
        o_ref[...] = _layer_norm(alpha * x + acc[...], g_ref[...], be_ref[...])


def _ffn_ln(x, w13, w2, g, be, alpha, tm=512, tf=1408):
    M = x.shape[0]
    tm = min(tm, M)
    F = w2.shape[0]
    nf = F // tf
    const = lambda a: pl.BlockSpec(a.shape, lambda i, j: (0, 0))
    return pl.pallas_call(
        functools.partial(_ffn_body, alpha=alpha, nf=nf), grid=(M // tm, nf),
        in_specs=[pl.BlockSpec((tm, D_MODEL), lambda i, j: (i, 0)),
                  pl.BlockSpec((D_MODEL, tf), lambda i, j: (0, j)),
                  pl.BlockSpec((D_MODEL, tf), lambda i, j: (0, j + nf)),
                  pl.BlockSpec((tf, D_MODEL), lambda i, j: (j, 0)), const(g), const(be)],
        out_specs=pl.BlockSpec((tm, D_MODEL), lambda i, j: (i, 0)),
        out_shape=jax.ShapeDtypeStruct((M, D_MODEL), F32),
        scratch_shapes=[pltpu.VMEM((tm, D_MODEL), F32)],
        compiler_params=_cparams(("parallel", "arbitrary"), big=True), name="ffn_ln")(x, w13, w13, w2, g, be)


def _moe_body(x_ref, rwh_ref, rwl_ref, rb_ref, wa_ref, wb_ref, w2_ref, g_ref, be_ref, o_ref, acc, comb,
              *, alpha, n_exp):
    e = pl.program_id(1)
    x = x_ref[...]
    xh = x.astype(BF16)
    tm = x.shape[0]
    lane = lax.broadcasted_iota(jnp.int32, (tm, LANES), 1)

    @pl.when(e == 0)
    def _():
        xl = (x - xh.astype(F32)).astype(BF16)
        logits = (_dot(xh, rwh_ref[...]) + _dot(xh, rwl_ref[...])) + _dot(xl, rwh_ref[...]) + rb_ref[...]
        valid = lane < n_exp
        lanef = lane.astype(F32)
        logits = jnp.where(valid, logits, NEG)
        ex = jnp.where(valid, jnp.exp(logits - jnp.max(logits, axis=-1, keepdims=True)), 0.0)
        p = ex / jnp.sum(ex, axis=-1, keepdims=True)
        p = jnp.where(valid, p, -1.0)
        m1 = jnp.max(p, axis=-1, keepdims=True)
        hit1 = lanef == jnp.min(jnp.where(p == m1, lanef, 1.0e9), axis=-1, keepdims=True)
        p2 = jnp.where(hit1, -1.0, p)
        m2 = jnp.max(p2, axis=-1, keepdims=True)
        hit2 = lanef == jnp.min(jnp.where(p2 == m2, lanef, 1.0e9), axis=-1, keepdims=True)
        den = m1 + m2
        comb[...] = jnp.where(hit1, m1 / den, 0.0) + jnp.where(hit2, m2 / den, 0.0)
        acc[...] = jnp.zeros_like(acc)

    w_e = jnp.sum(jnp.where(lane == e, comb[...], 0.0), axis=-1, keepdims=True)
    a = _dot(xh, wa_ref[...])
    b = _dot(xh, wb_ref[...])
    acc[...] += w_e * _dot(((a * jax.nn.sigmoid(a)) * b).astype(BF16), w2_ref[...])

    @pl.when(e == n_exp - 1)
    def _():
        o_ref[...] = _layer_norm(alpha * x + acc[...], g_ref[...], be_ref[...])


def _moe_ln(x, rwh, rwl, rb, w13, w2, g, be, alpha, tm=512):
    M = x.shape[0]
    tm = min(tm, M)
    n_exp, de, _ = w2.shape
    const = lambda a: pl.BlockSpec(a.shape, lambda i, e: (0, 0))
    return pl.pallas_call(
        functools.partial(_moe_body, alpha=alpha, n_exp=n_exp), grid=(M // tm, n_exp),
        in_specs=[pl.BlockSpec((tm, D_MODEL), lambda i, e: (i, 0)), const(rwh), const(rwl), const(rb),
                  pl.BlockSpec((None, D_MODEL, de), lambda i, e: (e, 0, 0)),
                  pl.BlockSpec((None, D_MODEL, de), lambda i, e: (e, 0, 1)),
                  pl.BlockSpec((None, de, D_MODEL), lambda i, e: (e, 0, 0)), const(g), const(be)],
        out_specs=pl.BlockSpec((tm, D_MODEL), lambda i, e: (i, 0)),
        out_shape=jax.ShapeDtypeStruct((M, D_MODEL), F32),
        scratch_shapes=[pltpu.VMEM((tm, D_MODEL), F32), pltpu.VMEM((tm, LANES), F32)],
        compiler_params=_cparams(("parallel", "arbitrary"), big=True), name="moe_ln")(
            x, rwh, rwl, rb, w13, w13, w2, g, be)


def _row_consts(rows, Td, past):
    row = lax.broadcasted_iota(jnp.int32, (rows, 1), 0)
    head = row // Td
    qpos = past + (row - head * Td)
    slope = jnp.zeros((rows, 1), F32)
    for h in range(NSA_HEADS):
        slope = jnp.where(head == h, SLOPES[h], slope)
    return qpos, slope


def _nsa_s_cmp_body(qz_ref, ak_ref, bk_ref, av_ref, bv_ref, ovl_ref, hs_ref, ocmp_ref, sel_ref, *, Td, past, n_slc):
    rows = qz_ref.shape[0]
    NC = ak_ref.shape[0]
    qpos, slope = _row_consts(rows, Td, past)
    ck = (ak_ref[...] + pltpu.roll(bk_ref[...], NC - 1, 0)).astype(BF16)
    cv = (av_ref[...] + pltpu.roll(bv_ref[...], NC - 1, 0)).astype(BF16)
    qz = (qz_ref[...] * (HEAD_DIM ** -0.5)).astype(BF16)
    cend = CMP_STRIDE * lax.broadcasted_iota(jnp.int32, (1, NC), 1) + (CMP_BLOCK - 1)
    dist = qpos - cend
    p = _masked_softmax(_dot_nt(qz, ck) - slope * dist.astype(F32), dist >= 0)
    ocmp_ref[...] = _dot(p.astype(BF16), cv)
    ph, pl_ = _split2(p)
    imp_h = _dot(ph, ovl_ref[...]) + _dot(pl_, ovl_ref[...])
    i1, i2, i3 = _split3(imp_h)
    hs = hs_ref[...]
    imp = (_dot(hs, i1) + _dot(hs, i2)) + _dot(hs, i3)
    NSP = imp.shape[1]
    lane = lax.broadcasted_iota(jnp.int32, (rows, NSP), 1)
    cur = qpos >> 6
    valid = (lane * SLC_BLOCK <= qpos) & (lane < n_slc)
    forced = (lane == 0) | (lane == cur) | (lane == cur - 1)
    score = jnp.where(valid, jnp.where(forced, FORCE_SCORE, imp), jnp.where(lane < n_slc, -1.0, -2.0))
    sel_ref[...] = _topk_mask(score, lane.astype(F32), SLC_TOP)


def _nsa_s_cmp(qz, ak, bk, av, bv, ovl, hs, Td, past, n_slc):
    Bd, rows, _ = qz.shape
    NC = ak.shape[1]
    NSP = ovl.shape[1]
    cspec = pl.BlockSpec((None, NC, KV_W), lambda b: (b, 0, 0))
    return pl.pallas_call(
        functools.partial(_nsa_s_cmp_body, Td=Td, past=past, n_slc=n_slc), grid=(Bd,),
        in_specs=[pl.BlockSpec((None, rows, KV_W), lambda b: (b, 0, 0)), cspec, cspec, cspec, cspec,
                  pl.BlockSpec(ovl.shape, lambda b: (0, 0)), pl.BlockSpec(hs.shape, lambda b: (0, 0))],
        out_specs=[pl.BlockSpec((None, rows, KV_W), lambda b: (b, 0, 0)),
                   pl.BlockSpec((None, rows, NSP), lambda b: (b, 0, 0))],
        out_shape=[jax.ShapeDtypeStruct((Bd, rows, KV_W), F32), jax.ShapeDtypeStruct((Bd, rows, NSP), F32)],
        compiler_params=_cparams(("parallel",), big=True), name="nsa_sample_cmp")(qz, ak, bk, av, bv, ovl, hs)


def _nsa_s_slc_body(pt_ref, *refs, P, Td, past, n_steps):
    kp = refs[:P]
    vp = refs[P:2 * P]
    (qz_ref, g_ref, ocmp_ref, sel_ref, nk_ref, nv_ref, wk_ref, wv_ref, o_ref, m_s, l_s, acc_s) = refs[2 * P:]
    s_id = pl.program_id(1)
    rows = qz_ref.shape[0]
    NSP = sel_ref.shape[1]
    qpos, slope = _row_consts(rows, Td, past)
    qz = (qz_ref[...] * (HEAD_DIM ** -0.5)).astype(BF16)
    sel_b = sel_ref[...].astype(BF16)

    @pl.when(s_id == 0)
    def _():
        m_s[...] = jnp.full_like(m_s, NEG)
        l_s[...] = jnp.zeros_like(l_s)
        acc_s[...] = jnp.zeros_like(acc_s)

    def tile(kt, vt, kpos, extra):
        nk = kt.shape[1]
        expand = ((kpos >> 6) == lax.broadcasted_iota(jnp.int32, (NSP, nk), 0)).astype(BF16)
        dist = qpos - kpos
        mask = (_dot(sel_b, expand) > 0.5) & (dist >= 0)
        if extra is not None:
            mask = mask & extra
        s = jnp.where(mask, _dot(qz, kt) - slope * dist.astype(F32), NEG)
        m_old = m_s[...]
        m_new = jnp.maximum(m_old, jnp.max(s, axis=-1, keepdims=True))
        alpha = jnp.exp(m_old - m_new)
        e = jnp.where(mask, jnp.exp(s - m_new), 0.0)
        l_s[...] = alpha * l_s[...] + jnp.sum(e, axis=-1, keepdims=True)
        acc_s[...] = alpha * acc_s[...] + _dot_nt(e.astype(BF16), vt)
        m_s[...] = m_new

    kt = jnp.concatenate([r[...] for r in kp], axis=1).astype(BF16)
    vt = jnp.concatenate([r[...] for r in vp], axis=1).astype(BF16)
    tile(kt, vt, s_id * (P * LANES) + lax.broadcasted_iota(jnp.int32, (1, P * LANES), 1), None)

    @pl.when(s_id == n_steps - 1)
    def _():
        nl = lax.broadcasted_iota(jnp.int32, (1, LANES), 1)
        tile(nk_ref[...].astype(BF16), nv_ref[...].astype(BF16), past + nl, nl < Td)
        o_slc = acc_s[...] / jnp.maximum(l_s[...], TINY)
        nw = wk_ref.shape[1]
        wl = lax.broadcasted_iota(jnp.int32, (1, nw), 1)
        dist = qpos - (past - (nw - LANES) + wl)
        okw = (wl < nw - LANES + Td) & (dist >= 0) & (dist <= WINDOW)
        pw = _masked_softmax(_dot(qz, wk_ref[...].astype(BF16)) - slope * dist.astype(F32), okw)
        o_win = _dot_nt(pw.astype(BF16), wv_ref[...].astype(BF16))
        gate = jax.nn.sigmoid(g_ref[...])
        o_ref[...] = gate[:, 0:1] * ocmp_ref[...] + gate[:, 1:2] * o_slc + gate[:, 2:3] * o_win


def _nsa_s_slc(page_table, layer, skc, svc, qz, graw, ocmp, sel, nks, nvs, wkx, wvx, Td, past, P):
    Bd, NP = page_table.shape
    rows = qz.shape[1]
    n_steps = NP // P
    pspec = [pl.BlockSpec((None, None, KV_W, LANES), lambda b, s, pt, i=i: (layer, pt[b, s * P + i], 0, 0))
             for i in range(P)]
    per_b = lambda a: pl.BlockSpec((None,) + a.shape[1:], lambda b, s, pt: (b, 0, 0))
    grid_spec = pltpu.PrefetchScalarGridSpec(
        num_scalar_prefetch=1, grid=(Bd, n_steps),
        in_specs=pspec + pspec + [per_b(a) for a in (qz, graw, ocmp, sel, nks, nvs, wkx, wvx)],
        out_specs=pl.BlockSpec((None, rows, KV_W), lambda b, s, pt: (b, 0, 0)),
        scratch_shapes=[pltpu.VMEM((rows, 1), F32), pltpu.VMEM((rows, 1), F32), pltpu.VMEM((rows, KV_W), F32)])
    return pl.pallas_call(
        functools.partial(_nsa_s_slc_body, P=P, Td=Td, past=past, n_steps=n_steps), grid_spec=grid_spec,
        out_shape=jax.ShapeDtypeStruct((Bd, rows, KV_W), F32),
        compiler_params=_cparams(("parallel", "arbitrary"), big=True), name="nsa_sample_slc")(
            page_table, *([skc] * P), *([svc] * P), qz, graw, ocmp, sel, nks, nvs, wkx, wvx)


def _sb_s_body(pt_ref, *refs, P, Td, n_steps):
    kp = refs[:P]
    vp = refs[P:2 * P]
    q_ref, nk_ref, nv_ref, u_ref, o_ref, carry_s, acc_s = refs[2 * P:]
    s_id = pl.program_id(1)
    rows = q_ref.shape[0]
    qb = (q_ref[...] * (HEAD_DIM ** -0.5)).astype(BF16)
    u = u_ref[...]

    @pl.when(s_id == 0)
    def _():
        row = lax.broadcasted_iota(jnp.int32, (rows, LANES), 0)
        lane = lax.broadcasted_iota(jnp.int32, (rows, LANES), 1)
        newer = lane < (row - (row // Td) * Td)
        carry, acc = _sb_block(qb, nk_ref[...].astype(BF16), nv_ref[...].astype(BF16), u,
                               jnp.zeros((rows, 1), F32), jnp.zeros((rows, SB_W), F32), newer)
        carry_s[...] = carry
        acc_s[...] = acc

    carry, acc = carry_s[...], acc_s[...]
    order = range(P - 1, -1, -1)
    ss = [_dot(qb, kp[i][...].astype(BF16)) for i in order]
    sps = [_softplus(s) for s in ss]
    cats = [jnp.concatenate(_split2(-sp), axis=0) for sp in sps]
    afters = [_dot(c, u) for c in cats]
    ws = []
    for s, sp, r in zip(ss, sps, afters):
        ws.append(jnp.exp((s - sp) + (r[:rows] + r[rows:]) + carry).astype(BF16))
        carry = carry - jnp.sum(sp, axis=-1, keepdims=True)
    for w, i in zip(ws, order):
        acc = acc + _dot_nt(w, vp[i][...].astype(BF16))
    carry_s[...] = carry
    acc_s[...] = acc

    @pl.when(s_id == n_steps - 1)
    def _():
        o_ref[...] = acc


def _sb_sample(page_table, layer, kc, vc, qbd, nk, nv, u, Td, P):
    Bd, NP = page_table.shape
    rows = qbd.shape[1]
    n_steps = NP // P
    pspec = [pl.BlockSpec((None, None, SB_W, LANES),
                          lambda b, s, pt, i=i: (layer, pt[b, NP - (s + 1) * P + i], 0, 0)) for i in range(P)]
    per_b = lambda a: pl.BlockSpec((None,) + a.shape[1:], lambda b, s, pt: (b, 0, 0))
    grid_spec = pltpu.PrefetchScalarGridSpec(
        num_scalar_prefetch=1, grid=(Bd, n_steps),
        in_specs=pspec + pspec + [per_b(qbd), per_b(nk), per_b(nv), pl.BlockSpec(u.shape, lambda b, s, pt: (0, 0))],
        out_specs=pl.BlockSpec((None, rows, SB_W), lambda b, s, pt: (b, 0, 0)),
        scratch_shapes=[pltpu.VMEM((rows, 1), F32), pltpu.VMEM((rows, SB_W), F32)])
    return pl.pallas_call(
        functools.partial(_sb_s_body, P=P, Td=Td, n_steps=n_steps), grid_spec=grid_spec,
        out_shape=jax.ShapeDtypeStruct((Bd, rows, SB_W), F32),
        compiler_params=_cparams(("parallel", "arbitrary"), big=True), name="sb_sample")(
            page_table, *([kc] * P), *([vc] * P), qbd, nk, nv, u)


def _cmp_weights(w, pe):
    ratio = CMP_BLOCK // CMP_STRIDE
    w5 = w.reshape(2, ratio, CMP_STRIDE, HEAD_DIM, HEAD_DIM)
    eye = jnp.eye(NSA_KV, dtype=w.dtype)
    wbd = jnp.einsum('krcde,gh->krcgdhe', w5, eye).reshape(2, ratio, CMP_STRIDE * KV_W, KV_W)
    pe2 = jnp.broadcast_to(pe.reshape(ratio, CMP_STRIDE, 1, HEAD_DIM), (ratio, CMP_STRIDE, NSA_KV, HEAD_DIM))
    return wbd.astype(BF16), jnp.swapaxes(wbd, 2, 3).astype(BF16), pe2.reshape(ratio, 1, CMP_STRIDE * KV_W)


def _overlap(n_cmp_pad, n_slc_pad):
    n = np.arange(n_cmp_pad)[:, None] * CMP_STRIDE
    j = np.arange(n_slc_pad)[None, :] * SLC_BLOCK
    return jnp.asarray((n < j + SLC_BLOCK) & (n + CMP_BLOCK > j), dtype=BF16)


def _suffix_matrix(n):
    return jnp.asarray(np.arange(n)[:, None] > np.arange(n)[None, :], dtype=BF16)


def _pair_suffix_matrix(n):
    u = (np.arange(n)[:, None] > np.arange(n)[None, :]).astype(np.float32)
    one, z = np.ones((n, n), np.float32), np.zeros((n, n), np.float32)
    return jnp.asarray(np.block([[u, z, one, z], [z, u, z, one]]), dtype=BF16)


def _feature_major_state(xt, heads):
    d, b, _, t = xt.shape
    return xt.reshape(d, b, heads, HEAD_DIM, t).transpose(0, 1, 4, 2, 3)


def kernel(x_prompt, x_sample, mem_prompt, cache_nsa_cmp_k, cache_nsa_cmp_v, cache_nsa_slc_k, cache_nsa_slc_v, cache_nsa_win_k, cache_nsa_win_v, cache_sb_k, cache_sb_v, cache_mem_k, cache_mem_v, state_lru_h, state_lru_conv, page_table, w_in, nsa_cmp_w, nsa_cmp_pe, lru_conv_w, lru_conv_b, lru_wa, lru_ba, lru_wx, lru_bx, lru_lambda, w_mem_kv, w_branch, w_out, ln1_g, ln1_b, ln2_g, ln2_b, ffn_w13, ffn_w2, moe_router_w, moe_router_b, moe_w13, moe_w2):
    depth = w_in.shape[0]
    B, T, _ = x_prompt.shape
    Bd, Td, _ = x_sample.shape
    NP = page_table.shape[1]
    page = cache_nsa_cmp_k.shape[2]
    past = NP * page
    win_buf = cache_nsa_win_k.shape[2]
    n_mem = mem_prompt.shape[1]
    alpha = (2.0 * depth) ** 0.25
    assert page == LANES and T % 512 == 0 and T // SLC_BLOCK >= SLC_TOP and T // SLC_BLOCK <= LANES
    assert win_buf == WINDOW and Td < CMP_STRIDE and NP % 8 == 0
    P = 8
    rows = NSA_HEADS * Td

    fm = lambda c: c.transpose(0, 1, 3, 4, 2).reshape(c.shape[0], c.shape[1], c.shape[3] * c.shape[4], c.shape[2])
    slc_kc, slc_vc, sb_kc, sb_vc = fm(cache_nsa_slc_k), fm(cache_nsa_slc_v), fm(cache_sb_k), fm(cache_sb_v)
    win_kc, win_vc = fm(cache_nsa_win_k), fm(cache_nsa_win_v)
    cmp_kc, cmp_vc = fm(cache_nsa_cmp_k), fm(cache_nsa_cmp_v)
    P_cmp = 16 if NP % 16 == 0 else 8

    n_cmp_p = T // CMP_STRIDE
    ovl_p = _overlap(n_cmp_p, LANES)
    n_slc_s = -(-(past + Td) // SLC_BLOCK)
    nsp_s = -(-n_slc_s // LANES) * LANES
    n_cmp_s = past // CMP_STRIDE
    ovl_s = _overlap(n_cmp_s, nsp_s)
    u_blk = _suffix_matrix(LANES)
    u_pair = _pair_suffix_matrix(LANES)
    hs = jnp.asarray((np.arange(rows)[:, None] // (NSA_GROUP * Td) == np.arange(rows)[None, :] // (NSA_GROUP * Td))
                     & (np.arange(rows)[:, None] % Td == np.arange(rows)[None, :] % Td), dtype=BF16)

    xp = x_prompt
    xs = x_sample.reshape(1, Bd * Td, D_MODEL)
    mem2 = mem_prompt.reshape(1, B * n_mem, D_MODEL)
    p_layers, s_layers = [], []
    for l in range(depth):
        wt = jnp.swapaxes(w_in[l], 0, 1)
        c = np.cumsum([0, 512, 128, 128, 128, 128, 128, 128, 24, 512, 512, 512, 512, 512, 512, 4096])
        seg = lambda a, b: wt[c[a]:c[b]]
        ng_w = jnp.pad(seg(7, 8), ((0, LANES - 24), (0, 0)))
        wn = jnp.concatenate([seg(0, 1), ng_w, seg(8, 10), seg(10, 11), seg(13, 14), seg(1, 3)], axis=0)
        wn = jnp.swapaxes(wn, 0, 1).astype(BF16)
        wtr = jnp.concatenate([seg(1, 7), seg(11, 13)], axis=0).astype(BF16)
        w_all = jnp.swapaxes(jnp.concatenate([seg(0, 7), ng_w, seg(8, 14)], axis=0), 0, 1).astype(BF16)
        wmg = jnp.swapaxes(seg(14, 15), 0, 1).astype(BF16)
        cw_tok, cw_feat, pe2 = _cmp_weights(nsa_cmp_w[l], nsa_cmp_pe[l])
        eye8 = jnp.eye(LRU_BLOCKS, dtype=F32)
        bd = lambda w: jnp.einsum('ncd,nm->ncmd', w, eye8).reshape(D_RNN, D_RNN)
        wbd = jnp.concatenate([bd(lru_wa[l]), bd(lru_wx[l])], axis=1).astype(BF16)
        gbias = jnp.concatenate([lru_ba[l].reshape(1, D_RNN), lru_bx[l].reshape(1, D_RNN)], axis=1)
        lam = lru_lambda[l].reshape(1, D_RNN)
        conv_w, conv_b = lru_conv_w[l], lru_conv_b[l].reshape(1, D_RNN)
        wmem = w_mem_kv[l].astype(BF16)
        wbr, wo = w_branch[l].astype(BF16), w_out[l].astype(BF16)
        g1, b1 = ln1_g[l].reshape(1, D_MODEL), ln1_b[l].reshape(1, D_MODEL)
        g2, b2 = ln2_g[l].reshape(1, D_MODEL), ln2_b[l].reshape(1, D_MODEL)

        ncols = [(0, 512), (512, 128), (640, 512), (1152, 512), (1664, 512), (2176, 512), (2688, 128), (2816, 128)]
        tcols = [(0, 128, ("full",)), (128, 128, ("full",)), (256, 128, ("full", "blk")), (384, 128, ("full", "blk")),
                 (512, 128, ("full", "blk")), (640, 128, ("full", "blk")), (768, 512, ("full", "blk")),
                 (1280, 512, ("full", "blk"))]
        (nq, ng, lx, lg, sq, mq, ck_tok, cv_tok, ck_t, cv_t, sk_t, sk_b, sv_t, sv_b, wk_t, wk_b, wv_t, wv_b,
         sbk_t, sbk_b, sbv_t, sbv_b) = _proj(xp, wn, ncols, wtr, tcols)
        ckc, cvc = _compress(ck_tok.reshape(B, n_cmp_p, CMP_STRIDE * KV_W), cv_tok.reshape(B, n_cmp_p, CMP_STRIDE * KV_W),
                             cw_feat[0], cw_feat[1], pe2)
        o_nsa = _nsa_prompt(nq, ng, ckc, cvc, sk_b, sv_b, wk_b, wv_b, ovl_p)
        o_lru, h_new, conv_new = _rglru(lx, lg, jnp.zeros((B, CONV_W - 1, D_RNN), F32), jnp.zeros((B, 1, D_RNN), F32),
                                        conv_w, conv_b, wbd, gbias, lam)
        o_sb = _sb_prompt(sq, sbk_b, sbv_b, u_pair)
        mk, mv = _proj(mem2, wmem, [(0, 512), (512, 512)])
        mk, mv = mk.reshape(B, n_mem, 512), mv.reshape(B, n_mem, 512)
        o_mem = _mem_attn(mq, mk, mv)
        M = B * T
        x1 = _merge_ln(xp.reshape(M, D_MODEL), [o.reshape(M, 512) for o in (o_nsa, o_lru, o_sb, o_mem)],
                       wmg, wbr, wo, g1, b1, alpha)
        p_layers.append((ck_t, cv_t, sk_t, sv_t, wk_t[:, :, T - min(WINDOW, T):], wv_t[:, :, T - min(WINDOW, T):],
                         sbk_t, sbv_t, h_new.reshape(B, D_RNN), conv_new,
                         mk.reshape(B, n_mem, MEM_HEADS, MEM_HEAD_DIM), mv.reshape(B, n_mem, MEM_HEADS, MEM_HEAD_DIM)))

        ms = Bd * Td
        scols = [(0, 512), (512, 128), (640, 128), (768, 128), (896, 128), (1024, 128), (1152, 128), (1280, 128),
                 (1408, 512), (1920, 512), (2432, 512), (2944, 512), (3456, 512), (3968, 512)]
        (nq_s, ck_s, cv_s, sk_s, sv_s, wk_s, wv_s, ng_s, lx_s, lg_s, sq_s, sbk_s, sbv_s, mq_s) = [
            a.reshape(Bd, Td, a.shape[-1]) for a in _proj(xs, w_all, scols)]
        ak, bk = _compress_paged(cmp_kc, l, page_table, cw_tok[0], pe2, P_cmp)
        av, bv = _compress_paged(cmp_vc, l, page_table, cw_tok[1], pe2, P_cmp)
        q8 = nq_s.reshape(Bd, Td, NSA_HEADS, HEAD_DIM).transpose(0, 2, 1, 3)
        gsel = jnp.asarray(np.arange(NSA_HEADS)[:, None] // NSA_GROUP == np.arange(NSA_KV)[None, :], dtype=F32)
        qz = (q8[:, :, :, None, :] * gsel[None, :, None, :, None]).reshape(Bd, rows, KV_W)
        hsel = jnp.eye(SB_HEADS, dtype=F32)
        s8 = sq_s.reshape(Bd, Td, SB_HEADS, HEAD_DIM).transpose(0, 2, 1, 3)
        qbd = (s8[:, :, :, None, :] * hsel[None, :, None, :, None]).reshape(Bd, rows, SB_W)
        graw = jnp.pad(ng_s[:, :, :24].reshape(Bd, Td, NSA_HEADS, 3).transpose(0, 2, 1, 3).reshape(Bd, rows, 3),
                       ((0, 0), (0, 0), (0, LANES - 3)))
        new_fm = lambda a: jnp.pad(jnp.swapaxes(a, 1, 2), ((0, 0), (0, 0), (0, LANES - Td)))
        ocmp, sel = _nsa_s_cmp(qz, ak, bk, av, bv, ovl_s, hs, Td, past, n_slc_s)
        wkx = jnp.concatenate([win_kc[l], new_fm(wk_s)], axis=2)
        wvx = jnp.concatenate([win_vc[l], new_fm(wv_s)], axis=2)
        o_rows = _nsa_s_slc(page_table, l, slc_kc, slc_vc, qz, graw, ocmp, sel, new_fm(sk_s), new_fm(sv_s),
                            wkx, wvx, Td, past, P)
        o6 = o_rows.reshape(Bd, NSA_KV, NSA_GROUP, Td, NSA_KV, HEAD_DIM)
        o_nsa_s = jnp.stack([o6[:, g, :, :, g, :] for g in range(NSA_KV)], axis=1)
        o_nsa_s = o_nsa_s.reshape(Bd, NSA_HEADS, Td, HEAD_DIM).transpose(0, 2, 1, 3).reshape(Bd, Td, NSA_HEADS * HEAD_DIM)
        o_lru_s, h_new_s, conv_new_s = _rglru(lx_s, lg_s, state_lru_conv[l], state_lru_h[l].reshape(Bd, 1, D_RNN),
                                              conv_w, conv_b, wbd, gbias, lam)
        sb_rows = _sb_sample(page_table, l, sb_kc, sb_vc, qbd, new_fm(sbk_s), new_fm(sbv_s), u_blk, Td, P)
        sb5 = sb_rows.reshape(Bd, SB_HEADS, Td, SB_HEADS, HEAD_DIM)
        o_sb_s = jnp.stack([sb5[:, h, :, h, :] for h in range(SB_HEADS)], axis=2).reshape(Bd, Td, SB_W)
        o_mem_s = _mem_attn(mq_s, cache_mem_k[l].reshape(Bd, n_mem, 512), cache_mem_v[l].reshape(Bd, n_mem, 512))
        x1s = _merge_ln(xs.reshape(ms, D_MODEL), [o.reshape(ms, 512) for o in (o_nsa_s, o_lru_s, o_sb_s, o_mem_s)],
                        wmg, wbr, wo, g1, b1, alpha)
        kv4 = lambda a: a.reshape(Bd, Td, NSA_KV, HEAD_DIM)
        s_layers.append((kv4(ck_s), kv4(cv_s), kv4(sk_s), kv4(sv_s), kv4(wk_s), kv4(wv_s),
                         sbk_s.reshape(Bd, Td, SB_HEADS, HEAD_DIM), sbv_s.reshape(Bd, Td, SB_HEADS, HEAD_DIM),
                         h_new_s.reshape(Bd, D_RNN), conv_new_s))

        if l % 2 == 0:
            w13, w2 = ffn_w13[l // 2].astype(BF16), ffn_w2[l // 2].astype(BF16)
            x2 = _ffn_ln(x1, w13, w2, g2, b2, alpha)
            x2s = _ffn_ln(x1s, w13, w2, g2, b2, alpha)
        else:
            rw = jnp.pad(moe_router_w[l // 2], ((0, 0), (0, LANES - N_EXP)))
            rwh = rw.astype(BF16)
            rwl = (rw - rwh.astype(F32)).astype(BF16)
            rb = jnp.pad(moe_router_b[l // 2].reshape(1, N_EXP), ((0, 0), (0, LANES - N_EXP)))
            w13, w2 = moe_w13[l // 2].astype(BF16), moe_w2[l // 2].astype(BF16)
            x2 = _moe_ln(x1, rwh, rwl, rb, w13, w2, g2, b2, alpha)
            x2s = _moe_ln(x1s, rwh, rwl, rb, w13, w2, g2, b2, alpha)
        xp = x2.reshape(B, T, D_MODEL)
        xs = x2s.reshape(1, ms, D_MODEL)

    pst = [jnp.stack(r, axis=0) for r in zip(*p_layers)]
    prompt_state = ([_feature_major_state(a, NSA_KV) for a in pst[:6]] + [_feature_major_state(a, SB_HEADS) for a in pst[6:8]]
                    + pst[8:])
    sample_state = [jnp.stack(r, axis=0) for r in zip(*s_layers)]
    return (xp, xs.reshape(Bd, Td, D_MODEL), *prompt_state, *sample_state)
```

```python
import functools

import numpy as np
import jax
import jax.numpy as jnp
from jax import lax
from jax.experimental import pallas as pl
from jax.experimental.pallas import tpu as pltpu

F32 = jnp.float32
BF16 = jnp.bfloat16

D_MODEL = 1024
HEAD_DIM = 64
NSA_HEADS = 8
NSA_KV = 2
NSA_GROUP = NSA_HEADS // NSA_KV
CMP_STRIDE = 16
CMP_BLOCK = 32
SLC_BLOCK = 64
SLC_TOP = 16
WINDOW = 512
FORCE_SCORE = 1.0e4
SB_HEADS = 8
D_RNN = 512
LRU_BLOCKS = 8
CONV_W = 4
LRU_C = 8.0
MEM_HEADS = 4
MEM_HEAD_DIM = 128
N_EXP = 8
LN_EPS = 1e-5
NEG = -1.0e30
TINY = 1.0e-30
SB_DEAD_LOG = -106.0
KV_W = NSA_KV * HEAD_DIM
SB_W = SB_HEADS * HEAD_DIM
LANES = 128
VMEM_LIMIT = 56 * 1024 * 1024
SLOPES = tuple(float(2.0 ** (-8.0 * (h + 1) / NSA_HEADS)) for h in range(NSA_HEADS))


def _cparams(sem, big=False):
    return pltpu.CompilerParams(dimension_semantics=sem, vmem_limit_bytes=VMEM_LIMIT if big else None)


def _dot(a, b):
    return jnp.dot(a, b, preferred_element_type=F32)


def _dot_nt(a, b):
    return lax.dot_general(a, b, (((1,), (1,)), ((), ())), preferred_element_type=F32)


def _split2(x):
    hi = x.astype(BF16)
    lo = (x - hi.astype(F32)).astype(BF16)
    return hi, lo


def _split3(x):
    hi = x.astype(BF16)
    r = x - hi.astype(F32)
    mid = r.astype(BF16)
    lo = (r - mid.astype(F32)).astype(BF16)
    return hi, mid, lo


def _softplus(x):
    return jnp.maximum(x, 0.0) + jnp.log1p(jnp.exp(-jnp.abs(x)))


def _layer_norm(z, g, b):
    mu = jnp.mean(z, axis=-1, keepdims=True)
    zc = z - mu
    var = jnp.mean(zc * zc, axis=-1, keepdims=True)
    return zc * lax.rsqrt(var + LN_EPS) * g + b


def _masked_softmax(s, mask):
    s = jnp.where(mask, s, NEG)
    m = jnp.max(s, axis=-1, keepdims=True)
    e = jnp.where(mask, jnp.exp(s - m), 0.0)
    return e / jnp.maximum(jnp.sum(e, axis=-1, keepdims=True), TINY)


def _topk_mask(score, lanef, k):
    sel = jnp.zeros_like(score)
    s = score
    for _ in range(k):
        m = jnp.max(s, axis=-1, keepdims=True)
        jm = jnp.min(jnp.where(s == m, lanef, 1.0e9), axis=-1, keepdims=True)
        hit = lanef == jm
        sel = jnp.where(hit, 1.0, sel)
        s = jnp.where(hit, -3.0, s)
    return sel


def _topk_rank(score_t, k):
    R, N = score_t.shape
    nv = R // 8
    groups = [score_t[8 * v:8 * v + 8] for v in range(nv)]
    sub = lax.broadcasted_iota(jnp.int32, (8, N), 0)
    ranks = [jnp.zeros((8, N), F32) for _ in range(nv)]
    for jp in range(R):
        sj = score_t[jp:jp + 1, :]
        vj, rj = divmod(jp, 8)
        for v in range(nv):
            if v > vj:
                beats = sj >= groups[v]
            elif v < vj:
                beats = sj > groups[v]
            else:
                beats = (sj > groups[v]) | ((sj == groups[v]) & (sub > rj))
            ranks[v] = ranks[v] + jnp.where(beats, 1.0, 0.0)
    return jnp.concatenate([jnp.where(r < k, 1.0, 0.0) for r in ranks], axis=0)


def _proj_body(*refs, ncols, tcols, has_t):
    x_ref, wn_ref = refs[0], refs[1]
    o_refs = refs[3:] if has_t else refs[2:]
    xb = x_ref[...].astype(BF16)
    k = 0
    for c0, wd in ncols:
        o_refs[k][...] = _dot(xb, wn_ref[:, c0:c0 + wd]).astype(o_refs[k].dtype)
        k += 1
    if has_t:
        wt_ref = refs[2]
        for c0, wd, kinds in tcols:
            r = _dot_nt(wt_ref[c0:c0 + wd, :], xb)
            for kind in kinds:
                if kind == "full":
                    o_refs[k][...] = r
                else:
                    rb = r.astype(BF16)
                    for c in range(rb.shape[1] // LANES):
                        o_refs[k][c] = rb[:, c * LANES:(c + 1) * LANES]
                k += 1


def _proj(x, wn, ncols, wt=None, tcols=(), tm=512):
    B, T, K = x.shape
    tm = min(tm, T)
    has_t = wt is not None
    in_specs = [pl.BlockSpec((None, tm, K), lambda b, t: (b, t, 0)),
                pl.BlockSpec(wn.shape, lambda b, t: (0, 0))]
    args = [x, wn]
    if has_t:
        in_specs.append(pl.BlockSpec(wt.shape, lambda b, t: (0, 0)))
        args.append(wt)
    out_specs, out_shape = [], []
    for _, wd in ncols:
        out_specs.append(pl.BlockSpec((None, tm, wd), lambda b, t: (b, t, 0)))
        out_shape.append(jax.ShapeDtypeStruct((B, T, wd), F32))
    for _, wd, kinds in tcols:
        for kind in kinds:
            if kind == "full":
                out_specs.append(pl.BlockSpec((None, wd, tm), lambda b, t: (b, 0, t)))
                out_shape.append(jax.ShapeDtypeStruct((B, wd, T), F32))
            else:
                nb = tm // LANES
                out_specs.append(pl.BlockSpec((None, nb, wd, LANES), lambda b, t: (b, t, 0, 0)))
                out_shape.append(jax.ShapeDtypeStruct((B, T // LANES, wd, LANES), BF16))
    return pl.pallas_call(
        functools.partial(_proj_body, ncols=tuple(ncols), tcols=tuple(tcols), has_t=has_t),
        grid=(B, T // tm), in_specs=in_specs, out_specs=out_specs, out_shape=out_shape,
        compiler_params=_cparams(("parallel", "parallel"), big=True), name="proj")(*args)


def _compress_body(xk_ref, xv_ref, wk_ref, wv_ref, pe_ref, ok_ref, ov_ref):
    n = xk_ref.shape[0]
    colid = lax.broadcasted_iota(jnp.int32, (KV_W, n), 1)
    for x_ref, w_ref, o_ref in ((xk_ref, wk_ref, ok_ref), (xv_ref, wv_ref, ov_ref)):
        x = x_ref[...]
        a = _dot_nt(w_ref[0], (x + pe_ref[0]).astype(BF16))
        b = _dot_nt(w_ref[1], (x + pe_ref[1]).astype(BF16))
        o_ref[...] = jnp.where(colid < n - 1, a + pltpu.roll(b, n - 1, 1), 0.0)


def _compress(xk, xv, wk, wv, pe):
    B, n, cw = xk.shape
    xspec = pl.BlockSpec((None, n, cw), lambda b: (b, 0, 0))
    wspec = pl.BlockSpec(wk.shape, lambda b: (0, 0, 0))
    ospec = pl.BlockSpec((None, KV_W, n), lambda b: (b, 0, 0))
    return pl.pallas_call(
        _compress_body, grid=(B,),
        in_specs=[xspec, xspec, wspec, wspec, pl.BlockSpec(pe.shape, lambda b: (0, 0, 0))],
        out_specs=[ospec, ospec], out_shape=[jax.ShapeDtypeStruct((B, KV_W, n), F32)] * 2,
        compiler_params=_cparams(("parallel",), big=True), name="compress")(xk, xv, wk, wv, pe)


def _compress_paged_body(pt_ref, *refs, P):
    x_refs = refs[:P]
    w_ref, pe_ref, a_ref, b_ref, tok = refs[P:]
    page = x_refs[0].shape[1]
    for i in range(P):
        tok[i * page:(i + 1) * page, :] = x_refs[i][...].T
    nch = P * page // CMP_STRIDE
    x = jnp.concatenate([tok[pl.ds(c, nch, stride=CMP_STRIDE), :] for c in range(CMP_STRIDE)], axis=1)
    a_ref[...] = _dot((x + pe_ref[0]).astype(BF16), w_ref[0])
    b_ref[...] = _dot((x + pe_ref[1]).astype(BF16), w_ref[1])


def _compress_paged(cache, layer, page_table, w, pe, P):
    Bd, NP = page_table.shape
    page = cache.shape[-1]
    nch = P * page // CMP_STRIDE
    in_specs = [pl.BlockSpec((None, None, KV_W, page), lambda b, s, pt, i=i: (layer, pt[b, s * P + i], 0, 0))
                for i in range(P)]
    in_specs += [pl.BlockSpec(w.shape, lambda b, s, pt: (0, 0, 0)), pl.BlockSpec(pe.shape, lambda b, s, pt: (0, 0, 0))]
    ospec = pl.BlockSpec((None, nch, KV_W), lambda b, s, pt: (b, s, 0))
    grid_spec = pltpu.PrefetchScalarGridSpec(num_scalar_prefetch=1, grid=(Bd, NP // P), in_specs=in_specs,
                                             out_specs=[ospec, ospec],
                                             scratch_shapes=[pltpu.VMEM((P * page, KV_W), F32)])
    return pl.pallas_call(
        functools.partial(_compress_paged_body, P=P), grid_spec=grid_spec,
        out_shape=[jax.ShapeDtypeStruct((Bd, NP * page // CMP_STRIDE, KV_W), F32)] * 2,
        compiler_params=_cparams(("parallel", "parallel"), big=True), name="compress_paged")(
            page_table, *([cache] * P), w, pe)


def _nsa_prompt_body(q_ref, ng_ref, ckt_ref, cvt_ref, sk_ref, sv_ref, wk_ref, wv_ref, ovl_ref, o_ref,
                     *, Q, KBN, n_slc):
    i = pl.program_id(1)
    q0 = i * Q
    KB = KBN * LANES
    QB = Q // LANES
    lane = lax.broadcasted_iota(jnp.int32, (Q, LANES), 1)
    lo_half = lane < HEAD_DIM
    qpos = q0 + lax.broadcasted_iota(jnp.int32, (Q, 1), 0)
    gate = jax.nn.sigmoid(ng_ref[...])
    slabs = [q_ref[:, LANES * s:LANES * (s + 1)] * (HEAD_DIM ** -0.5) for s in range(4)]

    def placed(h, g):
        sl = slabs[h // 2]
        if (h % 2) != g:
            sl = pltpu.roll(sl, HEAD_DIM, 1)
        return jnp.where(lo_half if g == 0 else jnp.logical_not(lo_half), sl, 0.0)

    ckt = ckt_ref[...].astype(BF16)
    cvt = cvt_ref[...].astype(BF16)
    NC = ckt.shape[1]
    cend = CMP_STRIDE * lax.broadcasted_iota(jnp.int32, (1, NC), 1) + (CMP_BLOCK - 1)
    dist_c = qpos - cend
    vis_c = dist_c >= 0
    dist_cf = dist_c.astype(F32)
    blk_t = lax.broadcasted_iota(jnp.int32, (n_slc, NSA_KV * Q), 0)
    col_t = lax.broadcasted_iota(jnp.int32, (n_slc, NSA_KV * Q), 1)
    qpos_t = q0 + (col_t & (Q - 1))
    cur_t = qpos_t >> 6
    valid_t = blk_t * SLC_BLOCK <= qpos_t
    forced_t = (blk_t == 0) | (blk_t == cur_t) | (blk_t == cur_t - 1)
    rows_of = [slice(h * Q, (h + 1) * Q) for h in range(NSA_HEADS)]
    group_of = [h // NSA_GROUP for h in range(NSA_HEADS)]
    qall = jnp.concatenate([placed(h, group_of[h]) for h in range(NSA_HEADS)], axis=0).astype(BF16)

    def softmax_rows(sd, distf, mask):
        ps = []
        for h in range(NSA_HEADS):
            s = jnp.where(mask, sd[rows_of[h]] - SLOPES[h] * distf, NEG)
            e = jnp.where(mask, jnp.exp(s - jnp.max(s, axis=-1, keepdims=True)), 0.0)
            ps.append(e / jnp.maximum(jnp.sum(e, axis=-1, keepdims=True), TINY))
        return ps

    sb0 = jnp.maximum(i * QB - WINDOW // LANES, 0)
    nwb = WINDOW // LANES + QB
    kt_w = jnp.concatenate([wk_ref[sb0 + c] for c in range(nwb)], axis=1)
    vt_w = jnp.concatenate([wv_ref[sb0 + c] for c in range(nwb)], axis=1)
    dist_w = qpos - (sb0 * LANES + lax.broadcasted_iota(jnp.int32, (1, nwb * LANES), 1))
    sd_c = _dot(qall, ckt)
    sd_w = _dot(qall, kt_w)
    ps = softmax_rows(sd_c, dist_cf, vis_c)
    pw = softmax_rows(sd_w, dist_w.astype(F32), (dist_w >= 0) & (dist_w <= WINDOW))
    o_cmp = _dot_nt(jnp.concatenate(ps, axis=0).astype(BF16), cvt)
    o_win = _dot_nt(jnp.concatenate(pw, axis=0).astype(BF16), vt_w)
    psum = jnp.concatenate([(ps[NSA_GROUP * g] + ps[NSA_GROUP * g + 1]) + (ps[NSA_GROUP * g + 2] + ps[NSA_GROUP * g + 3])
                            for g in range(NSA_KV)], axis=0)
    ph, pl_ = _split2(psum)
    imp = _dot(ph, ovl_ref[...]) + _dot(pl_, ovl_ref[...])
    score_t = jnp.where(valid_t, jnp.where(forced_t, FORCE_SCORE, imp.T[:n_slc]), -1.0)
    sel_t = _topk_rank(score_t, SLC_TOP)
    if n_slc < LANES:
        sel_t = jnp.concatenate([sel_t, jnp.zeros((LANES - n_slc, NSA_KV * Q), F32)], axis=0)
    sel_b = sel_t.T.astype(BF16)
    blk_row = lax.broadcasted_iota(jnp.int32, (LANES, KB), 0)

    half_rows = lax.broadcasted_iota(jnp.int32, (LANES, KB), 0) < HEAD_DIM

    def kt_body(t, carry):
        m, acc = carry
        kt = jnp.concatenate([sk_ref[t * KBN + c] for c in range(KBN)], axis=1)
        vt = jnp.concatenate([sv_ref[t * KBN + c] for c in range(KBN)], axis=1)
        one = jnp.ones_like(vt)
        vts = [jnp.where(half_rows, vt, one), jnp.where(half_rows, one, vt)]
        kpos = t * KB + lax.broadcasted_iota(jnp.int32, (1, KB), 1)
        expand = ((kpos >> 6) == blk_row).astype(BF16)
        sd = _dot(qall, kt)
        selx = _dot(sel_b, expand)
        dist = qpos - kpos
        causal = dist >= 0
        masks = [(selx[g * Q:(g + 1) * Q] > 0.5) & causal for g in range(NSA_KV)]
        distf = dist.astype(F32)
        es, alphas, ms = [], [], []
        for h in range(NSA_HEADS):
            mask = masks[group_of[h]]
            s = jnp.where(mask, sd[rows_of[h]] - SLOPES[h] * distf, NEG)
            m_old = m[rows_of[h]]
            m_new = jnp.maximum(m_old, jnp.max(s, axis=-1, keepdims=True))
            alphas.append(jnp.exp(m_old - m_new))
            ms.append(m_new)
            es.append(jnp.where(mask, jnp.exp(s - m_new), 0.0).astype(BF16))
        pv = jnp.concatenate([_dot_nt(jnp.concatenate(es[NSA_GROUP * g:NSA_GROUP * (g + 1)], axis=0), vts[g])
                              for g in range(NSA_KV)], axis=0)
        return jnp.concatenate(ms, axis=0), jnp.concatenate(alphas, axis=0) * acc + pv

    nkt = (q0 + Q + KB - 1) // KB
    init = (jnp.full((NSA_HEADS * Q, 1), NEG, F32), jnp.zeros((NSA_HEADS * Q, LANES), F32))
    _, acc_s = lax.fori_loop(0, nkt, kt_body, init)
    o_slc = acc_s / jnp.maximum(pltpu.roll(acc_s, HEAD_DIM, 1), TINY)
    outs = [None] * NSA_HEADS
    for h in range(NSA_HEADS):
        rows = rows_of[h]
        oh = (gate[:, 3 * h:3 * h + 1] * o_cmp[rows] + gate[:, 3 * h + 1:3 * h + 2] * o_slc[rows]
              + gate[:, 3 * h + 2:3 * h + 3] * o_win[rows])
        if (h % 2) != group_of[h]:
            oh = pltpu.roll(oh, HEAD_DIM, 1)
        outs[h] = oh
    for s in range(4):
        o_ref[:, LANES * s:LANES * (s + 1)] = jnp.where(lo_half, outs[2 * s], outs[2 * s + 1])


def _nsa_prompt(q, ng, ckt, cvt, skb, svb, wkb, wvb, ovl, Q=128, KBN=4):
    B, T, _ = q.shape
    n_slc = T // SLC_BLOCK
    NC = ckt.shape[2]
    nblk = T // LANES
    kvspec = pl.BlockSpec((None, nblk, KV_W, LANES), lambda b, i: (b, 0, 0, 0))
    cspec = pl.BlockSpec((None, KV_W, NC), lambda b, i: (b, 0, 0))
    return pl.pallas_call(
        functools.partial(_nsa_prompt_body, Q=Q, KBN=KBN, n_slc=n_slc), grid=(B, T // Q),
        in_specs=[pl.BlockSpec((None, Q, NSA_HEADS * HEAD_DIM), lambda b, i: (b, i, 0)),
                  pl.BlockSpec((None, Q, LANES), lambda b, i: (b, i, 0)),
                  cspec, cspec, kvspec, kvspec, kvspec, kvspec,
                  pl.BlockSpec(ovl.shape, lambda b, i: (0, 0))],
        out_specs=pl.BlockSpec((None, Q, NSA_HEADS * HEAD_DIM), lambda b, i: (b, i, 0)),
        out_shape=jax.ShapeDtypeStruct((B, T, NSA_HEADS * HEAD_DIM), F32),
        compiler_params=_cparams(("parallel", "parallel"), big=True), name="nsa_prompt")(
            q, ng, ckt, cvt, skb, svb, wkb, wvb, ovl)


def _sb_block(qh, kt, vt, u, carry, acc, mask):
    n = qh.shape[0]
    s = _dot(qh, kt)
    sp = _softplus(s)
    l1m = -sp if mask is None else jnp.where(mask, -sp, 0.0)
    hi, lo = _split2(l1m)
    r2 = _dot(jnp.concatenate([hi, lo], axis=0), u)
    after = r2[:n] + r2[n:]
    w = jnp.exp((s - sp) + after + carry)
    if mask is not None:
        w = jnp.where(mask, w, 0.0)
    acc = acc + _dot_nt(w.astype(BF16), vt)
    carry = carry + jnp.sum(l1m, axis=-1, keepdims=True)
    return carry, acc


def _sb_pairs_step(qs, kts, vts, u2_ref, row_lo, carries, accs, mask2):
    n = qs[0].shape[0]
    zero = jnp.zeros_like(kts[0])
    bd = lambda t: jnp.concatenate([jnp.where(row_lo, t, zero), jnp.where(row_lo, zero, t)], axis=1)
    ss = [_dot(q, bd(kt)) for q, kt in zip(qs, kts)]
    sps, cats = [], []
    for s in ss:
        sp = jnp.maximum(s, 0.0) + jnp.log(1.0 + jnp.exp(-jnp.abs(s)))
        l1m = -sp if mask2 is None else jnp.where(mask2, -sp, 0.0)
        hi, lo = _split2(l1m)
        sps.append(sp)
        cats.append(jnp.concatenate([hi, lo], axis=0))
    rs = [_dot(c, u2_ref[...]) for c in cats]
    ws, new_c = [], []
    for s, sp, r, carry2 in zip(ss, sps, rs, carries):
        r = r[:n] + r[n:]
        kb2 = s.shape[1]
        w = jnp.exp((s - sp) + r[:, :kb2] + carry2)
        if mask2 is not None:
            w = jnp.where(mask2, w, 0.0)
        ws.append(w.astype(BF16))
        new_c.append(carry2 + r[:, kb2:])
    return new_c, [acc + _dot_nt(w, bd(vt)) for acc, w, vt in zip(accs, ws, vts)]


def _sb_prompt_body(q_ref, kt_ref, vt_ref, u2_ref, o_ref, carry_s, acc_s, *, Q):
    i = pl.program_id(1)
    npair = SB_W // LANES
    tri = lax.broadcasted_iota(jnp.int32, (Q, Q), 1) < lax.broadcasted_iota(jnp.int32, (Q, Q), 0)
    mask2 = jnp.concatenate([tri, tri], axis=1)
    row_lo = lax.broadcasted_iota(jnp.int32, (LANES, Q), 0) < HEAD_DIM
    slab = [slice(LANES * p, LANES * (p + 1)) for p in range(npair)]

    def step(j, carries, accs, mask):
        qs = [(q_ref[:, slab[p]] * (HEAD_DIM ** -0.5)).astype(BF16) for p in range(npair)]
        new_c, new_a = _sb_pairs_step(qs, [kt_ref[j, slab[p], :] for p in range(npair)],
                                      [vt_ref[j, slab[p], :] for p in range(npair)], u2_ref, row_lo,
                                      carries, accs, mask)
        for p in range(npair):
            carry_s[p] = new_c[p]
            acc_s[p] = new_a[p]

    step(i, [jnp.zeros((Q, 2 * Q), F32)] * npair, [jnp.zeros((Q, LANES), F32)] * npair, mask2)

    def alive():
        mx = carry_s[0]
        for p in range(1, npair):
            mx = jnp.maximum(mx, carry_s[p])
        return (jnp.max(mx) > SB_DEAD_LOG).astype(jnp.int32)

    def body(st):
        jj, _ = st
        step(i - 1 - jj, [carry_s[p] for p in range(npair)], [acc_s[p] for p in range(npair)], None)
        return jj + 1, alive()

    lax.while_loop(lambda st: (st[0] < i) & (st[1] > 0), body, (jnp.int32(0), alive()))
    for p in range(npair):
        o_ref[:, slab[p]] = acc_s[p]


def _sb_prompt(q, ktb, vtb, u2):
    B, T, _ = q.shape
    Q = LANES
    nblk = T // LANES
    kvspec = pl.BlockSpec((None, nblk, SB_W, LANES), lambda b, i: (b, 0, 0, 0))
    return pl.pallas_call(
        functools.partial(_sb_prompt_body, Q=Q), grid=(B, T // Q),
        in_specs=[pl.BlockSpec((None, Q, SB_W), lambda b, i: (b, i, 0)), kvspec, kvspec,
                  pl.BlockSpec(u2.shape, lambda b, i: (0, 0))],
        out_specs=pl.BlockSpec((None, Q, SB_W), lambda b, i: (b, i, 0)),
        out_shape=jax.ShapeDtypeStruct((B, T, SB_W), F32),
        scratch_shapes=[pltpu.VMEM((SB_W // LANES, Q, 2 * Q), F32), pltpu.VMEM((SB_W // LANES, Q, LANES), F32)],
        compiler_params=_cparams(("parallel", "parallel"), big=True), name="sb_prompt")(q, ktb, vtb, u2)


def _mem_body(q_ref, k_ref, v_ref, o_ref):
    for h in range(MEM_HEADS):
        sl = slice(MEM_HEAD_DIM * h, MEM_HEAD_DIM * (h + 1))
        s = _dot_nt(q_ref[:, sl].astype(BF16), k_ref[:, sl].astype(BF16)) * (MEM_HEAD_DIM ** -0.5)
        e = jnp.exp(s - jnp.max(s, axis=-1, keepdims=True))
        p = e / jnp.sum(e, axis=-1, keepdims=True)
        o_ref[:, sl] = _dot(p.astype(BF16), v_ref[:, sl].astype(BF16))


def _mem_attn(q, mk, mv, tq=512):
    B, T, W = q.shape
    tq = min(tq, T)
    N = mk.shape[1]
    kvspec = pl.BlockSpec((None, N, W), lambda b, i: (b, 0, 0))
    return pl.pallas_call(
        _mem_body, grid=(B, T // tq),
        in_specs=[pl.BlockSpec((None, tq, W), lambda b, i: (b, i, 0)), kvspec, kvspec],
        out_specs=pl.BlockSpec((None, tq, W), lambda b, i: (b, i, 0)),
        out_shape=jax.ShapeDtypeStruct((B, T, W), F32),
        compiler_params=_cparams(("parallel", "parallel")), name="mem_attn")(q, mk, mv)


def _rglru_body(x_ref, g_ref, cb_ref, h0_ref, cw_ref, cbias_ref, wbd_ref, gb_ref, lam_ref,
                y_ref, hn_ref, cn_ref, xbuf, a_s, b_s, h_s, hc, *, n_t):
    t = pl.program_id(1)
    tt = x_ref.shape[0]
    hist = CONV_W - 1

    @pl.when(t == 0)
    def _():
        xbuf[8 - hist:8, :] = cb_ref[...]
        hc[...] = h0_ref[...]

    xbuf[8:8 + tt, :] = x_ref[...]
    xc = cw_ref[0:1, :] * xbuf[8 - hist:8 - hist + tt, :]
    for k in range(1, CONV_W):
        xc = xc + cw_ref[k:k + 1, :] * xbuf[8 - hist + k:8 - hist + k + tt, :]
    xc = xc + cbias_ref[...]
    gates = _dot(xc.astype(BF16), wbd_ref[...]) + gb_ref[...]
    r = jax.nn.sigmoid(gates[:, :D_RNN])
    ig = jax.nn.sigmoid(gates[:, D_RNN:])
    log_a = (-LRU_C * _softplus(-lam_ref[...])) * r
    a = jnp.exp(log_a)
    a_s[...] = a
    b_s[...] = jnp.sqrt(-jnp.tanh(log_a) * (a * a + 1.0)) * (ig * xc)

    def step(k, h):
        h = a_s[pl.ds(k, 1), :] * h + b_s[pl.ds(k, 1), :]
        h_s[pl.ds(k, 1), :] = h
        return h

    h = lax.fori_loop(0, tt, step, hc[...], unroll=8 if tt % 8 == 0 else tt)
    hc[...] = h
    y_ref[...] = h_s[...] * jax.nn.gelu(g_ref[...])
    hn_ref[...] = h
    cn_ref[...] = xbuf[8 + tt - hist:8 + tt, :]
    if n_t > 1:
        xbuf[0:8, :] = xbuf[tt:tt + 8, :]


def _rglru(x, gate_in, conv_buf, h0, conv_w, conv_b, wbd, gbias, lam, tt=256):
    B, T, W = x.shape
    tt = min(tt, T)
    n_t = T // tt
    xspec = pl.BlockSpec((None, tt, W), lambda b, t: (b, t, 0))
    const = lambda a: pl.BlockSpec(a.shape, lambda b, t: (0,) * a.ndim)
    return pl.pallas_call(
        functools.partial(_rglru_body, n_t=n_t), grid=(B, n_t),
        in_specs=[xspec, xspec, pl.BlockSpec((None, CONV_W - 1, W), lambda b, t: (b, 0, 0)),
                  pl.BlockSpec((None, 1, W), lambda b, t: (b, 0, 0)),
                  const(conv_w), const(conv_b), const(wbd), const(gbias), const(lam)],
        out_specs=[xspec, pl.BlockSpec((None, 1, W), lambda b, t: (b, 0, 0)),
                   pl.BlockSpec((None, CONV_W - 1, W), lambda b, t: (b, 0, 0))],
        out_shape=[jax.ShapeDtypeStruct((B, T, W), F32), jax.ShapeDtypeStruct((B, 1, W), F32),
                   jax.ShapeDtypeStruct((B, CONV_W - 1, W), F32)],
        scratch_shapes=[pltpu.VMEM((tt + 8, W), F32), pltpu.VMEM((tt, W), F32), pltpu.VMEM((tt, W), F32),
                        pltpu.VMEM((tt, W), F32), pltpu.VMEM((1, W), F32)],
        compiler_params=_cparams(("parallel", "arbitrary")), name="rglru")(
            x, gate_in, conv_buf, h0, conv_w, conv_b, wbd, gbias, lam)


def _merge_body(x_ref, b0, b1, b2, b3, wmg_ref, wbr_ref, wo_ref, g_ref, be_ref, o_ref, *, alpha):
    x = x_ref[...]
    xb = x.astype(BF16)
    acc = None
    off = 0
    for n, br in enumerate((b0, b1, b2, b3)):
        wdt = br.shape[-1]
        gate = jax.nn.sigmoid(_dot(xb, wmg_ref[:, n * D_MODEL:(n + 1) * D_MODEL]))
        term = gate * _dot(br[...].astype(BF16), wbr_ref[off:off + wdt, :])
        acc = term if acc is None else acc + term
        off += wdt
    mix = _dot(acc.astype(BF16), wo_ref[...])
    o_ref[...] = _layer_norm(alpha * x + mix, g_ref[...], be_ref[...])


def _merge_ln(x, branches, wmg, wbr, wo, g, be, alpha, tm=256):
    M = x.shape[0]
    tm = min(tm, M)
    row = lambda w: pl.BlockSpec((tm, w), lambda i: (i, 0))
    const = lambda a: pl.BlockSpec(a.shape, lambda i: (0, 0))
    return pl.pallas_call(
        functools.partial(_merge_body, alpha=alpha), grid=(M // tm,),
        in_specs=[row(D_MODEL)] + [row(b.shape[1]) for b in branches] + [const(wmg), const(wbr), const(wo),
                                                                         const(g), const(be)],
        out_specs=row(D_MODEL), out_shape=jax.ShapeDtypeStruct((M, D_MODEL), F32),
        compiler_params=_cparams(("parallel",), big=True), name="merge_ln")(x, *branches, wmg, wbr, wo, g, be)


def _ffn_body(x_ref, wa_ref, wb_ref, w2_ref, g_ref, be_ref, o_ref, acc, *, alpha, nf):
    j = pl.program_id(1)
    x = x_ref[...]
    xb = x.astype(BF16)
    a = _dot(xb, wa_ref[...])
    b = _dot(xb, wb_ref[...])
    part = _dot(((a * jax.nn.sigmoid(a)) * b).astype(BF16), w2_ref[...])

    @pl.when(j == 0)
    def _():
        acc[...] = part

    @pl.when(j > 0)
    def _():
        acc[...] += part

    @pl.when(j == nf - 1)
    def _():
        o_ref[...] = _layer_norm(alpha * x + acc[...], g_ref[...], be_ref[...])


def _ffn_ln(x, w13, w2, g, be, alpha, tm=512, tf=1408):
    M = x.shape[0]
    tm = min(tm, M)
    F = w2.shape[0]
    nf = F // tf
    const = lambda a: pl.BlockSpec(a.shape, lambda i, j: (0, 0))
    return pl.pallas_call(
        functools.partial(_ffn_body, alpha=alpha, nf=nf), grid=(M // tm, nf),
        in_specs=[pl.BlockSpec((tm, D_MODEL), lambda i, j: (i, 0)),
                  pl.BlockSpec((D_MODEL, tf), lambda i, j: (0, j)),
                  pl.BlockSpec((D_MODEL, tf), lambda i, j: (0, j + nf)),
                  pl.BlockSpec((tf, D_MODEL), lambda i, j: (j, 0)), const(g), const(be)],
        out_specs=pl.BlockSpec((tm, D_MODEL), lambda i, j: (i, 0)),
        out_shape=jax.ShapeDtypeStruct((M, D_MODEL), F32),
        scratch_shapes=[pltpu.VMEM((tm, D_MODEL), F32)],
        compiler_params=_cparams(("parallel", "arbitrary"), big=True), name="ffn_ln")(x, w13, w13, w2, g, be)


def _moe_body(x_ref, rwh_ref, rwl_ref, rb_ref, wa_ref, wb_ref, w2_ref, g_ref, be_ref, o_ref, acc, comb,
              *, alpha, n_exp):
    e = pl.program_id(1)
    x = x_ref[...]
    xh = x.astype(BF16)
    tm = x.shape[0]
    lane = lax.broadcasted_iota(jnp.int32, (tm, LANES), 1)

    @pl.when(e == 0)
    def _():
        xl = (x - xh.astype(F32)).astype(BF16)
        logits = (_dot(xh, rwh_ref[...]) + _dot(xh, rwl_ref[...])) + _dot(xl, rwh_ref[...]) + rb_ref[...]
        valid = lane < n_exp
        lanef = lane.astype(F32)
        logits = jnp.where(valid, logits, NEG)
        ex = jnp.where(valid, jnp.exp(logits - jnp.max(logits, axis=-1, keepdims=True)), 0.0)
        p = ex / jnp.sum(ex, axis=-1, keepdims=True)
        p = jnp.where(valid, p, -1.0)
        m1 = jnp.max(p, axis=-1, keepdims=True)
        hit1 = lanef == jnp.min(jnp.where(p == m1, lanef, 1.0e9), axis=-1, keepdims=True)
        p2 = jnp.where(hit1, -1.0, p)
        m2 = jnp.max(p2, axis=-1, keepdims=True)
        hit2 = lanef == jnp.min(jnp.where(p2 == m2, lanef, 1.0e9), axis=-1, keepdims=True)
        den = m1 + m2
        comb[...] = jnp.where(hit1, m1 / den, 0.0) + jnp.where(hit2, m2 / den, 0.0)
        acc[...] = jnp.zeros_like(acc)

    w_e = jnp.sum(jnp.where(lane == e, comb[...], 0.0), axis=-1, keepdims=True)
    a = _dot(xh, wa_ref[...])
    b = _dot(xh, wb_ref[...])
    acc[...] += w_e * _dot(((a * jax.nn.sigmoid(a)) * b).astype(BF16), w2_ref[...])

    @pl.when(e == n_exp - 1)
    def _():
        o_ref[...] = _layer_norm(alpha * x + acc[...], g_ref[...], be_ref[...])


def _moe_ln(x, rwh, rwl, rb, w13, w2, g, be, alpha, tm=512):
    M = x.shape[0]
    tm = min(tm, M)
    n_exp, de, _ = w2.shape
    const = lambda a: pl.BlockSpec(a.shape, lambda i, e: (0, 0))
    return pl.pallas_call(
        functools.partial(_moe_body, alpha=alpha, n_exp=n_exp), grid=(M // tm, n_exp),
        in_specs=[pl.BlockSpec((tm, D_MODEL), lambda i, e: (i, 0)), const(rwh), const(rwl), const(rb),
                  pl.BlockSpec((None, D_MODEL, de), lambda i, e: (e, 0, 0)),
                  pl.BlockSpec((None, D_MODEL, de), lambda i, e: (e, 0, 1)),
                  pl.BlockSpec((None, de, D_MODEL), lambda i, e: (e, 0, 0)), const(g), const(be)],
        out_specs=pl.BlockSpec((tm, D_MODEL), lambda i, e: (i, 0)),
        out_shape=jax.ShapeDtypeStruct((M, D_MODEL), F32),
        scratch_shapes=[pltpu.VMEM((tm, D_MODEL), F32), pltpu.VMEM((tm, LANES), F32)],
        compiler_params=_cparams(("parallel", "arbitrary"), big=True), name="moe_ln")(
            x, rwh, rwl, rb, w13, w13, w2, g, be)


def _row_consts(rows, Td, past):
    row = lax.broadcasted_iota(jnp.int32, (rows, 1), 0)
    head = row // Td
    qpos = past + (row - head * Td)
    slope = jnp.zeros((rows, 1), F32)
    for h in range(NSA_HEADS):
        slope = jnp.where(head == h, SLOPES[h], slope)
    return qpos, slope


def _nsa_s_cmp_body(qz_ref, ak_ref, bk_ref, av_ref, bv_ref, ovl_ref, hs_ref, ocmp_ref, sel_ref, *, Td, past, n_slc):
    rows = qz_ref.shape[0]
    NC = ak_ref.shape[0]
    qpos, slope = _row_consts(rows, Td, past)
    ck = (ak_ref[...] + pltpu.roll(bk_ref[...], NC - 1, 0)).astype(BF16)
    cv = (av_ref[...] + pltpu.roll(bv_ref[...], NC - 1, 0)).astype(BF16)
    qz = (qz_ref[...] * (HEAD_DIM ** -0.5)).astype(BF16)
    cend = CMP_STRIDE * lax.broadcasted_iota(jnp.int32, (1, NC), 1) + (CMP_BLOCK - 1)
    dist = qpos - cend
    p = _masked_softmax(_dot_nt(qz, ck) - slope * dist.astype(F32), dist >= 0)
    ocmp_ref[...] = _dot(p.astype(BF16), cv)
    ph, pl_ = _split2(p)
    imp_h = _dot(ph, ovl_ref[...]) + _dot(pl_, ovl_ref[...])
    i1, i2, i3 = _split3(imp_h)
    hs = hs_ref[...]
    imp = (_dot(hs, i1) + _dot(hs, i2)) + _dot(hs, i3)
    NSP = imp.shape[1]
    lane = lax.broadcasted_iota(jnp.int32, (rows, NSP), 1)
    cur = qpos >> 6
    valid = (lane * SLC_BLOCK <= qpos) & (lane < n_slc)
    forced = (lane == 0) | (lane == cur) | (lane == cur - 1)
    score = jnp.where(valid, jnp.where(forced, FORCE_SCORE, imp), jnp.where(lane < n_slc, -1.0, -2.0))
    sel_ref[...] = _topk_mask(score, lane.astype(F32), SLC_TOP)


def _nsa_s_cmp(qz, ak, bk, av, bv, ovl, hs, Td, past, n_slc):
    Bd, rows, _ = qz.shape
    NC = ak.shape[1]
    NSP = ovl.shape[1]
    cspec = pl.BlockSpec((None, NC, KV_W), lambda b: (b, 0, 0))
    return pl.pallas_call(
        functools.partial(_nsa_s_cmp_body, Td=Td, past=past, n_slc=n_slc), grid=(Bd,),
        in_specs=[pl.BlockSpec((None, rows, KV_W), lambda b: (b, 0, 0)), cspec, cspec, cspec, cspec,
                  pl.BlockSpec(ovl.shape, lambda b: (0, 0)), pl.BlockSpec(hs.shape, lambda b: (0, 0))],
        out_specs=[pl.BlockSpec((None, rows, KV_W), lambda b: (b, 0, 0)),
                   pl.BlockSpec((None, rows, NSP), lambda b: (b, 0, 0))],
        out_shape=[jax.ShapeDtypeStruct((Bd, rows, KV_W), F32), jax.ShapeDtypeStruct((Bd, rows, NSP), F32)],
        compiler_params=_cparams(("parallel",), big=True), name="nsa_sample_cmp")(qz, ak, bk, av, bv, ovl, hs)


def _nsa_s_slc_body(pt_ref, *refs, P, Td, past, n_steps):
    kp = refs[:P]
    vp = refs[P:2 * P]
    (qz_ref, g_ref, ocmp_ref, sel_ref, nk_ref, nv_ref, wk_ref, wv_ref, o_ref, m_s, l_s, acc_s) = refs[2 * P:]
    s_id = pl.program_id(1)
    rows = qz_ref.shape[0]
    NSP = sel_ref.shape[1]
    qpos, slope = _row_consts(rows, Td, past)
    qz = (qz_ref[...] * (HEAD_DIM ** -0.5)).astype(BF16)
    sel_b = sel_ref[...].astype(BF16)

    @pl.when(s_id == 0)
    def _():
        m_s[...] = jnp.full_like(m_s, NEG)
        l_s[...] = jnp.zeros_like(l_s)
        acc_s[...] = jnp.zeros_like(acc_s)

    def tile(kt, vt, kpos, extra):
        nk = kt.shape[1]
        expand = ((kpos >> 6) == lax.broadcasted_iota(jnp.int32, (NSP, nk), 0)).astype(BF16)
        dist = qpos - kpos
        mask = (_dot(sel_b, expand) > 0.5) & (dist >= 0)
        if extra is not None:
            mask = mask & extra
        s = jnp.where(mask, _dot(qz, kt) - slope * dist.astype(F32), NEG)
        m_old = m_s[...]
        m_new = jnp.maximum(m_old, jnp.max(s, axis=-1, keepdims=True))
        alpha = jnp.exp(m_old - m_new)
        e = jnp.where(mask, jnp.exp(s - m_new), 0.0)
        l_s[...] = alpha * l_s[...] + jnp.sum(e, axis=-1, keepdims=True)
        acc_s[...] = alpha * acc_s[...] + _dot_nt(e.astype(BF16), vt)
        m_s[...] = m_new

    kt = jnp.concatenate([r[...] for r in kp], axis=1).astype(BF16)
    vt = jnp.concatenate([r[...] for r in vp], axis=1).astype(BF16)
    tile(kt, vt, s_id * (P * LANES) + lax.broadcasted_iota(jnp.int32, (1, P * LANES), 1), None)

    @pl.when(s_id == n_steps - 1)
    def _():
        nl = lax.broadcasted_iota(jnp.int32, (1, LANES), 1)
        tile(nk_ref[...].astype(BF16), nv_ref[...].astype(BF16), past + nl, nl < Td)
        o_slc = acc_s[...] / jnp.maximum(l_s[...], TINY)
        nw = wk_ref.shape[1]
        wl = lax.broadcasted_iota(jnp.int32, (1, nw), 1)
        dist = qpos - (past - (nw - LANES) + wl)
        okw = (wl < nw - LANES + Td) & (dist >= 0) & (dist <= WINDOW)
        pw = _masked_softmax(_dot(qz, wk_ref[...].astype(BF16)) - slope * dist.astype(F32), okw)
        o_win = _dot_nt(pw.astype(BF16), wv_ref[...].astype(BF16))
        gate = jax.nn.sigmoid(g_ref[...])
        o_ref[...] = gate[:, 0:1] * ocmp_ref[...] + gate[:, 1:2] * o_slc + gate[:, 2:3] * o_win


def _nsa_s_slc(page_table, layer, skc, svc, qz, graw, ocmp, sel, nks, nvs, wkx, wvx, Td, past, P):
    Bd, NP = page_table.shape
    rows = qz.shape[1]
    n_steps = NP // P
    pspec = [pl.BlockSpec((None, None, KV_W, LANES), lambda b, s, pt, i=i: (layer, pt[b, s * P + i], 0, 0))
             for i in range(P)]
    per_b = lambda a: pl.BlockSpec((None,) + a.shape[1:], lambda b, s, pt: (b, 0, 0))
    grid_spec = pltpu.PrefetchScalarGridSpec(
        num_scalar_prefetch=1, grid=(Bd, n_steps),
        in_specs=pspec + pspec + [per_b(a) for a in (qz, graw, ocmp, sel, nks, nvs, wkx, wvx)],
        out_specs=pl.BlockSpec((None, rows, KV_W), lambda b, s, pt: (b, 0, 0)),
        scratch_shapes=[pltpu.VMEM((rows, 1), F32), pltpu.VMEM((rows, 1), F32), pltpu.VMEM((rows, KV_W), F32)])
    return pl.pallas_call(
        functools.partial(_nsa_s_slc_body, P=P, Td=Td, past=past, n_steps=n_steps), grid_spec=grid_spec,
        out_shape=jax.ShapeDtypeStruct((Bd, rows, KV_W), F32),
        compiler_params=_cparams(("parallel", "arbitrary"), big=True), name="nsa_sample_slc")(
            page_table, *([skc] * P), *([svc] * P), qz, graw, ocmp, sel, nks, nvs, wkx, wvx)


def _sb_s_body(pt_ref, *refs, P, Td, n_steps):
    kp = refs[:P]
    vp = refs[P:2 * P]
    q_ref, nk_ref, nv_ref, u_ref, o_ref, carry_s, acc_s, alive_s = refs[2 * P:]
    s_id = pl.program_id(1)
    rows = q_ref.shape[0]
    qb = (q_ref[...] * (HEAD_DIM ** -0.5)).astype(BF16)
    u = u_ref[...]

    @pl.when(s_id == 0)
    def _():
        row = lax.broadcasted_iota(jnp.int32, (rows, LANES), 0)
        lane = lax.broadcasted_iota(jnp.int32, (rows, LANES), 1)
        newer = lane < (row - (row // Td) * Td)
        carry, acc = _sb_block(qb, nk_ref[...].astype(BF16), nv_ref[...].astype(BF16), u,
                               jnp.zeros((rows, 1), F32), jnp.zeros((rows, SB_W), F32), newer)
        carry_s[...] = carry
        acc_s[...] = acc
        alive_s[0] = (jnp.max(carry) > SB_DEAD_LOG).astype(jnp.int32)

    @pl.when(alive_s[0] > 0)
    def _():
        carry, acc = carry_s[...], acc_s[...]
        order = range(P - 1, -1, -1)
        ss = [_dot(qb, kp[i][...].astype(BF16)) for i in order]
        sps = [_softplus(s) for s in ss]
        cats = [jnp.concatenate(_split2(-sp), axis=0) for sp in sps]
        afters = [_dot(c, u) for c in cats]
        ws = []
        for s, sp, r in zip(ss, sps, afters):
            ws.append(jnp.exp((s - sp) + (r[:rows] + r[rows:]) + carry).astype(BF16))
            carry = carry - jnp.sum(sp, axis=-1, keepdims=True)
        for w, i in zip(ws, order):
            acc = acc + _dot_nt(w, vp[i][...].astype(BF16))
        carry_s[...] = carry
        acc_s[...] = acc
        alive_s[0] = (jnp.max(carry) > SB_DEAD_LOG).astype(jnp.int32)

    @pl.when(s_id == n_steps - 1)
    def _():
        o_ref[...] = acc_s[...]


def _sb_sample(page_table, layer, kc, vc, qbd, nk, nv, u, Td, P):
    Bd, NP = page_table.shape
    rows = qbd.shape[1]
    n_steps = NP // P
    pspec = [pl.BlockSpec((None, None, SB_W, LANES),
                          lambda b, s, pt, i=i: (layer, pt[b, NP - (s + 1) * P + i], 0, 0)) for i in range(P)]
    per_b = lambda a: pl.BlockSpec((None,) + a.shape[1:], lambda b, s, pt: (b, 0, 0))
    grid_spec = pltpu.PrefetchScalarGridSpec(
        num_scalar_prefetch=1, grid=(Bd, n_steps),
        in_specs=pspec + pspec + [per_b(qbd), per_b(nk), per_b(nv), pl.BlockSpec(u.shape, lambda b, s, pt: (0, 0))],
        out_specs=pl.BlockSpec((None, rows, SB_W), lambda b, s, pt: (b, 0, 0)),
        scratch_shapes=[pltpu.VMEM((rows, 1), F32), pltpu.VMEM((rows, SB_W), F32), pltpu.SMEM((1,), jnp.int32)])
    return pl.pallas_call(
        functools.partial(_sb_s_body, P=P, Td=Td, n_steps=n_steps), grid_spec=grid_spec,
        out_shape=jax.ShapeDtypeStruct((Bd, rows, SB_W), F32),
        compiler_params=_cparams(("parallel", "arbitrary"), big=True), name="sb_sample")(
            page_table, *([kc] * P), *([vc] * P), qbd, nk, nv, u)


def _cmp_weights(w, pe):
    ratio = CMP_BLOCK // CMP_STRIDE
    w5 = w.reshape(2, ratio, CMP_STRIDE, HEAD_DIM, HEAD_DIM)
    eye = jnp.eye(NSA_KV, dtype=w.dtype)
    wbd = jnp.einsum('krcde,gh->krcgdhe', w5, eye).reshape(2, ratio, CMP_STRIDE * KV_W, KV_W)
    pe2 = jnp.broadcast_to(pe.reshape(ratio, CMP_STRIDE, 1, HEAD_DIM), (ratio, CMP_STRIDE, NSA_KV, HEAD_DIM))
    return wbd.astype(BF16), jnp.swapaxes(wbd, 2, 3).astype(BF16), pe2.reshape(ratio, 1, CMP_STRIDE * KV_W)


def _overlap(n_cmp_pad, n_slc_pad):
    n = np.arange(n_cmp_pad)[:, None] * CMP_STRIDE
    j = np.arange(n_slc_pad)[None, :] * SLC_BLOCK
    return jnp.asarray((n < j + SLC_BLOCK) & (n + CMP_BLOCK > j), dtype=BF16)


def _suffix_matrix(n):
    return jnp.asarray(np.arange(n)[:, None] > np.arange(n)[None, :], dtype=BF16)


def _pair_suffix_matrix(n):
    u = (np.arange(n)[:, None] > np.arange(n)[None, :]).astype(np.float32)
    one, z = np.ones((n, n), np.float32), np.zeros((n, n), np.float32)
    return jnp.asarray(np.block([[u, z, one, z], [z, u, z, one]]), dtype=BF16)


def _feature_major_state(xt, heads):
    d, b, _, t = xt.shape
    return xt.reshape(d, b, heads, HEAD_DIM, t).transpose(0, 1, 4, 2, 3)


def kernel(x_prompt, x_sample, mem_prompt, cache_nsa_cmp_k, cache_nsa_cmp_v, cache_nsa_slc_k, cache_nsa_slc_v, cache_nsa_win_k, cache_nsa_win_v, cache_sb_k, cache_sb_v, cache_mem_k, cache_mem_v, state_lru_h, state_lru_conv, page_table, w_in, nsa_cmp_w, nsa_cmp_pe, lru_conv_w, lru_conv_b, lru_wa, lru_ba, lru_wx, lru_bx, lru_lambda, w_mem_kv, w_branch, w_out, ln1_g, ln1_b, ln2_g, ln2_b, ffn_w13, ffn_w2, moe_router_w, moe_router_b, moe_w13, moe_w2):
    depth = w_in.shape[0]
    B, T, _ = x_prompt.shape
    Bd, Td, _ = x_sample.shape
    NP = page_table.shape[1]
    page = cache_nsa_cmp_k.shape[2]
    past = NP * page
    win_buf = cache_nsa_win_k.shape[2]
    n_mem = mem_prompt.shape[1]
    alpha = (2.0 * depth) ** 0.25
    assert page == LANES and T % 512 == 0 and T // SLC_BLOCK >= SLC_TOP and T // SLC_BLOCK <= LANES
    assert win_buf == WINDOW and Td < CMP_STRIDE and NP % 8 == 0
    P = 8
    rows = NSA_HEADS * Td

    fm = lambda c: c.transpose(0, 1, 3, 4, 2).reshape(c.shape[0], c.shape[1], c.shape[3] * c.shape[4], c.shape[2])
    slc_kc, slc_vc, sb_kc, sb_vc = fm(cache_nsa_slc_k), fm(cache_nsa_slc_v), fm(cache_sb_k), fm(cache_sb_v)
    win_kc, win_vc = fm(cache_nsa_win_k), fm(cache_nsa_win_v)
    cmp_kc, cmp_vc = fm(cache_nsa_cmp_k), fm(cache_nsa_cmp_v)
    P_cmp = 16 if NP % 16 == 0 else 8

    n_cmp_p = T // CMP_STRIDE
    ovl_p = _overlap(n_cmp_p, LANES)
    n_slc_s = -(-(past + Td) // SLC_BLOCK)
    nsp_s = -(-n_slc_s // LANES) * LANES
    n_cmp_s = past // CMP_STRIDE
    ovl_s = _overlap(n_cmp_s, nsp_s)
    u_blk = _suffix_matrix(LANES)
    u_pair = _pair_suffix_matrix(LANES)
    hs = jnp.asarray((np.arange(rows)[:, None] // (NSA_GROUP * Td) == np.arange(rows)[None, :] // (NSA_GROUP * Td))
                     & (np.arange(rows)[:, None] % Td == np.arange(rows)[None, :] % Td), dtype=BF16)

    xp = x_prompt
    xs = x_sample.reshape(1, Bd * Td, D_MODEL)
    mem2 = mem_prompt.reshape(1, B * n_mem, D_MODEL)
    p_layers, s_layers = [], []
    for l in range(depth):
        wt = jnp.swapaxes(w_in[l], 0, 1)
        c = np.cumsum([0, 512, 128, 128, 128, 128, 128, 128, 24, 512, 512, 512, 512, 512, 512, 4096])
        seg = lambda a, b: wt[c[a]:c[b]]
        ng_w = jnp.pad(seg(7, 8), ((0, LANES - 24), (0, 0)))
        wn = jnp.concatenate([seg(0, 1), ng_w, seg(8, 10), seg(10, 11), seg(13, 14), seg(1, 3)], axis=0)
        wn = jnp.swapaxes(wn, 0, 1).astype(BF16)
        wtr = jnp.concatenate([seg(1, 7), seg(11, 13)], axis=0).astype(BF16)
        w_all = jnp.swapaxes(jnp.concatenate([seg(0, 7), ng_w, seg(8, 14)], axis=0), 0, 1).astype(BF16)
        wmg = jnp.swapaxes(seg(14, 15), 0, 1).astype(BF16)
        cw_tok, cw_feat, pe2 = _cmp_weights(nsa_cmp_w[l], nsa_cmp_pe[l])
        eye8 = jnp.eye(LRU_BLOCKS, dtype=F32)
        bd = lambda w: jnp.einsum('ncd,nm->ncmd', w, eye8).reshape(D_RNN, D_RNN)
        wbd = jnp.concatenate([bd(lru_wa[l]), bd(lru_wx[l])], axis=1).astype(BF16)
        gbias = jnp.concatenate([lru_ba[l].reshape(1, D_RNN), lru_bx[l].reshape(1, D_RNN)], axis=1)
        lam = lru_lambda[l].reshape(1, D_RNN)
        conv_w, conv_b = lru_conv_w[l], lru_conv_b[l].reshape(1, D_RNN)
        wmem = w_mem_kv[l].astype(BF16)
        wbr, wo = w_branch[l].astype(BF16), w_out[l].astype(BF16)
        g1, b1 = ln1_g[l].reshape(1, D_MODEL), ln1_b[l].reshape(1, D_MODEL)
        g2, b2 = ln2_g[l].reshape(1, D_MODEL), ln2_b[l].reshape(1, D_MODEL)

        ncols = [(0, 512), (512, 128), (640, 512), (1152, 512), (1664, 512), (2176, 512), (2688, 128), (2816, 128)]
        tcols = [(0, 128, ("full",)), (128, 128, ("full",)), (256, 128, ("full", "blk")), (384, 128, ("full", "blk")),
                 (512, 128, ("full", "blk")), (640, 128, ("full", "blk")), (768, 512, ("full", "blk")),
                 (1280, 512, ("full", "blk"))]
        (nq, ng, lx, lg, sq, mq, ck_tok, cv_tok, ck_t, cv_t, sk_t, sk_b, sv_t, sv_b, wk_t, wk_b, wv_t, wv_b,
         sbk_t, sbk_b, sbv_t, sbv_b) = _proj(xp, wn, ncols, wtr, tcols)
        ckc, cvc = _compress(ck_tok.reshape(B, n_cmp_p, CMP_STRIDE * KV_W), cv_tok.reshape(B, n_cmp_p, CMP_STRIDE * KV_W),
                             cw_feat[0], cw_feat[1], pe2)
        o_nsa = _nsa_prompt(nq, ng, ckc, cvc, sk_b, sv_b, wk_b, wv_b, ovl_p)
        o_lru, h_new, conv_new = _rglru(lx, lg, jnp.zeros((B, CONV_W - 1, D_RNN), F32), jnp.zeros((B, 1, D_RNN), F32),
                                        conv_w, conv_b, wbd, gbias, lam)
        o_sb = _sb_prompt(sq, sbk_b, sbv_b, u_pair)
        mk, mv = _proj(mem2, wmem, [(0, 512), (512, 512)])
        mk, mv = mk.reshape(B, n_mem, 512), mv.reshape(B, n_mem, 512)
        o_mem = _mem_attn(mq, mk, mv)
        M = B * T
        x1 = _merge_ln(xp.reshape(M, D_MODEL), [o.reshape(M, 512) for o in (o_nsa, o_lru, o_sb, o_mem)],
                       wmg, wbr, wo, g1, b1, alpha)
        p_layers.append((ck_t, cv_t, sk_t, sv_t, wk_t[:, :, T - min(WINDOW, T):], wv_t[:, :, T - min(WINDOW, T):],
                         sbk_t, sbv_t, h_new.reshape(B, D_RNN), conv_new,
                         mk.reshape(B, n_mem, MEM_HEADS, MEM_HEAD_DIM), mv.reshape(B, n_mem, MEM_HEADS, MEM_HEAD_DIM)))

        ms = Bd * Td
        scols = [(0, 512), (512, 128), (640, 128), (768, 128), (896, 128), (1024, 128), (1152, 128), (1280, 128),
                 (1408, 512), (1920, 512), (2432, 512), (2944, 512), (3456, 512), (3968, 512)]
        (nq_s, ck_s, cv_s, sk_s, sv_s, wk_s, wv_s, ng_s, lx_s, lg_s, sq_s, sbk_s, sbv_s, mq_s) = [
            a.reshape(Bd, Td, a.shape[-1]) for a in _proj(xs, w_all, scols)]
        ak, bk = _compress_paged(cmp_kc, l, page_table, cw_tok[0], pe2, P_cmp)
        av, bv = _compress_paged(cmp_vc, l, page_table, cw_tok[1], pe2, P_cmp)
        q8 = nq_s.reshape(Bd, Td, NSA_HEADS, HEAD_DIM).transpose(0, 2, 1, 3)
        gsel = jnp.asarray(np.arange(NSA_HEADS)[:, None] // NSA_GROUP == np.arange(NSA_KV)[None, :], dtype=F32)
        qz = (q8[:, :, :, None, :] * gsel[None, :, None, :, None]).reshape(Bd, rows, KV_W)
        hsel = jnp.eye(SB_HEADS, dtype=F32)
        s8 = sq_s.reshape(Bd, Td, SB_HEADS, HEAD_DIM).transpose(0, 2, 1, 3)
        qbd = (s8[:, :, :, None, :] * hsel[None, :, None, :, None]).reshape(Bd, rows, SB_W)
        graw = jnp.pad(ng_s[:, :, :24].reshape(Bd, Td, NSA_HEADS, 3).transpose(0, 2, 1, 3).reshape(Bd, rows, 3),
                       ((0, 0), (0, 0), (0, LANES - 3)))
        new_fm = lambda a: jnp.pad(jnp.swapaxes(a, 1, 2), ((0, 0), (0, 0), (0, LANES - Td)))
        ocmp, sel = _nsa_s_cmp(qz, ak, bk, av, bv, ovl_s, hs, Td, past, n_slc_s)
        wkx = jnp.concatenate([win_kc[l], new_fm(wk_s)], axis=2)
        wvx = jnp.concatenate([win_vc[l], new_fm(wv_s)], axis=2)
        o_rows = _nsa_s_slc(page_table, l, slc_kc, slc_vc, qz, graw, ocmp, sel, new_fm(sk_s), new_fm(sv_s),
                            wkx, wvx, Td, past, P)
        o6 = o_rows.reshape(Bd, NSA_KV, NSA_GROUP, Td, NSA_KV, HEAD_DIM)
        o_nsa_s = jnp.stack([o6[:, g, :, :, g, :] for g in range(NSA_KV)], axis=1)
        o_nsa_s = o_nsa_s.reshape(Bd, NSA_HEADS, Td, HEAD_DIM).transpose(0, 2, 1, 3).reshape(Bd, Td, NSA_HEADS * HEAD_DIM)
        o_lru_s, h_new_s, conv_new_s = _rglru(lx_s, lg_s, state_lru_conv[l], state_lru_h[l].reshape(Bd, 1, D_RNN),
                                              conv_w, conv_b, wbd, gbias, lam)
        sb_rows = _sb_sample(page_table, l, sb_kc, sb_vc, qbd, new_fm(sbk_s), new_fm(sbv_s), u_blk, Td, P)
        sb5 = sb_rows.reshape(Bd, SB_HEADS, Td, SB_HEADS, HEAD_DIM)
        o_sb_s = jnp.stack([sb5[:, h, :, h, :] for h in range(SB_HEADS)], axis=2).reshape(Bd, Td, SB_W)
        o_mem_s = _mem_attn(mq_s, cache_mem_k[l].reshape(Bd, n_mem, 512), cache_mem_v[l].reshape(Bd, n_mem, 512))
        x1s = _merge_ln(xs.reshape(ms, D_MODEL), [o.reshape(ms, 512) for o in (o_nsa_s, o_lru_s, o_sb_s, o_mem_s)],
                        wmg, wbr, wo, g1, b1, alpha)
        kv4 = lambda a: a.reshape(Bd, Td, NSA_KV, HEAD_DIM)
        s_layers.append((kv4(ck_s), kv4(cv_s), kv4(sk_s), kv4(sv_s), kv4(wk_s), kv4(wv_s),
                         sbk_s.reshape(Bd, Td, SB_HEADS, HEAD_DIM), sbv_s.reshape(Bd, Td, SB_HEADS, HEAD_DIM),
                         h_new_s.reshape(Bd, D_RNN), conv_new_s))

        if l % 2 == 0:
            w13, w2 = ffn_w13[l // 2].astype(BF16), ffn_w2[l // 2].astype(BF16)
            x2 = _ffn_ln(x1, w13, w2, g2, b2, alpha)
            x2s = _ffn_ln(x1s, w13, w2, g2, b2, alpha)
        else:
            rw = jnp.pad(moe_router_w[l // 2], ((0, 0), (0, LANES - N_EXP)))
            rwh = rw.astype(BF16)
            rwl = (rw - rwh.astype(F32)).astype(BF16)
            rb = jnp.pad(moe_router_b[l // 2].reshape(1, N_EXP), ((0, 0), (0, LANES - N_EXP)))
            w13, w2 = moe_w13[l // 2].astype(BF16), moe_w2[l // 2].astype(BF16)
            x2 = _moe_ln(x1, rwh, rwl, rb, w13, w2, g2, b2, alpha)
            x2s = _moe_ln(x1s, rwh, rwl, rb, w13, w2, g2, b2, alpha)
        xp = x2.reshape(B, T, D_MODEL)
        xs = x2s.reshape(1, ms, D_MODEL)

    pst = [jnp.stack(r, axis=0) for r in zip(*p_layers)]
    prompt_state = ([_feature_major_state(a, NSA_KV) for a in pst[:6]] + [_feature_major_state(a, SB_HEADS) for a in pst[6:8]]
                    + pst[8:])
    sample_state = [jnp.stack(r, axis=0) for r in zip(*s_layers)]
    return (xp, xs.reshape(Bd, Td, D_MODEL), *prompt_state, *sample_state)
```

```python
import functools

import numpy as np
import jax
import jax.numpy as jnp
from jax import lax
from jax.experimental import pallas as pl
from jax.experimental.pallas import tpu as pltpu

F32 = jnp.float32
BF16 = jnp.bfloat16

D_MODEL = 1024
HEAD_DIM = 64
NSA_HEADS = 8
NSA_KV = 2
NSA_GROUP = NSA_HEADS // NSA_KV
CMP_STRIDE = 16
CMP_BLOCK = 32
SLC_BLOCK = 64
SLC_TOP = 16
WINDOW = 512
FORCE_SCORE = 1.0e4
SB_HEADS = 8
D_RNN = 512
LRU_BLOCKS = 8
CONV_W = 4
LRU_C = 8.0
MEM_HEADS = 4
MEM_HEAD_DIM = 128
N_EXP = 8
LN_EPS = 1e-5
NEG = -1.0e30
TINY = 1.0e-30
SB_DEAD_LOG = -106.0
KV_W = NSA_KV * HEAD_DIM
SB_W = SB_HEADS * HEAD_DIM
LANES = 128
VMEM_LIMIT = 56 * 1024 * 1024
SLOPES = tuple(float(2.0 ** (-8.0 * (h + 1) / NSA_HEADS)) for h in range(NSA_HEADS))


def _cparams(sem, big=False):
    return pltpu.CompilerParams(dimension_semantics=sem, vmem_limit_bytes=VMEM_LIMIT if big else None)


def _dot(a, b):
    return jnp.dot(a, b, preferred_element_type=F32)


def _dot_nt(a, b):
    return lax.dot_general(a, b, (((1,), (1,)), ((), ())), preferred_element_type=F32)


def _split2(x):
    hi = x.astype(BF16)
    lo = (x - hi.astype(F32)).astype(BF16)
    return hi, lo


def _split3(x):
    hi = x.astype(BF16)
    r = x - hi.astype(F32)
    mid = r.astype(BF16)
    lo = (r - mid.astype(F32)).astype(BF16)
    return hi, mid, lo


def _softplus(x):
    return jnp.maximum(x, 0.0) + jnp.log1p(jnp.exp(-jnp.abs(x)))


def _layer_norm(z, g, b):
    mu = jnp.mean(z, axis=-1, keepdims=True)
    zc = z - mu
    var = jnp.mean(zc * zc, axis=-1, keepdims=True)
    return zc * lax.rsqrt(var + LN_EPS) * g + b


def _masked_softmax(s, mask):
    s = jnp.where(mask, s, NEG)
    m = jnp.max(s, axis=-1, keepdims=True)
    e = jnp.where(mask, jnp.exp(s - m), 0.0)
    return e / jnp.maximum(jnp.sum(e, axis=-1, keepdims=True), TINY)


def _topk_mask(score, lanef, k):
    sel = jnp.zeros_like(score)
    s = score
    for _ in range(k):
        m = jnp.max(s, axis=-1, keepdims=True)
        jm = jnp.min(jnp.where(s == m, lanef, 1.0e9), axis=-1, keepdims=True)
        hit = lanef == jm
        sel = jnp.where(hit, 1.0, sel)
        s = jnp.where(hit, -3.0, s)
    return sel


def _topk_rank(score_t, k):
    R, N = score_t.shape
    nv = R // 8
    groups = [score_t[8 * v:8 * v + 8] for v in range(nv)]
    sub = lax.broadcasted_iota(jnp.int32, (8, N), 0)
    ranks = [jnp.zeros((8, N), F32) for _ in range(nv)]
    for jp in range(R):
        sj = score_t[jp:jp + 1, :]
        vj, rj = divmod(jp, 8)
        for v in range(nv):
            if v > vj:
                beats = sj >= groups[v]
            elif v < vj:
                beats = sj > groups[v]
            else:
                beats = (sj > groups[v]) | ((sj == groups[v]) & (sub > rj))
            ranks[v] = ranks[v] + jnp.where(beats, 1.0, 0.0)
    return jnp.concatenate([jnp.where(r < k, 1.0, 0.0) for r in ranks], axis=0)


def _proj_body(*refs, ncols, tcols, has_t):
    x_ref, wn_ref = refs[0], refs[1]
    o_refs = refs[3:] if has_t else refs[2:]
    xb = x_ref[...].astype(BF16)
    k = 0
    for c0, wd in ncols:
        o_refs[k][...] = _dot(xb, wn_ref[:, c0:c0 + wd]).astype(o_refs[k].dtype)
        k += 1
    if has_t:
        wt_ref = refs[2]
        for c0, wd, kinds in tcols:
            r = _dot_nt(wt_ref[c0:c0 + wd, :], xb)
            for kind in kinds:
                if kind == "full":
                    o_refs[k][...] = r
                else:
                    rb = r.astype(BF16)
                    for c in range(rb.shape[1] // LANES):
                        o_refs[k][c] = rb[:, c * LANES:(c + 1) * LANES]
                k += 1


def _proj(x, wn, ncols, wt=None, tcols=(), tm=512):
    B, T, K = x.shape
    tm = min(tm, T)
    has_t = wt is not None
    in_specs = [pl.BlockSpec((None, tm, K), lambda b, t: (b, t, 0)),
                pl.BlockSpec(wn.shape, lambda b, t: (0, 0))]
    args = [x, wn]
    if has_t:
        in_specs.append(pl.BlockSpec(wt.shape, lambda b, t: (0, 0)))
        args.append(wt)
    out_specs, out_shape = [], []
    for _, wd in ncols:
        out_specs.append(pl.BlockSpec((None, tm, wd), lambda b, t: (b, t, 0)))
        out_shape.append(jax.ShapeDtypeStruct((B, T, wd), F32))
    for _, wd, kinds in tcols:
        for kind in kinds:
            if kind == "full":
                out_specs.append(pl.BlockSpec((None, wd, tm), lambda b, t: (b, 0, t)))
                out_shape.append(jax.ShapeDtypeStruct((B, wd, T), F32))
            else:
                nb = tm // LANES
                out_specs.append(pl.BlockSpec((None, nb, wd, LANES), lambda b, t: (b, t, 0, 0)))
                out_shape.append(jax.ShapeDtypeStruct((B, T // LANES, wd, LANES), BF16))
    return pl.pallas_call(
        functools.partial(_proj_body, ncols=tuple(ncols), tcols=tuple(tcols), has_t=has_t),
        grid=(B, T // tm), in_specs=in_specs, out_specs=out_specs, out_shape=out_shape,
        compiler_params=_cparams(("parallel", "parallel"), big=True), name="proj")(*args)


def _compress_body(xk_ref, xv_ref, wk_ref, wv_ref, pe_ref, ok_ref, ov_ref):
    n = xk_ref.shape[0]
    colid = lax.broadcasted_iota(jnp.int32, (KV_W, n), 1)
    for x_ref, w_ref, o_ref in ((xk_ref, wk_ref, ok_ref), (xv_ref, wv_ref, ov_ref)):
        x = x_ref[...]
        a = _dot_nt(w_ref[0], (x + pe_ref[0]).astype(BF16))
        b = _dot_nt(w_ref[1], (x + pe_ref[1]).astype(BF16))
        o_ref[...] = jnp.where(colid < n - 1, a + pltpu.roll(b, n - 1, 1), 0.0)


def _compress(xk, xv, wk, wv, pe):
    B, n, cw = xk.shape
    xspec = pl.BlockSpec((None, n, cw), lambda b: (b, 0, 0))
    wspec = pl.BlockSpec(wk.shape, lambda b: (0, 0, 0))
    ospec = pl.BlockSpec((None, KV_W, n), lambda b: (b, 0, 0))
    return pl.pallas_call(
        _compress_body, grid=(B,),
        in_specs=[xspec, xspec, wspec, wspec, pl.BlockSpec(pe.shape, lambda b: (0, 0, 0))],
        out_specs=[ospec, ospec], out_shape=[jax.ShapeDtypeStruct((B, KV_W, n), F32)] * 2,
        compiler_params=_cparams(("parallel",), big=True), name="compress")(xk, xv, wk, wv, pe)


def _compress_paged_body(pt_ref, *refs, P):
    x_refs = refs[:P]
    w_ref, pe_ref, a_ref, b_ref, tok = refs[P:]
    page = x_refs[0].shape[1]
    for i in range(P):
        tok[i * page:(i + 1) * page, :] = x_refs[i][...].T
    nch = P * page // CMP_STRIDE
    x = jnp.concatenate([tok[pl.ds(c, nch, stride=CMP_STRIDE), :] for c in range(CMP_STRIDE)], axis=1)
    a_ref[...] = _dot((x + pe_ref[0]).astype(BF16), w_ref[0])
    b_ref[...] = _dot((x + pe_ref[1]).astype(BF16), w_ref[1])


def _compress_paged(cache, layer, page_table, w, pe, P):
    Bd, NP = page_table.shape
    page = cache.shape[-1]
    nch = P * page // CMP_STRIDE
    in_specs = [pl.BlockSpec((None, None, KV_W, page), lambda b, s, pt, i=i: (layer, pt[b, s * P + i], 0, 0))
                for i in range(P)]
    in_specs += [pl.BlockSpec(w.shape, lambda b, s, pt: (0, 0, 0)), pl.BlockSpec(pe.shape, lambda b, s, pt: (0, 0, 0))]
    ospec = pl.BlockSpec((None, nch, KV_W), lambda b, s, pt: (b, s, 0))
    grid_spec = pltpu.PrefetchScalarGridSpec(num_scalar_prefetch=1, grid=(Bd, NP // P), in_specs=in_specs,
                                             out_specs=[ospec, ospec],
                                             scratch_shapes=[pltpu.VMEM((P * page, KV_W), F32)])
    return pl.pallas_call(
        functools.partial(_compress_paged_body, P=P), grid_spec=grid_spec,
        out_shape=[jax.ShapeDtypeStruct((Bd, NP * page // CMP_STRIDE, KV_W), F32)] * 2,
        compiler_params=_cparams(("parallel", "parallel"), big=True), name="compress_paged")(
            page_table, *([cache] * P), w, pe)


def _nsa_prompt_body(q_ref, ng_ref, ckt_ref, cvt_ref, sk_ref, sv_ref, wk_ref, wv_ref, ovl_ref, o_ref,
                     *, Q, KBN, n_slc):
    i = pl.program_id(1)
    q0 = i * Q
    KB = KBN * LANES
    QB = Q // LANES
    lane = lax.broadcasted_iota(jnp.int32, (Q, LANES), 1)
    lo_half = lane < HEAD_DIM
    qpos = q0 + lax.broadcasted_iota(jnp.int32, (Q, 1), 0)
    gate = jax.nn.sigmoid(ng_ref[...])
    slabs = [q_ref[:, LANES * s:LANES * (s + 1)] * (HEAD_DIM ** -0.5) for s in range(4)]

    def placed(h, g):
        sl = slabs[h // 2]
        if (h % 2) != g:
            sl = pltpu.roll(sl, HEAD_DIM, 1)
        return jnp.where(lo_half if g == 0 else jnp.logical_not(lo_half), sl, 0.0)

    ckt = ckt_ref[...].astype(BF16)
    cvt = cvt_ref[...].astype(BF16)
    NC = ckt.shape[1]
    cend = CMP_STRIDE * lax.broadcasted_iota(jnp.int32, (1, NC), 1) + (CMP_BLOCK - 1)
    dist_c = qpos - cend
    vis_c = dist_c >= 0
    dist_cf = dist_c.astype(F32)
    blk_t = lax.broadcasted_iota(jnp.int32, (n_slc, NSA_KV * Q), 0)
    col_t = lax.broadcasted_iota(jnp.int32, (n_slc, NSA_KV * Q), 1)
    qpos_t = q0 + (col_t & (Q - 1))
    cur_t = qpos_t >> 6
    valid_t = blk_t * SLC_BLOCK <= qpos_t
    forced_t = (blk_t == 0) | (blk_t == cur_t) | (blk_t == cur_t - 1)
    rows_of = [slice(h * Q, (h + 1) * Q) for h in range(NSA_HEADS)]
    group_of = [h // NSA_GROUP for h in range(NSA_HEADS)]
    qall = jnp.concatenate([placed(h, group_of[h]) for h in range(NSA_HEADS)], axis=0).astype(BF16)

    def softmax_rows(sd, distf, mask):
        ps = []
        for h in range(NSA_HEADS):
            s = jnp.where(mask, sd[rows_of[h]] - SLOPES[h] * distf, NEG)
            e = jnp.where(mask, jnp.exp(s - jnp.max(s, axis=-1, keepdims=True)), 0.0)
            ps.append(e / jnp.maximum(jnp.sum(e, axis=-1, keepdims=True), TINY))
        return ps

    sb0 = jnp.maximum(i * QB - WINDOW // LANES, 0)
    nwb = WINDOW // LANES + QB
    kt_w = jnp.concatenate([wk_ref[sb0 + c] for c in range(nwb)], axis=1)
    vt_w = jnp.concatenate([wv_ref[sb0 + c] for c in range(nwb)], axis=1)
    dist_w = qpos - (sb0 * LANES + lax.broadcasted_iota(jnp.int32, (1, nwb * LANES), 1))
    sd_c = _dot(qall, ckt)
    sd_w = _dot(qall, kt_w)
    ps = softmax_rows(sd_c, dist_cf, vis_c)
    pw = softmax_rows(sd_w, dist_w.astype(F32), (dist_w >= 0) & (dist_w <= WINDOW))
    o_cmp = _dot_nt(jnp.concatenate(ps, axis=0).astype(BF16), cvt)
    o_win = _dot_nt(jnp.concatenate(pw, axis=0).astype(BF16), vt_w)
    psum = jnp.concatenate([(ps[NSA_GROUP * g] + ps[NSA_GROUP * g + 1]) + (ps[NSA_GROUP * g + 2] + ps[NSA_GROUP * g + 3])
                            for g in range(NSA_KV)], axis=0)
    ph, pl_ = _split2(psum)
    imp = _dot(ph, ovl_ref[...]) + _dot(pl_, ovl_ref[...])
    score_t = jnp.where(valid_t, jnp.where(forced_t, FORCE_SCORE, imp.T[:n_slc]), -1.0)
    sel_t = _topk_rank(score_t, SLC_TOP)
    if n_slc < LANES:
        sel_t = jnp.concatenate([sel_t, jnp.zeros((LANES - n_slc, NSA_KV * Q), F32)], axis=0)
    sel_b = sel_t.T.astype(BF16)
    blk_row = lax.broadcasted_iota(jnp.int32, (LANES, KB), 0)

    half_rows = lax.broadcasted_iota(jnp.int32, (LANES, KB), 0) < HEAD_DIM

    def kt_body(t, carry):
        m, acc = carry
        kt = jnp.concatenate([sk_ref[t * KBN + c] for c in range(KBN)], axis=1)
        vt = jnp.concatenate([sv_ref[t * KBN + c] for c in range(KBN)], axis=1)
        one = jnp.ones_like(vt)
        vts = [jnp.where(half_rows, vt, one), jnp.where(half_rows, one, vt)]
        kpos = t * KB + lax.broadcasted_iota(jnp.int32, (1, KB), 1)
        expand = ((kpos >> 6) == blk_row).astype(BF16)
        sd = _dot(qall, kt)
        selx = _dot(sel_b, expand)
        dist = qpos - kpos
        causal = dist >= 0
        masks = [(selx[g * Q:(g + 1) * Q] > 0.5) & causal for g in range(NSA_KV)]
        distf = dist.astype(F32)
        es, alphas, ms = [], [], []
        for h in range(NSA_HEADS):
            mask = masks[group_of[h]]
            s = jnp.where(mask, sd[rows_of[h]] - SLOPES[h] * distf, NEG)
            m_old = m[rows_of[h]]
            m_new = jnp.maximum(m_old, jnp.max(s, axis=-1, keepdims=True))
            alphas.append(jnp.exp(m_old - m_new))
            ms.append(m_new)
            es.append(jnp.where(mask, jnp.exp(s - m_new), 0.0).astype(BF16))
        pv = jnp.concatenate([_dot_nt(jnp.concatenate(es[NSA_GROUP * g:NSA_GROUP * (g + 1)], axis=0), vts[g])
                              for g in range(NSA_KV)], axis=0)
        return jnp.concatenate(ms, axis=0), jnp.concatenate(alphas, axis=0) * acc + pv

    nkt = (q0 + Q + KB - 1) // KB
    init = (jnp.full((NSA_HEADS * Q, 1), NEG, F32), jnp.zeros((NSA_HEADS * Q, LANES), F32))
    _, acc_s = lax.fori_loop(0, nkt, kt_body, init)
    o_slc = acc_s / jnp.maximum(pltpu.roll(acc_s, HEAD_DIM, 1), TINY)
    outs = [None] * NSA_HEADS
    for h in range(NSA_HEADS):
        rows = rows_of[h]
        oh = (gate[:, 3 * h:3 * h + 1] * o_cmp[rows] + gate[:, 3 * h + 1:3 * h + 2] * o_slc[rows]
              + gate[:, 3 * h + 2:3 * h + 3] * o_win[rows])
        if (h % 2) != group_of[h]:
            oh = pltpu.roll(oh, HEAD_DIM, 1)
        outs[h] = oh
    for s in range(4):
        o_ref[:, LANES * s:LANES * (s + 1)] = jnp.where(lo_half, outs[2 * s], outs[2 * s + 1])


def _nsa_prompt(q, ng, ckt, cvt, skb, svb, wkb, wvb, ovl, Q=128, KBN=4):
    B, T, _ = q.shape
    n_slc = T // SLC_BLOCK
    NC = ckt.shape[2]
    nblk = T // LANES
    kvspec = pl.BlockSpec((None, nblk, KV_W, LANES), lambda b, i: (b, 0, 0, 0))
    cspec = pl.BlockSpec((None, KV_W, NC), lambda b, i: (b, 0, 0))
    return pl.pallas_call(
        functools.partial(_nsa_prompt_body, Q=Q, KBN=KBN, n_slc=n_slc), grid=(B, T // Q),
        in_specs=[pl.BlockSpec((None, Q, NSA_HEADS * HEAD_DIM), lambda b, i: (b, i, 0)),
                  pl.BlockSpec((None, Q, LANES), lambda b, i: (b, i, 0)),
                  cspec, cspec, kvspec, kvspec, kvspec, kvspec,
                  pl.BlockSpec(ovl.shape, lambda b, i: (0, 0))],
        out_specs=pl.BlockSpec((None, Q, NSA_HEADS * HEAD_DIM), lambda b, i: (b, i, 0)),
        out_shape=jax.ShapeDtypeStruct((B, T, NSA_HEADS * HEAD_DIM), F32),
        compiler_params=_cparams(("parallel", "parallel"), big=True), name="nsa_prompt")(
            q, ng, ckt, cvt, skb, svb, wkb, wvb, ovl)


def _sb_block(qh, kt, vt, u, carry, acc, mask):
    n = qh.shape[0]
    s = _dot(qh, kt)
    sp = _softplus(s)
    l1m = -sp if mask is None else jnp.where(mask, -sp, 0.0)
    hi, lo = _split2(l1m)
    r2 = _dot(jnp.concatenate([hi, lo], axis=0), u)
    after = r2[:n] + r2[n:]
    w = jnp.exp((s - sp) + after + carry)
    if mask is not None:
        w = jnp.where(mask, w, 0.0)
    acc = acc + _dot_nt(w.astype(BF16), vt)
    carry = carry + jnp.sum(l1m, axis=-1, keepdims=True)
    return carry, acc


def _sb_pairs_step(qs, kts, vts, u2_ref, row_lo, carries, accs, mask2):
    n = qs[0].shape[0]
    zero = jnp.zeros_like(kts[0])
    bd = lambda t: jnp.concatenate([jnp.where(row_lo, t, zero), jnp.where(row_lo, zero, t)], axis=1)
    ss = [_dot(q, bd(kt)) for q, kt in zip(qs, kts)]
    sps, cats = [], []
    for s in ss:
        sp = jnp.maximum(s, 0.0) + jnp.log(1.0 + jnp.exp(-jnp.abs(s)))
        l1m = -sp if mask2 is None else jnp.where(mask2, -sp, 0.0)
        hi, lo = _split2(l1m)
        sps.append(sp)
        cats.append(jnp.concatenate([hi, lo], axis=0))
    rs = [_dot(c, u2_ref[...]) for c in cats]
    ws, new_c = [], []
    for s, sp, r, carry2 in zip(ss, sps, rs, carries):
        r = r[:n] + r[n:]
        kb2 = s.shape[1]
        w = jnp.exp((s - sp) + r[:, :kb2] + carry2)
        if mask2 is not None:
            w = jnp.where(mask2, w, 0.0)
        ws.append(w.astype(BF16))
        new_c.append(carry2 + r[:, kb2:])
    return new_c, [acc + _dot_nt(w, bd(vt)) for acc, w, vt in zip(accs, ws, vts)]


def _sb_prompt_body(q_ref, kt_ref, vt_ref, u2_ref, o_ref, carry_s, acc_s, *, Q):
    i = pl.program_id(1)
    npair = SB_W // LANES
    tri = lax.broadcasted_iota(jnp.int32, (Q, Q), 1) < lax.broadcasted_iota(jnp.int32, (Q, Q), 0)
    mask2 = jnp.concatenate([tri, tri], axis=1)
    row_lo = lax.broadcasted_iota(jnp.int32, (LANES, Q), 0) < HEAD_DIM
    slab = [slice(LANES * p, LANES * (p + 1)) for p in range(npair)]

    def step(j, carries, accs, mask):
        qs = [(q_ref[:, slab[p]] * (HEAD_DIM ** -0.5)).astype(BF16) for p in range(npair)]
        new_c, new_a = _sb_pairs_step(qs, [kt_ref[j, slab[p], :] for p in range(npair)],
                                      [vt_ref[j, slab[p], :] for p in range(npair)], u2_ref, row_lo,
                                      carries, accs, mask)
        for p in range(npair):
            carry_s[p] = new_c[p]
            acc_s[p] = new_a[p]

    step(i, [jnp.zeros((Q, 2 * Q), F32)] * npair, [jnp.zeros((Q, LANES), F32)] * npair, mask2)

    def alive():
        mx = carry_s[0]
        for p in range(1, npair):
            mx = jnp.maximum(mx, carry_s[p])
        return (jnp.max(mx) > SB_DEAD_LOG).astype(jnp.int32)

    def body(st):
        jj, _ = st
        step(i - 1 - jj, [carry_s[p] for p in range(npair)], [acc_s[p] for p in range(npair)], None)
        return jj + 1, alive()

    lax.while_loop(lambda st: (st[0] < i) & (st[1] > 0), body, (jnp.int32(0), alive()))
    for p in range(npair):
        o_ref[:, slab[p]] = acc_s[p]


def _sb_prompt(q, ktb, vtb, u2):
    B, T, _ = q.shape
    Q = LANES
    nblk = T // LANES
    kvspec = pl.BlockSpec((None, nblk, SB_W, LANES), lambda b, i: (b, 0, 0, 0))
    return pl.pallas_call(
        functools.partial(_sb_prompt_body, Q=Q), grid=(B, T // Q),
        in_specs=[pl.BlockSpec((None, Q, SB_W), lambda b, i: (b, i, 0)), kvspec, kvspec,
                  pl.BlockSpec(u2.shape, lambda b, i: (0, 0))],
        out_specs=pl.BlockSpec((None, Q, SB_W), lambda b, i: (b, i, 0)),
        out_shape=jax.ShapeDtypeStruct((B, T, SB_W), F32),
        scratch_shapes=[pltpu.VMEM((SB_W // LANES, Q, 2 * Q), F32), pltpu.VMEM((SB_W // LANES, Q, LANES), F32)],
        compiler_params=_cparams(("parallel", "parallel"), big=True), name="sb_prompt")(q, ktb, vtb, u2)


def _mem_body(q_ref, k_ref, v_ref, o_ref):
    for h in range(MEM_HEADS):
        sl = slice(MEM_HEAD_DIM * h, MEM_HEAD_DIM * (h + 1))
        s = _dot_nt(q_ref[:, sl].astype(BF16), k_ref[:, sl].astype(BF16)) * (MEM_HEAD_DIM ** -0.5)
        e = jnp.exp(s - jnp.max(s, axis=-1, keepdims=True))
        p = e / jnp.sum(e, axis=-1, keepdims=True)
        o_ref[:, sl] = _dot(p.astype(BF16), v_ref[:, sl].astype(BF16))


def _mem_attn(q, mk, mv, tq=512):
    B, T, W = q.shape
    tq = min(tq, T)
    N = mk.shape[1]
    kvspec = pl.BlockSpec((None, N, W), lambda b, i: (b, 0, 0))
    return pl.pallas_call(
        _mem_body, grid=(B, T // tq),
        in_specs=[pl.BlockSpec((None, tq, W), lambda b, i: (b, i, 0)), kvspec, kvspec],
        out_specs=pl.BlockSpec((None, tq, W), lambda b, i: (b, i, 0)),
        out_shape=jax.ShapeDtypeStruct((B, T, W), F32),
        compiler_params=_cparams(("parallel", "parallel")), name="mem_attn")(q, mk, mv)


def _rglru_body(x_ref, g_ref, cb_ref, h0_ref, cw_ref, cbias_ref, wbd_ref, gb_ref, lam_ref,
                y_ref, hn_ref, cn_ref, xbuf, a_s, b_s, h_s, hc, *, n_t):
    t = pl.program_id(1)
    tt = x_ref.shape[0]
    hist = CONV_W - 1

    @pl.when(t == 0)
    def _():
        xbuf[8 - hist:8, :] = cb_ref[...]
        hc[...] = h0_ref[...]

    xbuf[8:8 + tt, :] = x_ref[...]
    xc = cw_ref[0:1, :] * xbuf[8 - hist:8 - hist + tt, :]
    for k in range(1, CONV_W):
        xc = xc + cw_ref[k:k + 1, :] * xbuf[8 - hist + k:8 - hist + k + tt, :]
    xc = xc + cbias_ref[...]
    gates = _dot(xc.astype(BF16), wbd_ref[...]) + gb_ref[...]
    r = jax.nn.sigmoid(gates[:, :D_RNN])
    ig = jax.nn.sigmoid(gates[:, D_RNN:])
    log_a = (-LRU_C * _softplus(-lam_ref[...])) * r
    a = jnp.exp(log_a)
    a_s[...] = a
    b_s[...] = jnp.sqrt(-jnp.tanh(log_a) * (a * a + 1.0)) * (ig * xc)

    def step(k, h):
        h = a_s[pl.ds(k, 1), :] * h + b_s[pl.ds(k, 1), :]
        h_s[pl.ds(k, 1), :] = h
        return h

    h = lax.fori_loop(0, tt, step, hc[...], unroll=8 if tt % 8 == 0 else tt)
    hc[...] = h
    y_ref[...] = h_s[...] * jax.nn.gelu(g_ref[...])
    hn_ref[...] = h
    cn_ref[...] = xbuf[8 + tt - hist:8 + tt, :]
    if n_t > 1:
        xbuf[0:8, :] = xbuf[tt:tt + 8, :]


def _rglru(x, gate_in, conv_buf, h0, conv_w, conv_b, wbd, gbias, lam, tt=256):
    B, T, W = x.shape
    tt = min(tt, T)
    n_t = T // tt
    xspec = pl.BlockSpec((None, tt, W), lambda b, t: (b, t, 0))
    const = lambda a: pl.BlockSpec(a.shape, lambda b, t: (0,) * a.ndim)
    return pl.pallas_call(
        functools.partial(_rglru_body, n_t=n_t), grid=(B, n_t),
        in_specs=[xspec, xspec, pl.BlockSpec((None, CONV_W - 1, W), lambda b, t: (b, 0, 0)),
                  pl.BlockSpec((None, 1, W), lambda b, t: (b, 0, 0)),
                  const(conv_w), const(conv_b), const(wbd), const(gbias), const(lam)],
        out_specs=[xspec, pl.BlockSpec((None, 1, W), lambda b, t: (b, 0, 0)),
                   pl.BlockSpec((None, CONV_W - 1, W), lambda b, t: (b, 0, 0))],
        out_shape=[jax.ShapeDtypeStruct((B, T, W), F32), jax.ShapeDtypeStruct((B, 1, W), F32),
                   jax.ShapeDtypeStruct((B, CONV_W - 1, W), F32)],
        scratch_shapes=[pltpu.VMEM((tt + 8, W), F32), pltpu.VMEM((tt, W), F32), pltpu.VMEM((tt, W), F32),
                        pltpu.VMEM((tt, W), F32), pltpu.VMEM((1, W), F32)],
        compiler_params=_cparams(("parallel", "arbitrary")), name="rglru")(
            x, gate_in, conv_buf, h0, conv_w, conv_b, wbd, gbias, lam)


def _merge_body(x_ref, b0, b1, b2, b3, wmg_ref, wbr_ref, wo_ref, g_ref, be_ref, o_ref, *, alpha):
    x = x_ref[...]
    xb = x.astype(BF16)
    acc = None
    off = 0
    for n, br in enumerate((b0, b1, b2, b3)):
        wdt = br.shape[-1]
        gate = jax.nn.sigmoid(_dot(xb, wmg_ref[:, n * D_MODEL:(n + 1) * D_MODEL]))
        term = gate * _dot(br[...].astype(BF16), wbr_ref[off:off + wdt, :])
        acc = term if acc is None else acc + term
        off += wdt
    mix = _dot(acc.astype(BF16), wo_ref[...])
    o_ref[...] = _layer_norm(alpha * x + mix, g_ref[...], be_ref[...])


def _merge_ln(x, branches, wmg, wbr, wo, g, be, alpha, tm=256):
    M = x.shape[0]
    tm = min(tm, M)
    row = lambda w: pl.BlockSpec((tm, w), lambda i: (i, 0))
    const = lambda a: pl.BlockSpec(a.shape, lambda i: (0, 0))
    return pl.pallas_call(
        functools.partial(_merge_body, alpha=alpha), grid=(M // tm,),
        in_specs=[row(D_MODEL)] + [row(b.shape[1]) for b in branches] + [const(wmg), const(wbr), const(wo),
                                                                         const(g), const(be)],
        out_specs=row(D_MODEL), out_shape=jax.ShapeDtypeStruct((M, D_MODEL), F32),
        compiler_params=_cparams(("parallel",), big=True), name="merge_ln")(x, *branches, wmg, wbr, wo, g, be)


def _ffn_body(x_ref, wa_ref, wb_ref, w2_ref, g_ref, be_ref, o_ref, acc, *, alpha, nf):
    j = pl.program_id(1)
    x = x_ref[...]
    xb = x.astype(BF16)
    a = _dot(xb, wa_ref[...])
    b = _dot(xb, wb_ref[...])
    part = _dot(((a * jax.nn.sigmoid(a)) * b).astype(BF16), w2_ref[...])

    @pl.when(j == 0)
    def _():
        acc[...] = part

    @pl.when(j > 0)
    def _():
        acc[...] += part

    @pl.when(j == nf - 1)
    def _():
        o_ref[...] = _layer_norm(alpha * x + acc[...], g_ref[...], be_ref[...])


def _ffn_ln(x, w13, w2, g, be, alpha, tm=512, tf=1408):
    M = x.shape[0]
    tm = min(tm, M)
    F = w2.shape[0]
    nf = F // tf
    const = lambda a: pl.BlockSpec(a.shape, lambda i, j: (0, 0))
    return pl.pallas_call(
        functools.partial(_ffn_body, alpha=alpha, nf=nf), grid=(M // tm, nf),
        in_specs=[pl.BlockSpec((tm, D_MODEL), lambda i, j: (i, 0)),
                  pl.BlockSpec((D_MODEL, tf), lambda i, j: (0, j)),
                  pl.BlockSpec((D_MODEL, tf), lambda i, j: (0, j + nf)),
                  pl.BlockSpec((tf, D_MODEL), lambda i, j: (j, 0)), const(g), const(be)],
        out_specs=pl.BlockSpec((tm, D_MODEL), lambda i, j: (i, 0)),
        out_shape=jax.ShapeDtypeStruct((M, D_MODEL), F32),
        scratch_shapes=[pltpu.VMEM((tm, D_MODEL), F32)],
        compiler_params=_cparams(("parallel", "arbitrary"), big=True), name="ffn_ln")(x, w13, w13, w2, g, be)


def _moe_body(x_ref, rwh_ref, rwl_ref, rb_ref, low_ref, wa_ref, wb_ref, w2_ref, g_ref, be_ref, o_ref,
              acc, comb, ind_c, pos_c, ind_r, pos_r, cnt_s, *, alpha, n_exp, chunk):
    e = pl.program_id(1)
    x = x_ref[...]
    xh = x.astype(BF16)
    tm = x.shape[0]
    lane = lax.broadcasted_iota(jnp.int32, (tm, LANES), 1)

    @pl.when(e == 0)
    def _():
        xl = (x - xh.astype(F32)).astype(BF16)
        logits = (_dot(xh, rwh_ref[...]) + _dot(xh, rwl_ref[...])) + _dot(xl, rwh_ref[...]) + rb_ref[...]
        valid = lane < n_exp
        lanef = lane.astype(F32)
        logits = jnp.where(valid, logits, NEG)
        ex = jnp.where(valid, jnp.exp(logits - jnp.max(logits, axis=-1, keepdims=True)), 0.0)
        p = ex / jnp.sum(ex, axis=-1, keepdims=True)
        p = jnp.where(valid, p, -1.0)
        m1 = jnp.max(p, axis=-1, keepdims=True)
        hit1 = lanef == jnp.min(jnp.where(p == m1, lanef, 1.0e9), axis=-1, keepdims=True)
        p2 = jnp.where(hit1, -1.0, p)
        m2 = jnp.max(p2, axis=-1, keepdims=True)
        hit2 = lanef == jnp.min(jnp.where(p2 == m2, lanef, 1.0e9), axis=-1, keepdims=True)
        den = m1 + m2
        comb[...] = jnp.where(hit1, m1 / den, 0.0) + jnp.where(hit2, m2 / den, 0.0)
        acc[...] = jnp.zeros_like(acc)
        ind = jnp.where(hit1 | hit2, 1.0, 0.0)
        pos = _dot(low_ref[...], ind.astype(BF16))
        ind_c[...] = ind
        pos_c[...] = pos
        ind_r[...] = ind.T
        pos_r[...] = pos.T
        for k in range(n_exp):
            cnt_s[k] = jnp.sum(ind[:, k:k + 1]).astype(jnp.int32)

    sel_lane = lane == e
    w_e = jnp.sum(jnp.where(sel_lane, comb[...], 0.0), axis=-1, keepdims=True)
    used_c = jnp.sum(jnp.where(sel_lane, ind_c[...], 0.0), axis=-1, keepdims=True) > 0.5
    slot_c = jnp.sum(jnp.where(sel_lane, pos_c[...], 0.0), axis=-1, keepdims=True)
    used_r = ind_r[pl.ds(e, 1), :] > 0.5
    slot_r = pos_r[pl.ds(e, 1), :]
    row_id = lax.broadcasted_iota(jnp.int32, (chunk, tm), 0).astype(F32)
    col_id = lax.broadcasted_iota(jnp.int32, (tm, chunk), 1).astype(F32)

    def run_chunk(c, _):
        base = (c * chunk).astype(F32)
        gather = jnp.where(used_r & (slot_r == base + row_id), 1.0, 0.0).astype(BF16)
        xs = _dot(gather, xh).astype(BF16)
        a = _dot(xs, wa_ref[...])
        b = _dot(xs, wb_ref[...])
        o = _dot(((a * jax.nn.sigmoid(a)) * b).astype(BF16), w2_ref[...])
        scatter = jnp.where(used_c & (slot_c == base + col_id), 1.0, 0.0).astype(BF16)
        oh, ol = _split2(o)
        acc[...] += w_e * (_dot(scatter, oh) + _dot(scatter, ol))
        return 0

    lax.fori_loop(0, (cnt_s[e] + chunk - 1) // chunk, run_chunk, 0)

    @pl.when(e == n_exp - 1)
    def _():
        o_ref[...] = _layer_norm(alpha * x + acc[...], g_ref[...], be_ref[...])


def _moe_ln(x, rwh, rwl, rb, w13, w2, g, be, alpha, tm=512):
    M = x.shape[0]
    tm = min(tm, M)
    chunk = min(LANES, tm)
    n_exp, de, _ = w2.shape
    low = jnp.asarray(np.arange(tm)[:, None] > np.arange(tm)[None, :], dtype=BF16)
    const = lambda a: pl.BlockSpec(a.shape, lambda i, e: (0, 0))
    return pl.pallas_call(
        functools.partial(_moe_body, alpha=alpha, n_exp=n_exp, chunk=chunk), grid=(M // tm, n_exp),
        in_specs=[pl.BlockSpec((tm, D_MODEL), lambda i, e: (i, 0)), const(rwh), const(rwl), const(rb), const(low),
                  pl.BlockSpec((None, D_MODEL, de), lambda i, e: (e, 0, 0)),
                  pl.BlockSpec((None, D_MODEL, de), lambda i, e: (e, 0, 1)),
                  pl.BlockSpec((None, de, D_MODEL), lambda i, e: (e, 0, 0)), const(g), const(be)],
        out_specs=pl.BlockSpec((tm, D_MODEL), lambda i, e: (i, 0)),
        out_shape=jax.ShapeDtypeStruct((M, D_MODEL), F32),
        scratch_shapes=[pltpu.VMEM((tm, D_MODEL), F32), pltpu.VMEM((tm, LANES), F32), pltpu.VMEM((tm, LANES), F32),
                        pltpu.VMEM((tm, LANES), F32), pltpu.VMEM((LANES, tm), F32), pltpu.VMEM((LANES, tm), F32),
                        pltpu.SMEM((n_exp,), jnp.int32)],
        compiler_params=_cparams(("parallel", "arbitrary"), big=True), name="moe_ln")(
            x, rwh, rwl, rb, low, w13, w13, w2, g, be)


def _row_consts(rows, Td, past):
    row = lax.broadcasted_iota(jnp.int32, (rows, 1), 0)
    head = row // Td
    qpos = past + (row - head * Td)
    slope = jnp.zeros((rows, 1), F32)
    for h in range(NSA_HEADS):
        slope = jnp.where(head == h, SLOPES[h], slope)
    return qpos, slope


def _nsa_s_cmp_body(qz_ref, ak_ref, bk_ref, av_ref, bv_ref, ovl_ref, hs_ref, ocmp_ref, sel_ref, *, Td, past, n_slc):
    rows = qz_ref.shape[0]
    NC = ak_ref.shape[0]
    qpos, slope = _row_consts(rows, Td, past)
    ck = (ak_ref[...] + pltpu.roll(bk_ref[...], NC - 1, 0)).astype(BF16)
    cv = (av_ref[...] + pltpu.roll(bv_ref[...], NC - 1, 0)).astype(BF16)
    qz = (qz_ref[...] * (HEAD_DIM ** -0.5)).astype(BF16)
    cend = CMP_STRIDE * lax.broadcasted_iota(jnp.int32, (1, NC), 1) + (CMP_BLOCK - 1)
    dist = qpos - cend
    p = _masked_softmax(_dot_nt(qz, ck) - slope * dist.astype(F32), dist >= 0)
    ocmp_ref[...] = _dot(p.astype(BF16), cv)
    ph, pl_ = _split2(p)
    imp_h = _dot(ph, ovl_ref[...]) + _dot(pl_, ovl_ref[...])
    i1, i2, i3 = _split3(imp_h)
    hs = hs_ref[...]
    imp = (_dot(hs, i1) + _dot(hs, i2)) + _dot(hs, i3)
    NSP = imp.shape[1]
    lane = lax.broadcasted_iota(jnp.int32, (rows, NSP), 1)
    cur = qpos >> 6
    valid = (lane * SLC_BLOCK <= qpos) & (lane < n_slc)
    forced = (lane == 0) | (lane == cur) | (lane == cur - 1)
    score = jnp.where(valid, jnp.where(forced, FORCE_SCORE, imp), jnp.where(lane < n_slc, -1.0, -2.0))
    sel_ref[...] = _topk_mask(score, lane.astype(F32), SLC_TOP)


def _nsa_s_cmp(qz, ak, bk, av, bv, ovl, hs, Td, past, n_slc):
    Bd, rows, _ = qz.shape
    NC = ak.shape[1]
    NSP = ovl.shape[1]
    cspec = pl.BlockSpec((None, NC, KV_W), lambda b: (b, 0, 0))
    return pl.pallas_call(
        functools.partial(_nsa_s_cmp_body, Td=Td, past=past, n_slc=n_slc), grid=(Bd,),
        in_specs=[pl.BlockSpec((None, rows, KV_W), lambda b: (b, 0, 0)), cspec, cspec, cspec, cspec,
                  pl.BlockSpec(ovl.shape, lambda b: (0, 0)), pl.BlockSpec(hs.shape, lambda b: (0, 0))],
        out_specs=[pl.BlockSpec((None, rows, KV_W), lambda b: (b, 0, 0)),
                   pl.BlockSpec((None, rows, NSP), lambda b: (b, 0, 0))],
        out_shape=[jax.ShapeDtypeStruct((Bd, rows, KV_W), F32), jax.ShapeDtypeStruct((Bd, rows, NSP), F32)],
        compiler_params=_cparams(("parallel",), big=True), name="nsa_sample_cmp")(qz, ak, bk, av, bv, ovl, hs)


def _nsa_s_slc_body(pt_ref, plist_ref, nneed_ref, *refs, P, Td, past, n_steps):
    kp = refs[:P]
    vp = refs[P:2 * P]
    (qz_ref, g_ref, ocmp_ref, sel_ref, nk_ref, nv_ref, wk_ref, wv_ref, o_ref, m_s, l_s, acc_s) = refs[2 * P:]
    b_id = pl.program_id(0)
    s_id = pl.program_id(1)
    rows = qz_ref.shape[0]
    NSP = sel_ref.shape[1]
    qpos, slope = _row_consts(rows, Td, past)
    qz = (qz_ref[...] * (HEAD_DIM ** -0.5)).astype(BF16)
    sel_b = sel_ref[...].astype(BF16)

    @pl.when(s_id == 0)
    def _():
        m_s[...] = jnp.full_like(m_s, NEG)
        l_s[...] = jnp.zeros_like(l_s)
        acc_s[...] = jnp.zeros_like(acc_s)

    def tile(kt, vt, kpos, extra):
        nk = kt.shape[1]
        expand = ((kpos >> 6) == lax.broadcasted_iota(jnp.int32, (NSP, nk), 0)).astype(BF16)
        dist = qpos - kpos
        mask = (_dot(sel_b, expand) > 0.5) & (dist >= 0)
        if extra is not None:
            mask = mask & extra
        s = jnp.where(mask, _dot(qz, kt) - slope * dist.astype(F32), NEG)
        m_old = m_s[...]
        m_new = jnp.maximum(m_old, jnp.max(s, axis=-1, keepdims=True))
        alpha = jnp.exp(m_old - m_new)
        e = jnp.where(mask, jnp.exp(s - m_new), 0.0)
        l_s[...] = alpha * l_s[...] + jnp.sum(e, axis=-1, keepdims=True)
        acc_s[...] = alpha * acc_s[...] + _dot_nt(e.astype(BF16), vt)
        m_s[...] = m_new

    nneed = nneed_ref[b_id]

    @pl.when(s_id * P < nneed)
    def _():
        kt = jnp.concatenate([r[...] for r in kp], axis=1).astype(BF16)
        vt = jnp.concatenate([r[...] for r in vp], axis=1).astype(BF16)
        lane = lax.broadcasted_iota(jnp.int32, (1, LANES), 1)
        kpos = jnp.concatenate([plist_ref[b_id, s_id * P + i] * LANES + lane for i in range(P)], axis=1)
        live = jnp.concatenate([jnp.full((1, LANES), s_id * P + i, jnp.int32) for i in range(P)], axis=1) < nneed
        tile(kt, vt, kpos, live)

    @pl.when(s_id == n_steps - 1)
    def _():
        nl = lax.broadcasted_iota(jnp.int32, (1, LANES), 1)
        tile(nk_ref[...].astype(BF16), nv_ref[...].astype(BF16), past + nl, nl < Td)
        o_slc = acc_s[...] / jnp.maximum(l_s[...], TINY)
        nw = wk_ref.shape[1]
        wl = lax.broadcasted_iota(jnp.int32, (1, nw), 1)
        dist = qpos - (past - (nw - LANES) + wl)
        okw = (wl < nw - LANES + Td) & (dist >= 0) & (dist <= WINDOW)
        pw = _masked_softmax(_dot(qz, wk_ref[...].astype(BF16)) - slope * dist.astype(F32), okw)
        o_win = _dot_nt(pw.astype(BF16), wv_ref[...].astype(BF16))
        gate = jax.nn.sigmoid(g_ref[...])
        o_ref[...] = gate[:, 0:1] * ocmp_ref[...] + gate[:, 1:2] * o_slc + gate[:, 2:3] * o_win


def _nsa_s_slc(page_table, layer, skc, svc, qz, graw, ocmp, sel, nks, nvs, wkx, wvx, Td, past, P):
    Bd, NP = page_table.shape
    rows = qz.shape[1]
    n_steps = NP // P
    blocks_per_page = LANES // SLC_BLOCK
    need_blk = jnp.any(sel[:, :, :NP * blocks_per_page] > 0.5, axis=1)
    need_page = jnp.any(need_blk.reshape(Bd, NP, blocks_per_page), axis=2)
    nneed = jnp.sum(need_page, axis=1).astype(jnp.int32)
    order = jnp.argsort(jnp.logical_not(need_page), axis=1, stable=True).astype(jnp.int32)
    slot = jnp.minimum(jnp.arange(NP, dtype=jnp.int32)[None, :], nneed[:, None] - 1)
    plist = jnp.take_along_axis(order, slot, axis=1)
    pspec = [pl.BlockSpec((None, None, KV_W, LANES),
                          lambda b, s, pt, pls, nn, i=i: (layer, pt[b, pls[b, s * P + i]], 0, 0)) for i in range(P)]
    per_b = lambda a: pl.BlockSpec((None,) + a.shape[1:], lambda b, s, pt, pls, nn: (b, 0, 0))
    grid_spec = pltpu.PrefetchScalarGridSpec(
        num_scalar_prefetch=3, grid=(Bd, n_steps),
        in_specs=pspec + pspec + [per_b(a) for a in (qz, graw, ocmp, sel, nks, nvs, wkx, wvx)],
        out_specs=pl.BlockSpec((None, rows, KV_W), lambda b, s, pt, pls, nn: (b, 0, 0)),
        scratch_shapes=[pltpu.VMEM((rows, 1), F32), pltpu.VMEM((rows, 1), F32), pltpu.VMEM((rows, KV_W), F32)])
    return pl.pallas_call(
        functools.partial(_nsa_s_slc_body, P=P, Td=Td, past=past, n_steps=n_steps), grid_spec=grid_spec,
        out_shape=jax.ShapeDtypeStruct((Bd, rows, KV_W), F32),
        compiler_params=_cparams(("parallel", "arbitrary"), big=True), name="nsa_sample_slc")(
            page_table, plist, nneed, *([skc] * P), *([svc] * P), qz, graw, ocmp, sel, nks, nvs, wkx, wvx)


def _sb_s_body(pt_ref, need_ref, *refs, P, Td, n_steps, first):
    kp = refs[:P]
    vp = refs[P:2 * P]
    q_ref, in0_ref, in1_ref, u_ref, co_ref, o_ref, carry_s, acc_s, alive_s = refs[2 * P:]
    s_id = pl.program_id(1)
    rows = q_ref.shape[0]
    qb = (q_ref[...] * (HEAD_DIM ** -0.5)).astype(BF16)
    u = u_ref[...]

    @pl.when(s_id == 0)
    def _():
        if first:
            row = lax.broadcasted_iota(jnp.int32, (rows, LANES), 0)
            lane = lax.broadcasted_iota(jnp.int32, (rows, LANES), 1)
            newer = lane < (row - (row // Td) * Td)
            carry, acc = _sb_block(qb, in0_ref[...].astype(BF16), in1_ref[...].astype(BF16), u,
                                   jnp.zeros((rows, 1), F32), jnp.zeros((rows, SB_W), F32), newer)
        else:
            carry, acc = in0_ref[:, 0:1], in1_ref[...]
        carry_s[...] = carry
        acc_s[...] = acc
        alive_s[0] = (jnp.max(carry) > SB_DEAD_LOG).astype(jnp.int32)

    @pl.when(alive_s[0] > 0)
    def _():
        carry, acc = carry_s[...], acc_s[...]
        order = range(P - 1, -1, -1)
        ss = [_dot(qb, kp[i][...].astype(BF16)) for i in order]
        sps = [_softplus(s) for s in ss]
        cats = [jnp.concatenate(_split2(-sp), axis=0) for sp in sps]
        afters = [_dot(c, u) for c in cats]
        ws = []
        for s, sp, r in zip(ss, sps, afters):
            ws.append(jnp.exp((s - sp) + (r[:rows] + r[rows:]) + carry).astype(BF16))
            carry = carry - jnp.sum(sp, axis=-1, keepdims=True)
        for w, i in zip(ws, order):
            acc = acc + _dot_nt(w, vp[i][...].astype(BF16))
        carry_s[...] = carry
        acc_s[...] = acc
        alive_s[0] = (jnp.max(carry) > SB_DEAD_LOG).astype(jnp.int32)

    @pl.when(s_id == n_steps - 1)
    def _():
        co_ref[...] = jnp.broadcast_to(carry_s[...], co_ref.shape)
        o_ref[...] = acc_s[...]


def _sb_sample_part(page_table, need, layer, kc, vc, qbd, in0, in1, u, Td, P, first, s0, n_steps):
    Bd, NP = page_table.shape
    rows = qbd.shape[1]
    pspec = [pl.BlockSpec((None, None, SB_W, LANES),
                          lambda b, s, pt, nd, i=i: (layer, pt[b, NP - (s0 + s * nd[b] + 1) * P + i], 0, 0))
             for i in range(P)]
    per_b = lambda a: pl.BlockSpec((None,) + a.shape[1:], lambda b, s, pt, nd: (b, 0, 0))
    grid_spec = pltpu.PrefetchScalarGridSpec(
        num_scalar_prefetch=2, grid=(Bd, n_steps),
        in_specs=pspec + pspec + [per_b(qbd), per_b(in0), per_b(in1),
                                  pl.BlockSpec(u.shape, lambda b, s, pt, nd: (0, 0))],
        out_specs=[pl.BlockSpec((None, rows, LANES), lambda b, s, pt, nd: (b, 0, 0)),
                   pl.BlockSpec((None, rows, SB_W), lambda b, s, pt, nd: (b, 0, 0))],
        scratch_shapes=[pltpu.VMEM((rows, 1), F32), pltpu.VMEM((rows, SB_W), F32), pltpu.SMEM((1,), jnp.int32)])
    return pl.pallas_call(
        functools.partial(_sb_s_body, P=P, Td=Td, n_steps=n_steps, first=first), grid_spec=grid_spec,
        out_shape=[jax.ShapeDtypeStruct((Bd, rows, LANES), F32), jax.ShapeDtypeStruct((Bd, rows, SB_W), F32)],
        compiler_params=_cparams(("parallel", "arbitrary"), big=True), name="sb_sample")(
            page_table, need, *([kc] * P), *([vc] * P), qbd, in0, in1, u)


def _sb_sample(page_table, layer, kc, vc, qbd, nk, nv, u, Td, P):
    Bd, NP = page_table.shape
    carry, acc = _sb_sample_part(page_table, jnp.ones((Bd,), jnp.int32), layer, kc, vc, qbd, nk, nv, u, Td, P,
                                 True, 0, 1)
    if NP // P == 1:
        return acc
    need = (jnp.max(carry[:, :, 0], axis=1) > SB_DEAD_LOG).astype(jnp.int32)
    return _sb_sample_part(page_table, need, layer, kc, vc, qbd, carry, acc, u, Td, P, False, 1, NP // P - 1)[1]


def _cmp_weights(w, pe):
    ratio = CMP_BLOCK // CMP_STRIDE
    w5 = w.reshape(2, ratio, CMP_STRIDE, HEAD_DIM, HEAD_DIM)
    eye = jnp.eye(NSA_KV, dtype=w.dtype)
    wbd = jnp.einsum('krcde,gh->krcgdhe', w5, eye).reshape(2, ratio, CMP_STRIDE * KV_W, KV_W)
    pe2 = jnp.broadcast_to(pe.reshape(ratio, CMP_STRIDE, 1, HEAD_DIM), (ratio, CMP_STRIDE, NSA_KV, HEAD_DIM))
    return wbd.astype(BF16), jnp.swapaxes(wbd, 2, 3).astype(BF16), pe2.reshape(ratio, 1, CMP_STRIDE * KV_W)


def _overlap(n_cmp_pad, n_slc_pad):
    n = np.arange(n_cmp_pad)[:, None] * CMP_STRIDE
    j = np.arange(n_slc_pad)[None, :] * SLC_BLOCK
    return jnp.asarray((n < j + SLC_BLOCK) & (n + CMP_BLOCK > j), dtype=BF16)


def _suffix_matrix(n):
    return jnp.asarray(np.arange(n)[:, None] > np.arange(n)[None, :], dtype=BF16)


def _pair_suffix_matrix(n):
    u = (np.arange(n)[:, None] > np.arange(n)[None, :]).astype(np.float32)
    one, z = np.ones((n, n), np.float32), np.zeros((n, n), np.float32)
    return jnp.asarray(np.block([[u, z, one, z], [z, u, z, one]]), dtype=BF16)


def _feature_major_state(xt, heads):
    d, b, _, t = xt.shape
    return xt.reshape(d, b, heads, HEAD_DIM, t).transpose(0, 1, 4, 2, 3)


def kernel(x_prompt, x_sample, mem_prompt, cache_nsa_cmp_k, cache_nsa_cmp_v, cache_nsa_slc_k, cache_nsa_slc_v, cache_nsa_win_k, cache_nsa_win_v, cache_sb_k, cache_sb_v, cache_mem_k, cache_mem_v, state_lru_h, state_lru_conv, page_table, w_in, nsa_cmp_w, nsa_cmp_pe, lru_conv_w, lru_conv_b, lru_wa, lru_ba, lru_wx, lru_bx, lru_lambda, w_mem_kv, w_branch, w_out, ln1_g, ln1_b, ln2_g, ln2_b, ffn_w13, ffn_w2, moe_router_w, moe_router_b, moe_w13, moe_w2):
    depth = w_in.shape[0]
    B, T, _ = x_prompt.shape
    Bd, Td, _ = x_sample.shape
    NP = page_table.shape[1]
    page = cache_nsa_cmp_k.shape[2]
    past = NP * page
    win_buf = cache_nsa_win_k.shape[2]
    n_mem = mem_prompt.shape[1]
    alpha = (2.0 * depth) ** 0.25
    assert page == LANES and T % 512 == 0 and T // SLC_BLOCK >= SLC_TOP and T // SLC_BLOCK <= LANES
    assert win_buf == WINDOW and Td < CMP_STRIDE and NP % 8 == 0
    P = 8
    rows = NSA_HEADS * Td

    fm = lambda c: c.transpose(0, 1, 3, 4, 2).reshape(c.shape[0], c.shape[1], c.shape[3] * c.shape[4], c.shape[2])
    slc_kc, slc_vc, sb_kc, sb_vc = fm(cache_nsa_slc_k), fm(cache_nsa_slc_v), fm(cache_sb_k), fm(cache_sb_v)
    win_kc, win_vc = fm(cache_nsa_win_k), fm(cache_nsa_win_v)
    cmp_kc, cmp_vc = fm(cache_nsa_cmp_k), fm(cache_nsa_cmp_v)
    P_cmp = 16 if NP % 16 == 0 else 8

    n_cmp_p = T // CMP_STRIDE
    ovl_p = _overlap(n_cmp_p, LANES)
    n_slc_s = -(-(past + Td) // SLC_BLOCK)
    nsp_s = -(-n_slc_s // LANES) * LANES
    n_cmp_s = past // CMP_STRIDE
    ovl_s = _overlap(n_cmp_s, nsp_s)
    u_blk = _suffix_matrix(LANES)
    u_pair = _pair_suffix_matrix(LANES)
    hs = jnp.asarray((np.arange(rows)[:, None] // (NSA_GROUP * Td) == np.arange(rows)[None, :] // (NSA_GROUP * Td))
                     & (np.arange(rows)[:, None] % Td == np.arange(rows)[None, :] % Td), dtype=BF16)

    xp = x_prompt
    xs = x_sample.reshape(1, Bd * Td, D_MODEL)
    mem2 = mem_prompt.reshape(1, B * n_mem, D_MODEL)
    p_layers, s_layers = [], []
    for l in range(depth):
        wt = jnp.swapaxes(w_in[l], 0, 1)
        c = np.cumsum([0, 512, 128, 128, 128, 128, 128, 128, 24, 512, 512, 512, 512, 512, 512, 4096])
        seg = lambda a, b: wt[c[a]:c[b]]
        ng_w = jnp.pad(seg(7, 8), ((0, LANES - 24), (0, 0)))
        wn = jnp.concatenate([seg(0, 1), ng_w, seg(8, 10), seg(10, 11), seg(13, 14), seg(1, 3)], axis=0)
        wn = jnp.swapaxes(wn, 0, 1).astype(BF16)
        wtr = jnp.concatenate([seg(1, 7), seg(11, 13)], axis=0).astype(BF16)
        w_all = jnp.swapaxes(jnp.concatenate([seg(0, 7), ng_w, seg(8, 14)], axis=0), 0, 1).astype(BF16)
        wmg = jnp.swapaxes(seg(14, 15), 0, 1).astype(BF16)
        cw_tok, cw_feat, pe2 = _cmp_weights(nsa_cmp_w[l], nsa_cmp_pe[l])
        eye8 = jnp.eye(LRU_BLOCKS, dtype=F32)
        bd = lambda w: jnp.einsum('ncd,nm->ncmd', w, eye8).reshape(D_RNN, D_RNN)
        wbd = jnp.concatenate([bd(lru_wa[l]), bd(lru_wx[l])], axis=1).astype(BF16)
        gbias = jnp.concatenate([lru_ba[l].reshape(1, D_RNN), lru_bx[l].reshape(1, D_RNN)], axis=1)
        lam = lru_lambda[l].reshape(1, D_RNN)
        conv_w, conv_b = lru_conv_w[l], lru_conv_b[l].reshape(1, D_RNN)
        wmem = w_mem_kv[l].astype(BF16)
        wbr, wo = w_branch[l].astype(BF16), w_out[l].astype(BF16)
        g1, b1 = ln1_g[l].reshape(1, D_MODEL), ln1_b[l].reshape(1, D_MODEL)
        g2, b2 = ln2_g[l].reshape(1, D_MODEL), ln2_b[l].reshape(1, D_MODEL)

        ncols = [(0, 512), (512, 128), (640, 512), (1152, 512), (1664, 512), (2176, 512), (2688, 128), (2816, 128)]
        tcols = [(0, 128, ("full",)), (128, 128, ("full",)), (256, 128, ("full", "blk")), (384, 128, ("full", "blk")),
                 (512, 128, ("full", "blk")), (640, 128, ("full", "blk")), (768, 512, ("full", "blk")),
                 (1280, 512, ("full", "blk"))]
        (nq, ng, lx, lg, sq, mq, ck_tok, cv_tok, ck_t, cv_t, sk_t, sk_b, sv_t, sv_b, wk_t, wk_b, wv_t, wv_b,
         sbk_t, sbk_b, sbv_t, sbv_b) = _proj(xp, wn, ncols, wtr, tcols)
        ckc, cvc = _compress(ck_tok.reshape(B, n_cmp_p, CMP_STRIDE * KV_W), cv_tok.reshape(B, n_cmp_p, CMP_STRIDE * KV_W),
                             cw_feat[0], cw_feat[1], pe2)
        o_nsa = _nsa_prompt(nq, ng, ckc, cvc, sk_b, sv_b, wk_b, wv_b, ovl_p)
        o_lru, h_new, conv_new = _rglru(lx, lg, jnp.zeros((B, CONV_W - 1, D_RNN), F32), jnp.zeros((B, 1, D_RNN), F32),
                                        conv_w, conv_b, wbd, gbias, lam)
        o_sb = _sb_prompt(sq, sbk_b, sbv_b, u_pair)
        mk, mv = _proj(mem2, wmem, [(0, 512), (512, 512)])
        mk, mv = mk.reshape(B, n_mem, 512), mv.reshape(B, n_mem, 512)
        o_mem = _mem_attn(mq, mk, mv)
        M = B * T
        x1 = _merge_ln(xp.reshape(M, D_MODEL), [o.reshape(M, 512) for o in (o_nsa, o_lru, o_sb, o_mem)],
                       wmg, wbr, wo, g1, b1, alpha)
        p_layers.append((ck_t, cv_t, sk_t, sv_t, wk_t[:, :, T - min(WINDOW, T):], wv_t[:, :, T - min(WINDOW, T):],
                         sbk_t, sbv_t, h_new.reshape(B, D_RNN), conv_new,
                         mk.reshape(B, n_mem, MEM_HEADS, MEM_HEAD_DIM), mv.reshape(B, n_mem, MEM_HEADS, MEM_HEAD_DIM)))

        ms = Bd * Td
        scols = [(0, 512), (512, 128), (640, 128), (768, 128), (896, 128), (1024, 128), (1152, 128), (1280, 128),
                 (1408, 512), (1920, 512), (2432, 512), (2944, 512), (3456, 512), (3968, 512)]
        (nq_s, ck_s, cv_s, sk_s, sv_s, wk_s, wv_s, ng_s, lx_s, lg_s, sq_s, sbk_s, sbv_s, mq_s) = [
            a.reshape(Bd, Td, a.shape[-1]) for a in _proj(xs, w_all, scols)]
        ak, bk = _compress_paged(cmp_kc, l, page_table, cw_tok[0], pe2, P_cmp)
        av, bv = _compress_paged(cmp_vc, l, page_table, cw_tok[1], pe2, P_cmp)
        q8 = nq_s.reshape(Bd, Td, NSA_HEADS, HEAD_DIM).transpose(0, 2, 1, 3)
        gsel = jnp.asarray(np.arange(NSA_HEADS)[:, None] // NSA_GROUP == np.arange(NSA_KV)[None, :], dtype=F32)
        qz = (q8[:, :, :, None, :] * gsel[None, :, None, :, None]).reshape(Bd, rows, KV_W)
        hsel = jnp.eye(SB_HEADS, dtype=F32)
        s8 = sq_s.reshape(Bd, Td, SB_HEADS, HEAD_DIM).transpose(0, 2, 1, 3)
        qbd = (s8[:, :, :, None, :] * hsel[None, :, None, :, None]).reshape(Bd, rows, SB_W)
        graw = jnp.pad(ng_s[:, :, :24].reshape(Bd, Td, NSA_HEADS, 3).transpose(0, 2, 1, 3).reshape(Bd, rows, 3),
                       ((0, 0), (0, 0), (0, LANES - 3)))
        new_fm = lambda a: jnp.pad(jnp.swapaxes(a, 1, 2), ((0, 0), (0, 0), (0, LANES - Td)))
        ocmp, sel = _nsa_s_cmp(qz, ak, bk, av, bv, ovl_s, hs, Td, past, n_slc_s)
        wkx = jnp.concatenate([win_kc[l], new_fm(wk_s)], axis=2)
        wvx = jnp.concatenate([win_vc[l], new_fm(wv_s)], axis=2)
        o_rows = _nsa_s_slc(page_table, l, slc_kc, slc_vc, qz, graw, ocmp, sel, new_fm(sk_s), new_fm(sv_s),
                            wkx, wvx, Td, past, P)
        o6 = o_rows.reshape(Bd, NSA_KV, NSA_GROUP, Td, NSA_KV, HEAD_DIM)
        o_nsa_s = jnp.stack([o6[:, g, :, :, g, :] for g in range(NSA_KV)], axis=1)
        o_nsa_s = o_nsa_s.reshape(Bd, NSA_HEADS, Td, HEAD_DIM).transpose(0, 2, 1, 3).reshape(Bd, Td, NSA_HEADS * HEAD_DIM)
        o_lru_s, h_new_s, conv_new_s = _rglru(lx_s, lg_s, state_lru_conv[l], state_lru_h[l].reshape(Bd, 1, D_RNN),
                                              conv_w, conv_b, wbd, gbias, lam)
        sb_rows = _sb_sample(page_table, l, sb_kc, sb_vc, qbd, new_fm(sbk_s), new_fm(sbv_s), u_blk, Td, P)
        sb5 = sb_rows.reshape(Bd, SB_HEADS, Td, SB_HEADS, HEAD_DIM)
        o_sb_s = jnp.stack([sb5[:, h, :, h, :] for h in range(SB_HEADS)], axis=2).reshape(Bd, Td, SB_W)
        o_mem_s = _mem_attn(mq_s, cache_mem_k[l].reshape(Bd, n_mem, 512), cache_mem_v[l].reshape(Bd, n_mem, 512))
        x1s = _merge_ln(xs.reshape(ms, D_MODEL), [o.reshape(ms, 512) for o in (o_nsa_s, o_lru_s, o_sb_s, o_mem_s)],
                        wmg, wbr, wo, g1, b1, alpha)
        kv4 = lambda a: a.reshape(Bd, Td, NSA_KV, HEAD_DIM)
        s_layers.append((kv4(ck_s), kv4(cv_s), kv4(sk_s), kv4(sv_s), kv4(wk_s), kv4(wv_s),
                         sbk_s.reshape(Bd, Td, SB_HEADS, HEAD_DIM), sbv_s.reshape(Bd, Td, SB_HEADS, HEAD_DIM),
                         h_new_s.reshape(Bd, D_RNN), conv_new_s))

        if l % 2 == 0:
            w13, w2 = ffn_w13[l // 2].astype(BF16), ffn_w2[l // 2].astype(BF16)
            x2 = _ffn_ln(x1, w13, w2, g2, b2, alpha)
            x2s = _ffn_ln(x1s, w13, w2, g2, b2, alpha)
        else:
            rw = jnp.pad(moe_router_w[l // 2], ((0, 0), (0, LANES - N_EXP)))
            rwh = rw.astype(BF16)
            rwl = (rw - rwh.astype(F32)).astype(BF16)
            rb = jnp.pad(moe_router_b[l // 2].reshape(1, N_EXP), ((0, 0), (0, LANES - N_EXP)))
            w13, w2 = moe_w13[l // 2].astype(BF16), moe_w2[l // 2].astype(BF16)
            x2 = _moe_ln(x1, rwh, rwl, rb, w13, w2, g2, b2, alpha)
            x2s = _moe_ln(x1s, rwh, rwl, rb, w13, w2, g2, b2, alpha)
        xp = x2.reshape(B, T, D_MODEL)
        xs = x2s.reshape(1, ms, D_MODEL)

    pst = [jnp.stack(r, axis=0) for r in zip(*p_layers)]
    prompt_state = ([_feature_major_state(a, NSA_KV) for a in pst[:6]] + [_feature_major_state(a, SB_HEADS) for a in pst[6:8]]
                    + pst[8:])
    sample_state = [jnp.stack(r, axis=0) for r in zip(*s_layers)]
    return (xp, xs.reshape(Bd, Td, D_MODEL), *prompt_state, *sample_state)
```

```python
import functools

import numpy as np
import jax
import jax.numpy as jnp
from jax import lax
from jax.experimental import pallas as pl
from jax.experimental.pallas import tpu as pltpu

F32 = jnp.float32
BF16 = jnp.bfloat16

D_MODEL = 1024
HEAD_DIM = 64
NSA_HEADS = 8
NSA_KV = 2
NSA_GROUP = NSA_HEADS // NSA_KV
CMP_STRIDE = 16
CMP_BLOCK = 32
SLC_BLOCK = 64
SLC_TOP = 16
WINDOW = 512
FORCE_SCORE = 1.0e4
SB_HEADS = 8
D_RNN = 512
LRU_BLOCKS = 8
CONV_W = 4
LRU_C = 8.0
MEM_HEADS = 4
MEM_HEAD_DIM = 128
N_EXP = 8
LN_EPS = 1e-5
NEG = -1.0e30
TINY = 1.0e-30
SB_DEAD_LOG = -106.0
KV_W = NSA_KV * HEAD_DIM
SB_W = SB_HEADS * HEAD_DIM
LANES = 128
VMEM_LIMIT = 56 * 1024 * 1024
SLOPES = tuple(float(2.0 ** (-8.0 * (h + 1) / NSA_HEADS)) for h in range(NSA_HEADS))


def _cparams(sem, big=False):
    return pltpu.CompilerParams(dimension_semantics=sem, vmem_limit_bytes=VMEM_LIMIT if big else None)


def _dot(a, b):
    return jnp.dot(a, b, preferred_element_type=F32)


def _dot_nt(a, b):
    return lax.dot_general(a, b, (((1,), (1,)), ((), ())), preferred_element_type=F32)


def _split2(x):
    hi = x.astype(BF16)
    lo = (x - hi.astype(F32)).astype(BF16)
    return hi, lo


def _split3(x):
    hi = x.astype(BF16)
    r = x - hi.astype(F32)
    mid = r.astype(BF16)
    lo = (r - mid.astype(F32)).astype(BF16)
    return hi, mid, lo


def _softplus(x):
    return jnp.maximum(x, 0.0) + jnp.log1p(jnp.exp(-jnp.abs(x)))


def _layer_norm(z, g, b):
    mu = jnp.mean(z, axis=-1, keepdims=True)
    zc = z - mu
    var = jnp.mean(zc * zc, axis=-1, keepdims=True)
    return zc * lax.rsqrt(var + LN_EPS) * g + b


def _masked_softmax(s, mask):
    s = jnp.where(mask, s, NEG)
    m = jnp.max(s, axis=-1, keepdims=True)
    e = jnp.where(mask, jnp.exp(s - m), 0.0)
    return e / jnp.maximum(jnp.sum(e, axis=-1, keepdims=True), TINY)


def _topk_mask(score, lanef, k):
    sel = jnp.zeros_like(score)
    s = score
    for _ in range(k):
        m = jnp.max(s, axis=-1, keepdims=True)
        jm = jnp.min(jnp.where(s == m, lanef, 1.0e9), axis=-1, keepdims=True)
        hit = lanef == jm
        sel = jnp.where(hit, 1.0, sel)
        s = jnp.where(hit, -3.0, s)
    return sel


def _topk_rank(score_t, k):
    R, N = score_t.shape
    nv = R // 8
    groups = [score_t[8 * v:8 * v + 8] for v in range(nv)]
    sub = lax.broadcasted_iota(jnp.int32, (8, N), 0)
    ranks = [jnp.zeros((8, N), F32) for _ in range(nv)]
    for jp in range(R):
        sj = score_t[jp:jp + 1, :]
        vj, rj = divmod(jp, 8)
        for v in range(nv):
            if v > vj:
                beats = sj >= groups[v]
            elif v < vj:
                beats = sj > groups[v]
            else:
                beats = (sj > groups[v]) | ((sj == groups[v]) & (sub > rj))
            ranks[v] = ranks[v] + jnp.where(beats, 1.0, 0.0)
    return jnp.concatenate([jnp.where(r < k, 1.0, 0.0) for r in ranks], axis=0)


def _proj_body(*refs, ncols, tcols, has_t):
    x_ref, wn_ref = refs[0], refs[1]
    o_refs = refs[3:] if has_t else refs[2:]
    xb = x_ref[...].astype(BF16)
    k = 0
    for c0, wd in ncols:
        o_refs[k][...] = _dot(xb, wn_ref[:, c0:c0 + wd]).astype(o_refs[k].dtype)
        k += 1
    if has_t:
        wt_ref = refs[2]
        for c0, wd, kinds in tcols:
            r = _dot_nt(wt_ref[c0:c0 + wd, :], xb)
            for kind in kinds:
                if kind == "full":
                    o_refs[k][...] = r
                else:
                    rb = r.astype(BF16)
                    for c in range(rb.shape[1] // LANES):
                        o_refs[k][c] = rb[:, c * LANES:(c + 1) * LANES]
                k += 1


def _proj(x, wn, ncols, wt=None, tcols=(), tm=512):
    B, T, K = x.shape
    tm = min(tm, T)
    has_t = wt is not None
    in_specs = [pl.BlockSpec((None, tm, K), lambda b, t: (b, t, 0)),
                pl.BlockSpec(wn.shape, lambda b, t: (0, 0))]
    args = [x, wn]
    if has_t:
        in_specs.append(pl.BlockSpec(wt.shape, lambda b, t: (0, 0)))
        args.append(wt)
    out_specs, out_shape = [], []
    for _, wd in ncols:
        out_specs.append(pl.BlockSpec((None, tm, wd), lambda b, t: (b, t, 0)))
        out_shape.append(jax.ShapeDtypeStruct((B, T, wd), F32))
    for _, wd, kinds in tcols:
        for kind in kinds:
            if kind == "full":
                out_specs.append(pl.BlockSpec((None, wd, tm), lambda b, t: (b, 0, t)))
                out_shape.append(jax.ShapeDtypeStruct((B, wd, T), F32))
            else:
                nb = tm // LANES
                out_specs.append(pl.BlockSpec((None, nb, wd, LANES), lambda b, t: (b, t, 0, 0)))
                out_shape.append(jax.ShapeDtypeStruct((B, T // LANES, wd, LANES), BF16))
    return pl.pallas_call(
        functools.partial(_proj_body, ncols=tuple(ncols), tcols=tuple(tcols), has_t=has_t),
        grid=(B, T // tm), in_specs=in_specs, out_specs=out_specs, out_shape=out_shape,
        compiler_params=_cparams(("parallel", "parallel"), big=True), name="proj")(*args)


def _compress_body(xk_ref, xv_ref, wk_ref, wv_ref, pe_ref, ok_ref, ov_ref):
    n = xk_ref.shape[0]
    colid = lax.broadcasted_iota(jnp.int32, (KV_W, n), 1)
    for x_ref, w_ref, o_ref in ((xk_ref, wk_ref, ok_ref), (xv_ref, wv_ref, ov_ref)):
        x = x_ref[...]
        a = _dot_nt(w_ref[0], (x + pe_ref[0]).astype(BF16))
        b = _dot_nt(w_ref[1], (x + pe_ref[1]).astype(BF16))
        o_ref[...] = jnp.where(colid < n - 1, a + pltpu.roll(b, n - 1, 1), 0.0)


def _compress(xk, xv, wk, wv, pe):
    B, n, cw = xk.shape
    xspec = pl.BlockSpec((None, n, cw), lambda b: (b, 0, 0))
    wspec = pl.BlockSpec(wk.shape, lambda b: (0, 0, 0))
    ospec = pl.BlockSpec((None, KV_W, n), lambda b: (b, 0, 0))
    return pl.pallas_call(
        _compress_body, grid=(B,),
        in_specs=[xspec, xspec, wspec, wspec, pl.BlockSpec(pe.shape, lambda b: (0, 0, 0))],
        out_specs=[ospec, ospec], out_shape=[jax.ShapeDtypeStruct((B, KV_W, n), F32)] * 2,
        compiler_params=_cparams(("parallel",), big=True), name="compress")(xk, xv, wk, wv, pe)


def _compress_paged_body(pt_ref, *refs, P):
    k_refs, v_refs = refs[:P], refs[P:2 * P]
    wk_ref, wv_ref, pe_ref, ak_ref, bk_ref, av_ref, bv_ref, tok_k, tok_v = refs[2 * P:]
    page = k_refs[0].shape[1]
    nch = P * page // CMP_STRIDE
    for i in range(P):
        tok_k[i * page:(i + 1) * page, :] = k_refs[i][...].T
        tok_v[i * page:(i + 1) * page, :] = v_refs[i][...].T
    xs = [jnp.concatenate([tok[pl.ds(c, nch, stride=CMP_STRIDE), :] for c in range(CMP_STRIDE)], axis=1)
          for tok in (tok_k, tok_v)]
    lhs = [(x + pe_ref[r]).astype(BF16) for x in xs for r in range(2)]
    outs = [_dot(lhs[0], wk_ref[0]), _dot(lhs[1], wk_ref[1]), _dot(lhs[2], wv_ref[0]), _dot(lhs[3], wv_ref[1])]
    for o_ref, o in zip((ak_ref, bk_ref, av_ref, bv_ref), outs):
        o_ref[...] = o


def _compress_paged(cache_k, cache_v, layer, page_table, wk, wv, pe, P):
    Bd, NP = page_table.shape
    page = cache_k.shape[-1]
    nch = P * page // CMP_STRIDE
    pspec = [pl.BlockSpec((None, None, KV_W, page), lambda b, s, pt, i=i: (layer, pt[b, s * P + i], 0, 0))
             for i in range(P)]
    const = lambda a: pl.BlockSpec(a.shape, lambda b, s, pt: (0, 0, 0))
    ospec = pl.BlockSpec((None, nch, KV_W), lambda b, s, pt: (b, s, 0))
    grid_spec = pltpu.PrefetchScalarGridSpec(num_scalar_prefetch=1, grid=(Bd, NP // P),
                                             in_specs=pspec + pspec + [const(wk), const(wv), const(pe)],
                                             out_specs=[ospec] * 4,
                                             scratch_shapes=[pltpu.VMEM((P * page, KV_W), F32)] * 2)
    return pl.pallas_call(
        functools.partial(_compress_paged_body, P=P), grid_spec=grid_spec,
        out_shape=[jax.ShapeDtypeStruct((Bd, NP * page // CMP_STRIDE, KV_W), F32)] * 4,
        compiler_params=_cparams(("parallel", "parallel"), big=True), name="compress_paged")(
            page_table, *([cache_k] * P), *([cache_v] * P), wk, wv, pe)


def _nsa_prompt_body(q_ref, ng_ref, ckt_ref, cvt_ref, sk_ref, sv_ref, wk_ref, wv_ref, ovl_ref, o_ref,
                     *, Q, KBN, n_slc):
    i = pl.program_id(1)
    q0 = i * Q
    KB = KBN * LANES
    QB = Q // LANES
    lane = lax.broadcasted_iota(jnp.int32, (Q, LANES), 1)
    lo_half = lane < HEAD_DIM
    qpos = q0 + lax.broadcasted_iota(jnp.int32, (Q, 1), 0)
    gate = jax.nn.sigmoid(ng_ref[...])
    slabs = [q_ref[:, LANES * s:LANES * (s + 1)] * (HEAD_DIM ** -0.5) for s in range(4)]

    def placed(h, g):
        sl = slabs[h // 2]
        if (h % 2) != g:
            sl = pltpu.roll(sl, HEAD_DIM, 1)
        return jnp.where(lo_half if g == 0 else jnp.logical_not(lo_half), sl, 0.0)

    ckt = ckt_ref[...].astype(BF16)
    cvt = cvt_ref[...].astype(BF16)
    NC = ckt.shape[1]
    cend = CMP_STRIDE * lax.broadcasted_iota(jnp.int32, (1, NC), 1) + (CMP_BLOCK - 1)
    dist_c = qpos - cend
    vis_c = dist_c >= 0
    dist_cf = dist_c.astype(F32)
    blk_t = lax.broadcasted_iota(jnp.int32, (n_slc, NSA_KV * Q), 0)
    col_t = lax.broadcasted_iota(jnp.int32, (n_slc, NSA_KV * Q), 1)
    qpos_t = q0 + (col_t & (Q - 1))
    cur_t = qpos_t >> 6
    valid_t = blk_t * SLC_BLOCK <= qpos_t
    forced_t = (blk_t == 0) | (blk_t == cur_t) | (blk_t == cur_t - 1)
    rows_of = [slice(h * Q, (h + 1) * Q) for h in range(NSA_HEADS)]
    group_of = [h // NSA_GROUP for h in range(NSA_HEADS)]
    qall = jnp.concatenate([placed(h, group_of[h]) for h in range(NSA_HEADS)], axis=0).astype(BF16)

    def softmax_rows(sd, distf, mask):
        ps = []
        for h in range(NSA_HEADS):
            s = jnp.where(mask, sd[rows_of[h]] - SLOPES[h] * distf, NEG)
            e = jnp.where(mask, jnp.exp(s - jnp.max(s, axis=-1, keepdims=True)), 0.0)
            ps.append(e / jnp.maximum(jnp.sum(e, axis=-1, keepdims=True), TINY))
        return ps

    sb0 = jnp.maximum(i * QB - WINDOW // LANES, 0)
    nwb = WINDOW // LANES + QB
    kt_w = jnp.concatenate([wk_ref[sb0 + c] for c in range(nwb)], axis=1)
    vt_w = jnp.concatenate([wv_ref[sb0 + c] for c in range(nwb)], axis=1)
    dist_w = qpos - (sb0 * LANES + lax.broadcasted_iota(jnp.int32, (1, nwb * LANES), 1))
    sd_c = _dot(qall, ckt)
    sd_w = _dot(qall, kt_w)
    ps = softmax_rows(sd_c, dist_cf, vis_c)
    pw = softmax_rows(sd_w, dist_w.astype(F32), (dist_w >= 0) & (dist_w <= WINDOW))
    o_cmp = _dot_nt(jnp.concatenate(ps, axis=0).astype(BF16), cvt)
    o_win = _dot_nt(jnp.concatenate(pw, axis=0).astype(BF16), vt_w)
    psum = jnp.concatenate([(ps[NSA_GROUP * g] + ps[NSA_GROUP * g + 1]) + (ps[NSA_GROUP * g + 2] + ps[NSA_GROUP * g + 3])
                            for g in range(NSA_KV)], axis=0)
    ph, pl_ = _split2(psum)
    imp = _dot(ph, ovl_ref[...]) + _dot(pl_, ovl_ref[...])
    score_t = jnp.where(valid_t, jnp.where(forced_t, FORCE_SCORE, imp.T[:n_slc]), -1.0)
    sel_t = _topk_rank(score_t, SLC_TOP)
    if n_slc < LANES:
        sel_t = jnp.concatenate([sel_t, jnp.zeros((LANES - n_slc, NSA_KV * Q), F32)], axis=0)
    sel_b = sel_t.T.astype(BF16)
    blk_row = lax.broadcasted_iota(jnp.int32, (LANES, KB), 0)

    half_rows = lax.broadcasted_iota(jnp.int32, (LANES, KB), 0) < HEAD_DIM

    def kt_body(t, carry):
        m, acc = carry
        kt = jnp.concatenate([sk_ref[t * KBN + c] for c in range(KBN)], axis=1)
        vt = jnp.concatenate([sv_ref[t * KBN + c] for c in range(KBN)], axis=1)
        one = jnp.ones_like(vt)
        vts = [jnp.where(half_rows, vt, one), jnp.where(half_rows, one, vt)]
        kpos = t * KB + lax.broadcasted_iota(jnp.int32, (1, KB), 1)
        expand = ((kpos >> 6) == blk_row).astype(BF16)
        sd = _dot(qall, kt)
        selx = _dot(sel_b, expand)
        dist = qpos - kpos
        causal = dist >= 0
        masks = [(selx[g * Q:(g + 1) * Q] > 0.5) & causal for g in range(NSA_KV)]
        distf = dist.astype(F32)
        es, alphas, ms = [], [], []
        for h in range(NSA_HEADS):
            mask = masks[group_of[h]]
            s = jnp.where(mask, sd[rows_of[h]] - SLOPES[h] * distf, NEG)
            m_old = m[rows_of[h]]
            m_new = jnp.maximum(m_old, jnp.max(s, axis=-1, keepdims=True))
            alphas.append(jnp.exp(m_old - m_new))
            ms.append(m_new)
            es.append(jnp.where(mask, jnp.exp(s - m_new), 0.0).astype(BF16))
        pv = jnp.concatenate([_dot_nt(jnp.concatenate(es[NSA_GROUP * g:NSA_GROUP * (g + 1)], axis=0), vts[g])
                              for g in range(NSA_KV)], axis=0)
        return jnp.concatenate(ms, axis=0), jnp.concatenate(alphas, axis=0) * acc + pv

    nkt = (q0 + Q + KB - 1) // KB
    init = (jnp.full((NSA_HEADS * Q, 1), NEG, F32), jnp.zeros((NSA_HEADS * Q, LANES), F32))
    _, acc_s = lax.fori_loop(0, nkt, kt_body, init)
    o_slc = acc_s / jnp.maximum(pltpu.roll(acc_s, HEAD_DIM, 1), TINY)
    outs = [None] * NSA_HEADS
    for h in range(NSA_HEADS):
        rows = rows_of[h]
        oh = (gate[:, 3 * h:3 * h + 1] * o_cmp[rows] + gate[:, 3 * h + 1:3 * h + 2] * o_slc[rows]
              + gate[:, 3 * h + 2:3 * h + 3] * o_win[rows])
        if (h % 2) != group_of[h]:
            oh = pltpu.roll(oh, HEAD_DIM, 1)
        outs[h] = oh
    for s in range(4):
        o_ref[:, LANES * s:LANES * (s + 1)] = jnp.where(lo_half, outs[2 * s], outs[2 * s + 1])


def _nsa_prompt(q, ng, ckt, cvt, skb, svb, wkb, wvb, ovl, Q=128, KBN=4):
    B, T, _ = q.shape
    n_slc = T // SLC_BLOCK
    NC = ckt.shape[2]
    nblk = T // LANES
    kvspec = pl.BlockSpec((None, nblk, KV_W, LANES), lambda b, i: (b, 0, 0, 0))
    cspec = pl.BlockSpec((None, KV_W, NC), lambda b, i: (b, 0, 0))
    return pl.pallas_call(
        functools.partial(_nsa_prompt_body, Q=Q, KBN=KBN, n_slc=n_slc), grid=(B, T // Q),
        in_specs=[pl.BlockSpec((None, Q, NSA_HEADS * HEAD_DIM), lambda b, i: (b, i, 0)),
                  pl.BlockSpec((None, Q, LANES), lambda b, i: (b, i, 0)),
                  cspec, cspec, kvspec, kvspec, kvspec, kvspec,
                  pl.BlockSpec(ovl.shape, lambda b, i: (0, 0))],
        out_specs=pl.BlockSpec((None, Q, NSA_HEADS * HEAD_DIM), lambda b, i: (b, i, 0)),
        out_shape=jax.ShapeDtypeStruct((B, T, NSA_HEADS * HEAD_DIM), F32),
        compiler_params=_cparams(("parallel", "parallel"), big=True), name="nsa_prompt")(
            q, ng, ckt, cvt, skb, svb, wkb, wvb, ovl)


def _sb_block(qh, kt, vt, u, carry, acc, mask):
    n = qh.shape[0]
    s = _dot(qh, kt)
    sp = _softplus(s)
    l1m = -sp if mask is None else jnp.where(mask, -sp, 0.0)
    hi, lo = _split2(l1m)
    r2 = _dot(jnp.concatenate([hi, lo], axis=0), u)
    after = r2[:n] + r2[n:]
    w = jnp.exp((s - sp) + after + carry)
    if mask is not None:
        w = jnp.where(mask, w, 0.0)
    acc = acc + _dot_nt(w.astype(BF16), vt)
    carry = carry + jnp.sum(l1m, axis=-1, keepdims=True)
    return carry, acc


def _sb_pairs_step(qs, kts, vts, u2_ref, row_lo, carries, accs, mask2):
    n = qs[0].shape[0]
    zero = jnp.zeros_like(kts[0])
    bd = lambda t: jnp.concatenate([jnp.where(row_lo, t, zero), jnp.where(row_lo, zero, t)], axis=1)
    ss = [_dot(q, bd(kt)) for q, kt in zip(qs, kts)]
    sps, cats = [], []
    for s in ss:
        sp = jnp.maximum(s, 0.0) + jnp.log(1.0 + jnp.exp(-jnp.abs(s)))
        l1m = -sp if mask2 is None else jnp.where(mask2, -sp, 0.0)
        hi, lo = _split2(l1m)
        sps.append(sp)
        cats.append(jnp.concatenate([hi, lo], axis=0))
    rs = [_dot(c, u2_ref[...]) for c in cats]
    ws, new_c = [], []
    for s, sp, r, carry2 in zip(ss, sps, rs, carries):
        r = r[:n] + r[n:]
        kb2 = s.shape[1]
        w = jnp.exp((s - sp) + r[:, :kb2] + carry2)
        if mask2 is not None:
            w = jnp.where(mask2, w, 0.0)
        ws.append(w.astype(BF16))
        new_c.append(carry2 + r[:, kb2:])
    return new_c, [acc + _dot_nt(w, bd(vt)) for acc, w, vt in zip(accs, ws, vts)]


def _sb_prompt_body(q_ref, kt_ref, vt_ref, u2_ref, o_ref, carry_s, acc_s, *, Q):
    i = pl.program_id(1)
    npair = SB_W // LANES
    tri = lax.broadcasted_iota(jnp.int32, (Q, Q), 1) < lax.broadcasted_iota(jnp.int32, (Q, Q), 0)
    mask2 = jnp.concatenate([tri, tri], axis=1)
    row_lo = lax.broadcasted_iota(jnp.int32, (LANES, Q), 0) < HEAD_DIM
    slab = [slice(LANES * p, LANES * (p + 1)) for p in range(npair)]

    def step(j, carries, accs, mask):
        qs = [(q_ref[:, slab[p]] * (HEAD_DIM ** -0.5)).astype(BF16) for p in range(npair)]
        new_c, new_a = _sb_pairs_step(qs, [kt_ref[j, slab[p], :] for p in range(npair)],
                                      [vt_ref[j, slab[p], :] for p in range(npair)], u2_ref, row_lo,
                                      carries, accs, mask)
        for p in range(npair):
            carry_s[p] = new_c[p]
            acc_s[p] = new_a[p]

    step(i, [jnp.zeros((Q, 2 * Q), F32)] * npair, [jnp.zeros((Q, LANES), F32)] * npair, mask2)

    def alive():
        mx = carry_s[0]
        for p in range(1, npair):
            mx = jnp.maximum(mx, carry_s[p])
        return (jnp.max(mx) > SB_DEAD_LOG).astype(jnp.int32)

    def body(st):
        jj, _ = st
        step(i - 1 - jj, [carry_s[p] for p in range(npair)], [acc_s[p] for p in range(npair)], None)
        return jj + 1, alive()

    lax.while_loop(lambda st: (st[0] < i) & (st[1] > 0), body, (jnp.int32(0), alive()))
    for p in range(npair):
        o_ref[:, slab[p]] = acc_s[p]


def _sb_prompt(q, ktb, vtb, u2):
    B, T, _ = q.shape
    Q = LANES
    nblk = T // LANES
    kvspec = pl.BlockSpec((None, nblk, SB_W, LANES), lambda b, i: (b, 0, 0, 0))
    return pl.pallas_call(
        functools.partial(_sb_prompt_body, Q=Q), grid=(B, T // Q),
        in_specs=[pl.BlockSpec((None, Q, SB_W), lambda b, i: (b, i, 0)), kvspec, kvspec,
                  pl.BlockSpec(u2.shape, lambda b, i: (0, 0))],
        out_specs=pl.BlockSpec((None, Q, SB_W), lambda b, i: (b, i, 0)),
        out_shape=jax.ShapeDtypeStruct((B, T, SB_W), F32),
        scratch_shapes=[pltpu.VMEM((SB_W // LANES, Q, 2 * Q), F32), pltpu.VMEM((SB_W // LANES, Q, LANES), F32)],
        compiler_params=_cparams(("parallel", "parallel"), big=True), name="sb_prompt")(q, ktb, vtb, u2)


def _mem_body(q_ref, k_ref, v_ref, o_ref):
    for h in range(MEM_HEADS):
        sl = slice(MEM_HEAD_DIM * h, MEM_HEAD_DIM * (h + 1))
        s = _dot_nt(q_ref[:, sl].astype(BF16), k_ref[:, sl].astype(BF16)) * (MEM_HEAD_DIM ** -0.5)
        e = jnp.exp(s - jnp.max(s, axis=-1, keepdims=True))
        p = e / jnp.sum(e, axis=-1, keepdims=True)
        o_ref[:, sl] = _dot(p.astype(BF16), v_ref[:, sl].astype(BF16))


def _mem_attn(q, mk, mv, tq=512):
    B, T, W = q.shape
    tq = min(tq, T)
    N = mk.shape[1]
    kvspec = pl.BlockSpec((None, N, W), lambda b, i: (b, 0, 0))
    return pl.pallas_call(
        _mem_body, grid=(B, T // tq),
        in_specs=[pl.BlockSpec((None, tq, W), lambda b, i: (b, i, 0)), kvspec, kvspec],
        out_specs=pl.BlockSpec((None, tq, W), lambda b, i: (b, i, 0)),
        out_shape=jax.ShapeDtypeStruct((B, T, W), F32),
        compiler_params=_cparams(("parallel", "parallel")), name="mem_attn")(q, mk, mv)


def _rglru_body(x_ref, g_ref, cb_ref, h0_ref, cw_ref, cbias_ref, wbd_ref, gb_ref, lam_ref,
                y_ref, hn_ref, cn_ref, xbuf, a_s, b_s, h_s, hc, *, n_t):
    t = pl.program_id(1)
    tt = x_ref.shape[0]
    hist = CONV_W - 1

    @pl.when(t == 0)
    def _():
        xbuf[8 - hist:8, :] = cb_ref[...]
        hc[...] = h0_ref[...]

    xbuf[8:8 + tt, :] = x_ref[...]
    xc = cw_ref[0:1, :] * xbuf[8 - hist:8 - hist + tt, :]
    for k in range(1, CONV_W):
        xc = xc + cw_ref[k:k + 1, :] * xbuf[8 - hist + k:8 - hist + k + tt, :]
    xc = xc + cbias_ref[...]
    gates = _dot(xc.astype(BF16), wbd_ref[...]) + gb_ref[...]
    r = jax.nn.sigmoid(gates[:, :D_RNN])
    ig = jax.nn.sigmoid(gates[:, D_RNN:])
    log_a = (-LRU_C * _softplus(-lam_ref[...])) * r
    a = jnp.exp(log_a)
    a_s[...] = a
    b_s[...] = jnp.sqrt(-jnp.tanh(log_a) * (a * a + 1.0)) * (ig * xc)

    def step(k, h):
        h = a_s[pl.ds(k, 1), :] * h + b_s[pl.ds(k, 1), :]
        h_s[pl.ds(k, 1), :] = h
        return h

    h = lax.fori_loop(0, tt, step, hc[...], unroll=8 if tt % 8 == 0 else tt)
    hc[...] = h
    y_ref[...] = h_s[...] * jax.nn.gelu(g_ref[...])
    hn_ref[...] = h
    cn_ref[...] = xbuf[8 + tt - hist:8 + tt, :]
    if n_t > 1:
        xbuf[0:8, :] = xbuf[tt:tt + 8, :]


def _rglru(x, gate_in, conv_buf, h0, conv_w, conv_b, wbd, gbias, lam, tt=256):
    B, T, W = x.shape
    tt = min(tt, T)
    n_t = T // tt
    xspec = pl.BlockSpec((None, tt, W), lambda b, t: (b, t, 0))
    const = lambda a: pl.BlockSpec(a.shape, lambda b, t: (0,) * a.ndim)
    return pl.pallas_call(
        functools.partial(_rglru_body, n_t=n_t), grid=(B, n_t),
        in_specs=[xspec, xspec, pl.BlockSpec((None, CONV_W - 1, W), lambda b, t: (b, 0, 0)),
                  pl.BlockSpec((None, 1, W), lambda b, t: (b, 0, 0)),
                  const(conv_w), const(conv_b), const(wbd), const(gbias), const(lam)],
        out_specs=[xspec, pl.BlockSpec((None, 1, W), lambda b, t: (b, 0, 0)),
                   pl.BlockSpec((None, CONV_W - 1, W), lambda b, t: (b, 0, 0))],
        out_shape=[jax.ShapeDtypeStruct((B, T, W), F32), jax.ShapeDtypeStruct((B, 1, W), F32),
                   jax.ShapeDtypeStruct((B, CONV_W - 1, W), F32)],
        scratch_shapes=[pltpu.VMEM((tt + 8, W), F32), pltpu.VMEM((tt, W), F32), pltpu.VMEM((tt, W), F32),
                        pltpu.VMEM((tt, W), F32), pltpu.VMEM((1, W), F32)],
        compiler_params=_cparams(("parallel", "arbitrary")), name="rglru")(
            x, gate_in, conv_buf, h0, conv_w, conv_b, wbd, gbias, lam)


def _merge_body(x_ref, b0, b1, b2, b3, wmg_ref, wbr_ref, wo_ref, g_ref, be_ref, o_ref, *, alpha):
    x = x_ref[...]
    xb = x.astype(BF16)
    acc = None
    off = 0
    for n, br in enumerate((b0, b1, b2, b3)):
        wdt = br.shape[-1]
        gate = jax.nn.sigmoid(_dot(xb, wmg_ref[:, n * D_MODEL:(n + 1) * D_MODEL]))
        term = gate * _dot(br[...].astype(BF16), wbr_ref[off:off + wdt, :])
        acc = term if acc is None else acc + term
        off += wdt
    mix = _dot(acc.astype(BF16), wo_ref[...])
    o_ref[...] = _layer_norm(alpha * x + mix, g_ref[...], be_ref[...])


def _merge_ln(x, branches, wmg, wbr, wo, g, be, alpha, tm=256):
    M = x.shape[0]
    tm = min(tm, M)
    row = lambda w: pl.BlockSpec((tm, w), lambda i: (i, 0))
    const = lambda a: pl.BlockSpec(a.shape, lambda i: (0, 0))
    return pl.pallas_call(
        functools.partial(_merge_body, alpha=alpha), grid=(M // tm,),
        in_specs=[row(D_MODEL)] + [row(b.shape[1]) for b in branches] + [const(wmg), const(wbr), const(wo),
                                                                         const(g), const(be)],
        out_specs=row(D_MODEL), out_shape=jax.ShapeDtypeStruct((M, D_MODEL), F32),
        compiler_params=_cparams(("parallel",), big=True), name="merge_ln")(x, *branches, wmg, wbr, wo, g, be)


def _ffn_body(x_ref, wa_ref, wb_ref, w2_ref, g_ref, be_ref, o_ref, acc, *, alpha, nf):
    j = pl.program_id(1)
    x = x_ref[...]
    xb = x.astype(BF16)
    a = _dot(xb, wa_ref[...])
    b = _dot(xb, wb_ref[...])
    part = _dot(((a * jax.nn.sigmoid(a)) * b).astype(BF16), w2_ref[...])

    @pl.when(j == 0)
    def _():
        acc[...] = part

    @pl.when(j > 0)
    def _():
        acc[...] += part

    @pl.when(j == nf - 1)
    def _():
        o_ref[...] = _layer_norm(alpha * x + acc[...], g_ref[...], be_ref[...])


def _ffn_ln(x, w13, w2, g, be, alpha, tm=512, tf=1408):
    M = x.shape[0]
    tm = min(tm, M)
    F = w2.shape[0]
    nf = F // tf
    const = lambda a: pl.BlockSpec(a.shape, lambda i, j: (0, 0))
    return pl.pallas_call(
        functools.partial(_ffn_body, alpha=alpha, nf=nf), grid=(M // tm, nf),
        in_specs=[pl.BlockSpec((tm, D_MODEL), lambda i, j: (i, 0)),
                  pl.BlockSpec((D_MODEL, tf), lambda i, j: (0, j)),
                  pl.BlockSpec((D_MODEL, tf), lambda i, j: (0, j + nf)),
                  pl.BlockSpec((tf, D_MODEL), lambda i, j: (j, 0)), const(g), const(be)],
        out_specs=pl.BlockSpec((tm, D_MODEL), lambda i, j: (i, 0)),
        out_shape=jax.ShapeDtypeStruct((M, D_MODEL), F32),
        scratch_shapes=[pltpu.VMEM((tm, D_MODEL), F32)],
        compiler_params=_cparams(("parallel", "arbitrary"), big=True), name="ffn_ln")(x, w13, w13, w2, g, be)


def _moe_body(x_ref, rwh_ref, rwl_ref, rb_ref, low_ref, wa_ref, wb_ref, w2_ref, g_ref, be_ref, o_ref,
              acc, comb, ind_c, pos_c, ind_r, pos_r, cnt_s, *, alpha, n_exp, chunk):
    e = pl.program_id(1)
    x = x_ref[...]
    xh = x.astype(BF16)
    tm = x.shape[0]
    lane = lax.broadcasted_iota(jnp.int32, (tm, LANES), 1)

    @pl.when(e == 0)
    def _():
        xl = (x - xh.astype(F32)).astype(BF16)
        logits = (_dot(xh, rwh_ref[...]) + _dot(xh, rwl_ref[...])) + _dot(xl, rwh_ref[...]) + rb_ref[...]
        valid = lane < n_exp
        lanef = lane.astype(F32)
        logits = jnp.where(valid, logits, NEG)
        ex = jnp.where(valid, jnp.exp(logits - jnp.max(logits, axis=-1, keepdims=True)), 0.0)
        p = ex / jnp.sum(ex, axis=-1, keepdims=True)
        p = jnp.where(valid, p, -1.0)
        m1 = jnp.max(p, axis=-1, keepdims=True)
        hit1 = lanef == jnp.min(jnp.where(p == m1, lanef, 1.0e9), axis=-1, keepdims=True)
        p2 = jnp.where(hit1, -1.0, p)
        m2 = jnp.max(p2, axis=-1, keepdims=True)
        hit2 = lanef == jnp.min(jnp.where(p2 == m2, lanef, 1.0e9), axis=-1, keepdims=True)
        den = m1 + m2
        comb[...] = jnp.where(hit1, m1 / den, 0.0) + jnp.where(hit2, m2 / den, 0.0)
        acc[...] = jnp.zeros_like(acc)
        ind = jnp.where(hit1 | hit2, 1.0, 0.0)
        pos = _dot(low_ref[...], ind.astype(BF16))
        ind_c[...] = ind
        pos_c[...] = pos
        ind_r[...] = ind.T
        pos_r[...] = pos.T
        for k in range(n_exp):
            cnt_s[k] = jnp.sum(ind[:, k:k + 1]).astype(jnp.int32)

    sel_lane = lane == e
    w_e = jnp.sum(jnp.where(sel_lane, comb[...], 0.0), axis=-1, keepdims=True)
    used_c = jnp.sum(jnp.where(sel_lane, ind_c[...], 0.0), axis=-1, keepdims=True) > 0.5
    slot_c = jnp.sum(jnp.where(sel_lane, pos_c[...], 0.0), axis=-1, keepdims=True)
    used_r = ind_r[pl.ds(e, 1), :] > 0.5
    slot_r = pos_r[pl.ds(e, 1), :]
    row_id = lax.broadcasted_iota(jnp.int32, (chunk, tm), 0).astype(F32)
    col_id = lax.broadcasted_iota(jnp.int32, (tm, chunk), 1).astype(F32)

    def run_chunk(c, _):
        base = (c * chunk).astype(F32)
        gather = jnp.where(used_r & (slot_r == base + row_id), 1.0, 0.0).astype(BF16)
        xs = _dot(gather, xh).astype(BF16)
        a = _dot(xs, wa_ref[...])
        b = _dot(xs, wb_ref[...])
        o = _dot(((a * jax.nn.sigmoid(a)) * b).astype(BF16), w2_ref[...])
        scatter = jnp.where(used_c & (slot_c == base + col_id), 1.0, 0.0).astype(BF16)
        oh, ol = _split2(o)
        acc[...] += w_e * (_dot(scatter, oh) + _dot(scatter, ol))
        return 0

    lax.fori_loop(0, (cnt_s[e] + chunk - 1) // chunk, run_chunk, 0)

    @pl.when(e == n_exp - 1)
    def _():
        o_ref[...] = _layer_norm(alpha * x + acc[...], g_ref[...], be_ref[...])


def _moe_ln(x, rwh, rwl, rb, w13, w2, g, be, alpha, tm=512):
    M = x.shape[0]
    tm = min(tm, M)
    chunk = min(LANES, tm)
    n_exp, de, _ = w2.shape
    low = jnp.asarray(np.arange(tm)[:, None] > np.arange(tm)[None, :], dtype=BF16)
    const = lambda a: pl.BlockSpec(a.shape, lambda i, e: (0, 0))
    return pl.pallas_call(
        functools.partial(_moe_body, alpha=alpha, n_exp=n_exp, chunk=chunk), grid=(M // tm, n_exp),
        in_specs=[pl.BlockSpec((tm, D_MODEL), lambda i, e: (i, 0)), const(rwh), const(rwl), const(rb), const(low),
                  pl.BlockSpec((None, D_MODEL, de), lambda i, e: (e, 0, 0)),
                  pl.BlockSpec((None, D_MODEL, de), lambda i, e: (e, 0, 1)),
                  pl.BlockSpec((None, de, D_MODEL), lambda i, e: (e, 0, 0)), const(g), const(be)],
        out_specs=pl.BlockSpec((tm, D_MODEL), lambda i, e: (i, 0)),
        out_shape=jax.ShapeDtypeStruct((M, D_MODEL), F32),
        scratch_shapes=[pltpu.VMEM((tm, D_MODEL), F32), pltpu.VMEM((tm, LANES), F32), pltpu.VMEM((tm, LANES), F32),
                        pltpu.VMEM((tm, LANES), F32), pltpu.VMEM((LANES, tm), F32), pltpu.VMEM((LANES, tm), F32),
                        pltpu.SMEM((n_exp,), jnp.int32)],
        compiler_params=_cparams(("parallel", "arbitrary"), big=True), name="moe_ln")(
            x, rwh, rwl, rb, low, w13, w13, w2, g, be)


def _row_consts(rows, Td, past):
    row = lax.broadcasted_iota(jnp.int32, (rows, 1), 0)
    head = row // Td
    qpos = past + (row - head * Td)
    slope = jnp.zeros((rows, 1), F32)
    for h in range(NSA_HEADS):
        slope = jnp.where(head == h, SLOPES[h], slope)
    return qpos, slope


def _nsa_s_cmp_body(qz_ref, ak_ref, bk_ref, av_ref, bv_ref, ovl_ref, hs_ref, ocmp_ref, sel_ref, *, Td, past, n_slc):
    rows = qz_ref.shape[0]
    NC = ak_ref.shape[0]
    qpos, slope = _row_consts(rows, Td, past)
    ck = (ak_ref[...] + pltpu.roll(bk_ref[...], NC - 1, 0)).astype(BF16)
    cv = (av_ref[...] + pltpu.roll(bv_ref[...], NC - 1, 0)).astype(BF16)
    qz = (qz_ref[...] * (HEAD_DIM ** -0.5)).astype(BF16)
    cend = CMP_STRIDE * lax.broadcasted_iota(jnp.int32, (1, NC), 1) + (CMP_BLOCK - 1)
    dist = qpos - cend
    p = _masked_softmax(_dot_nt(qz, ck) - slope * dist.astype(F32), dist >= 0)
    ocmp_ref[...] = _dot(p.astype(BF16), cv)
    ph, pl_ = _split2(p)
    imp_h = _dot(ph, ovl_ref[...]) + _dot(pl_, ovl_ref[...])
    i1, i2, i3 = _split3(imp_h)
    hs = hs_ref[...]
    imp = (_dot(hs, i1) + _dot(hs, i2)) + _dot(hs, i3)
    NSP = imp.shape[1]
    lane = lax.broadcasted_iota(jnp.int32, (rows, NSP), 1)
    cur = qpos >> 6
    valid = (lane * SLC_BLOCK <= qpos) & (lane < n_slc)
    forced = (lane == 0) | (lane == cur) | (lane == cur - 1)
    score = jnp.where(valid, jnp.where(forced, FORCE_SCORE, imp), jnp.where(lane < n_slc, -1.0, -2.0))
    sel_ref[...] = _topk_mask(score, lane.astype(F32), SLC_TOP)


def _nsa_s_cmp(qz, ak, bk, av, bv, ovl, hs, Td, past, n_slc):
    Bd, rows, _ = qz.shape
    NC = ak.shape[1]
    NSP = ovl.shape[1]
    cspec = pl.BlockSpec((None, NC, KV_W), lambda b: (b, 0, 0))
    return pl.pallas_call(
        functools.partial(_nsa_s_cmp_body, Td=Td, past=past, n_slc=n_slc), grid=(Bd,),
        in_specs=[pl.BlockSpec((None, rows, KV_W), lambda b: (b, 0, 0)), cspec, cspec, cspec, cspec,
                  pl.BlockSpec(ovl.shape, lambda b: (0, 0)), pl.BlockSpec(hs.shape, lambda b: (0, 0))],
        out_specs=[pl.BlockSpec((None, rows, KV_W), lambda b: (b, 0, 0)),
                   pl.BlockSpec((None, rows, NSP), lambda b: (b, 0, 0))],
        out_shape=[jax.ShapeDtypeStruct((Bd, rows, KV_W), F32), jax.ShapeDtypeStruct((Bd, rows, NSP), F32)],
        compiler_params=_cparams(("parallel",), big=True), name="nsa_sample_cmp")(qz, ak, bk, av, bv, ovl, hs)


def _nsa_s_slc_body(pt_ref, plist_ref, nneed_ref, *refs, P, Td, past, n_steps):
    kp = refs[:P]
    vp = refs[P:2 * P]
    (qz_ref, g_ref, ocmp_ref, sel_ref, nk_ref, nv_ref, wk_ref, wv_ref, o_ref, m_s, l_s, acc_s) = refs[2 * P:]
    b_id = pl.program_id(0)
    s_id = pl.program_id(1)
    rows = qz_ref.shape[0]
    NSP = sel_ref.shape[1]
    qpos, slope = _row_consts(rows, Td, past)
    qz = (qz_ref[...] * (HEAD_DIM ** -0.5)).astype(BF16)
    sel_b = sel_ref[...].astype(BF16)

    @pl.when(s_id == 0)
    def _():
        m_s[...] = jnp.full_like(m_s, NEG)
        l_s[...] = jnp.zeros_like(l_s)
        acc_s[...] = jnp.zeros_like(acc_s)

    def tile(kt, vt, kpos, extra):
        nk = kt.shape[1]
        expand = ((kpos >> 6) == lax.broadcasted_iota(jnp.int32, (NSP, nk), 0)).astype(BF16)
        dist = qpos - kpos
        mask = (_dot(sel_b, expand) > 0.5) & (dist >= 0)
        if extra is not None:
            mask = mask & extra
        s = jnp.where(mask, _dot(qz, kt) - slope * dist.astype(F32), NEG)
        m_old = m_s[...]
        m_new = jnp.maximum(m_old, jnp.max(s, axis=-1, keepdims=True))
        alpha = jnp.exp(m_old - m_new)
        e = jnp.where(mask, jnp.exp(s - m_new), 0.0)
        l_s[...] = alpha * l_s[...] + jnp.sum(e, axis=-1, keepdims=True)
        acc_s[...] = alpha * acc_s[...] + _dot_nt(e.astype(BF16), vt)
        m_s[...] = m_new

    nneed = nneed_ref[b_id]

    @pl.when(s_id * P < nneed)
    def _():
        kt = jnp.concatenate([r[...] for r in kp], axis=1).astype(BF16)
        vt = jnp.concatenate([r[...] for r in vp], axis=1).astype(BF16)
        lane = lax.broadcasted_iota(jnp.int32, (1, LANES), 1)
        kpos = jnp.concatenate([plist_ref[b_id, s_id * P + i] * LANES + lane for i in range(P)], axis=1)
        live = jnp.concatenate([jnp.full((1, LANES), s_id * P + i, jnp.int32) for i in range(P)], axis=1) < nneed
        tile(kt, vt, kpos, live)

    @pl.when(s_id == n_steps - 1)
    def _():
        nl = lax.broadcasted_iota(jnp.int32, (1, LANES), 1)
        tile(nk_ref[...].astype(BF16), nv_ref[...].astype(BF16), past + nl, nl < Td)
        o_slc = acc_s[...] / jnp.maximum(l_s[...], TINY)
        nw = wk_ref.shape[1]
        wl = lax.broadcasted_iota(jnp.int32, (1, nw), 1)
        dist = qpos - (past - (nw - LANES) + wl)
        okw = (wl < nw - LANES + Td) & (dist >= 0) & (dist <= WINDOW)
        pw = _masked_softmax(_dot(qz, wk_ref[...].astype(BF16)) - slope * dist.astype(F32), okw)
        o_win = _dot_nt(pw.astype(BF16), wv_ref[...].astype(BF16))
        gate = jax.nn.sigmoid(g_ref[...])
        o_ref[...] = gate[:, 0:1] * ocmp_ref[...] + gate[:, 1:2] * o_slc + gate[:, 2:3] * o_win


def _nsa_s_slc(page_table, layer, skc, svc, qz, graw, ocmp, sel, nks, nvs, wkx, wvx, Td, past, P):
    Bd, NP = page_table.shape
    rows = qz.shape[1]
    n_steps = NP // P
    blocks_per_page = LANES // SLC_BLOCK
    need_blk = jnp.any(sel[:, :, :NP * blocks_per_page] > 0.5, axis=1)
    need_page = jnp.any(need_blk.reshape(Bd, NP, blocks_per_page), axis=2)
    nneed = jnp.sum(need_page, axis=1).astype(jnp.int32)
    order = jnp.argsort(jnp.logical_not(need_page), axis=1, stable=True).astype(jnp.int32)
    slot = jnp.minimum(jnp.arange(NP, dtype=jnp.int32)[None, :], nneed[:, None] - 1)
    plist = jnp.take_along_axis(order, slot, axis=1)
    pspec = [pl.BlockSpec((None, None, KV_W, LANES),
                          lambda b, s, pt, pls, nn, i=i: (layer, pt[b, pls[b, s * P + i]], 0, 0)) for i in range(P)]
    per_b = lambda a: pl.BlockSpec((None,) + a.shape[1:], lambda b, s, pt, pls, nn: (b, 0, 0))
    grid_spec = pltpu.PrefetchScalarGridSpec(
        num_scalar_prefetch=3, grid=(Bd, n_steps),
        in_specs=pspec + pspec + [per_b(a) for a in (qz, graw, ocmp, sel, nks, nvs, wkx, wvx)],
        out_specs=pl.BlockSpec((None, rows, KV_W), lambda b, s, pt, pls, nn: (b, 0, 0)),
        scratch_shapes=[pltpu.VMEM((rows, 1), F32), pltpu.VMEM((rows, 1), F32), pltpu.VMEM((rows, KV_W), F32)])
    return pl.pallas_call(
        functools.partial(_nsa_s_slc_body, P=P, Td=Td, past=past, n_steps=n_steps), grid_spec=grid_spec,
        out_shape=jax.ShapeDtypeStruct((Bd, rows, KV_W), F32),
        compiler_params=_cparams(("parallel", "arbitrary"), big=True), name="nsa_sample_slc")(
            page_table, plist, nneed, *([skc] * P), *([svc] * P), qz, graw, ocmp, sel, nks, nvs, wkx, wvx)


def _sb_s_body(pt_ref, need_ref, *refs, P, Td, n_steps, first):
    kp = refs[:P]
    vp = refs[P:2 * P]
    q_ref, in0_ref, in1_ref, u_ref, co_ref, o_ref, carry_s, acc_s, alive_s = refs[2 * P:]
    s_id = pl.program_id(1)
    rows = q_ref.shape[0]
    qb = (q_ref[...] * (HEAD_DIM ** -0.5)).astype(BF16)
    u = u_ref[...]

    @pl.when(s_id == 0)
    def _():
        if first:
            row = lax.broadcasted_iota(jnp.int32, (rows, LANES), 0)
            lane = lax.broadcasted_iota(jnp.int32, (rows, LANES), 1)
            newer = lane < (row - (row // Td) * Td)
            carry, acc = _sb_block(qb, in0_ref[...].astype(BF16), in1_ref[...].astype(BF16), u,
                                   jnp.zeros((rows, 1), F32), jnp.zeros((rows, SB_W), F32), newer)
        else:
            carry, acc = in0_ref[:, 0:1], in1_ref[...]
        carry_s[...] = carry
        acc_s[...] = acc
        alive_s[0] = (jnp.max(carry) > SB_DEAD_LOG).astype(jnp.int32)

    @pl.when(alive_s[0] > 0)
    def _():
        carry, acc = carry_s[...], acc_s[...]
        order = range(P - 1, -1, -1)
        ss = [_dot(qb, kp[i][...].astype(BF16)) for i in order]
        sps = [_softplus(s) for s in ss]
        cats = [jnp.concatenate(_split2(-sp), axis=0) for sp in sps]
        afters = [_dot(c, u) for c in cats]
        ws = []
        for s, sp, r in zip(ss, sps, afters):
            ws.append(jnp.exp((s - sp) + (r[:rows] + r[rows:]) + carry).astype(BF16))
            carry = carry - jnp.sum(sp, axis=-1, keepdims=True)
        for w, i in zip(ws, order):
            acc = acc + _dot_nt(w, vp[i][...].astype(BF16))
        carry_s[...] = carry
        acc_s[...] = acc
        alive_s[0] = (jnp.max(carry) > SB_DEAD_LOG).astype(jnp.int32)

    @pl.when(s_id == n_steps - 1)
    def _():
        co_ref[...] = jnp.broadcast_to(carry_s[...], co_ref.shape)
        o_ref[...] = acc_s[...]


def _sb_sample_part(page_table, need, layer, kc, vc, qbd, in0, in1, u, Td, P, first, s0, n_steps):
    Bd, NP = page_table.shape
    rows = qbd.shape[1]
    pspec = [pl.BlockSpec((None, None, SB_W, LANES),
                          lambda b, s, pt, nd, i=i: (layer, pt[b, NP - (s0 + s * nd[b] + 1) * P + i], 0, 0))
             for i in range(P)]
    per_b = lambda a: pl.BlockSpec((None,) + a.shape[1:], lambda b, s, pt, nd: (b, 0, 0))
    grid_spec = pltpu.PrefetchScalarGridSpec(
        num_scalar_prefetch=2, grid=(Bd, n_steps),
        in_specs=pspec + pspec + [per_b(qbd), per_b(in0), per_b(in1),
                                  pl.BlockSpec(u.shape, lambda b, s, pt, nd: (0, 0))],
        out_specs=[pl.BlockSpec((None, rows, LANES), lambda b, s, pt, nd: (b, 0, 0)),
                   pl.BlockSpec((None, rows, SB_W), lambda b, s, pt, nd: (b, 0, 0))],
        scratch_shapes=[pltpu.VMEM((rows, 1), F32), pltpu.VMEM((rows, SB_W), F32), pltpu.SMEM((1,), jnp.int32)])
    return pl.pallas_call(
        functools.partial(_sb_s_body, P=P, Td=Td, n_steps=n_steps, first=first), grid_spec=grid_spec,
        out_shape=[jax.ShapeDtypeStruct((Bd, rows, LANES), F32), jax.ShapeDtypeStruct((Bd, rows, SB_W), F32)],
        compiler_params=_cparams(("parallel", "arbitrary"), big=True), name="sb_sample")(
            page_table, need, *([kc] * P), *([vc] * P), qbd, in0, in1, u)


def _sb_sample(page_table, layer, kc, vc, qbd, nk, nv, u, Td, P):
    Bd, NP = page_table.shape
    carry, acc = _sb_sample_part(page_table, jnp.ones((Bd,), jnp.int32), layer, kc, vc, qbd, nk, nv, u, Td, P,
                                 True, 0, 1)
    if NP // P == 1:
        return acc
    need = (jnp.max(carry[:, :, 0], axis=1) > SB_DEAD_LOG).astype(jnp.int32)
    older = lambda: _sb_sample_part(page_table, need, layer, kc, vc, qbd, carry, acc, u, Td, P, False, 1,
                                    NP // P - 1)[1]
    return lax.cond(jnp.any(need > 0), older, lambda: acc)


def _cmp_weights(w, pe):
    ratio = CMP_BLOCK // CMP_STRIDE
    w5 = w.reshape(2, ratio, CMP_STRIDE, HEAD_DIM, HEAD_DIM)
    eye = jnp.eye(NSA_KV, dtype=w.dtype)
    wbd = jnp.einsum('krcde,gh->krcgdhe', w5, eye).reshape(2, ratio, CMP_STRIDE * KV_W, KV_W)
    pe2 = jnp.broadcast_to(pe.reshape(ratio, CMP_STRIDE, 1, HEAD_DIM), (ratio, CMP_STRIDE, NSA_KV, HEAD_DIM))
    return wbd.astype(BF16), jnp.swapaxes(wbd, 2, 3).astype(BF16), pe2.reshape(ratio, 1, CMP_STRIDE * KV_W)


def _overlap(n_cmp_pad, n_slc_pad):
    n = np.arange(n_cmp_pad)[:, None] * CMP_STRIDE
    j = np.arange(n_slc_pad)[None, :] * SLC_BLOCK
    return jnp.asarray((n < j + SLC_BLOCK) & (n + CMP_BLOCK > j), dtype=BF16)


def _suffix_matrix(n):
    return jnp.asarray(np.arange(n)[:, None] > np.arange(n)[None, :], dtype=BF16)


def _pair_suffix_matrix(n):
    u = (np.arange(n)[:, None] > np.arange(n)[None, :]).astype(np.float32)
    one, z = np.ones((n, n), np.float32), np.zeros((n, n), np.float32)
    return jnp.asarray(np.block([[u, z, one, z], [z, u, z, one]]), dtype=BF16)


def _feature_major_state(xt, heads):
    d, b, _, t = xt.shape
    return xt.reshape(d, b, heads, HEAD_DIM, t).transpose(0, 1, 4, 2, 3)


def kernel(x_prompt, x_sample, mem_prompt, cache_nsa_cmp_k, cache_nsa_cmp_v, cache_nsa_slc_k, cache_nsa_slc_v, cache_nsa_win_k, cache_nsa_win_v, cache_sb_k, cache_sb_v, cache_mem_k, cache_mem_v, state_lru_h, state_lru_conv, page_table, w_in, nsa_cmp_w, nsa_cmp_pe, lru_conv_w, lru_conv_b, lru_wa, lru_ba, lru_wx, lru_bx, lru_lambda, w_mem_kv, w_branch, w_out, ln1_g, ln1_b, ln2_g, ln2_b, ffn_w13, ffn_w2, moe_router_w, moe_router_b, moe_w13, moe_w2):
    depth = w_in.shape[0]
    B, T, _ = x_prompt.shape
    Bd, Td, _ = x_sample.shape
    NP = page_table.shape[1]
    page = cache_nsa_cmp_k.shape[2]
    past = NP * page
    win_buf = cache_nsa_win_k.shape[2]
    n_mem = mem_prompt.shape[1]
    alpha = (2.0 * depth) ** 0.25
    assert page == LANES and T % 512 == 0 and T // SLC_BLOCK >= SLC_TOP and T // SLC_BLOCK <= LANES
    assert win_buf == WINDOW and Td < CMP_STRIDE and NP % 8 == 0
    P = 8
    rows = NSA_HEADS * Td

    fm = lambda c: c.transpose(0, 1, 3, 4, 2).reshape(c.shape[0], c.shape[1], c.shape[3] * c.shape[4], c.shape[2])
    slc_kc, slc_vc, sb_kc, sb_vc = fm(cache_nsa_slc_k), fm(cache_nsa_slc_v), fm(cache_sb_k), fm(cache_sb_v)
    win_kc, win_vc = fm(cache_nsa_win_k), fm(cache_nsa_win_v)
    cmp_kc, cmp_vc = fm(cache_nsa_cmp_k), fm(cache_nsa_cmp_v)
    P_cmp = 16 if NP % 16 == 0 else 8

    n_cmp_p = T // CMP_STRIDE
    ovl_p = _overlap(n_cmp_p, LANES)
    n_slc_s = -(-(past + Td) // SLC_BLOCK)
    nsp_s = -(-n_slc_s // LANES) * LANES
    n_cmp_s = past // CMP_STRIDE
    ovl_s = _overlap(n_cmp_s, nsp_s)
    u_blk = _suffix_matrix(LANES)
    u_pair = _pair_suffix_matrix(LANES)
    hs = jnp.asarray((np.arange(rows)[:, None] // (NSA_GROUP * Td) == np.arange(rows)[None, :] // (NSA_GROUP * Td))
                     & (np.arange(rows)[:, None] % Td == np.arange(rows)[None, :] % Td), dtype=BF16)

    xp = x_prompt
    xs = x_sample.reshape(1, Bd * Td, D_MODEL)
    mem2 = mem_prompt.reshape(1, B * n_mem, D_MODEL)
    p_layers, s_layers = [], []
    for l in range(depth):
        wt = jnp.swapaxes(w_in[l], 0, 1)
        c = np.cumsum([0, 512, 128, 128, 128, 128, 128, 128, 24, 512, 512, 512, 512, 512, 512, 4096])
        seg = lambda a, b: wt[c[a]:c[b]]
        ng_w = jnp.pad(seg(7, 8), ((0, LANES - 24), (0, 0)))
        wn = jnp.concatenate([seg(0, 1), ng_w, seg(8, 10), seg(10, 11), seg(13, 14), seg(1, 3)], axis=0)
        wn = jnp.swapaxes(wn, 0, 1).astype(BF16)
        wtr = jnp.concatenate([seg(1, 7), seg(11, 13)], axis=0).astype(BF16)
        w_all = jnp.swapaxes(jnp.concatenate([seg(0, 7), ng_w, seg(8, 14)], axis=0), 0, 1).astype(BF16)
        wmg = jnp.swapaxes(seg(14, 15), 0, 1).astype(BF16)
        cw_tok, cw_feat, pe2 = _cmp_weights(nsa_cmp_w[l], nsa_cmp_pe[l])
        eye8 = jnp.eye(LRU_BLOCKS, dtype=F32)
        bd = lambda w: jnp.einsum('ncd,nm->ncmd', w, eye8).reshape(D_RNN, D_RNN)
        wbd = jnp.concatenate([bd(lru_wa[l]), bd(lru_wx[l])], axis=1).astype(BF16)
        gbias = jnp.concatenate([lru_ba[l].reshape(1, D_RNN), lru_bx[l].reshape(1, D_RNN)], axis=1)
        lam = lru_lambda[l].reshape(1, D_RNN)
        conv_w, conv_b = lru_conv_w[l], lru_conv_b[l].reshape(1, D_RNN)
        wmem = w_mem_kv[l].astype(BF16)
        wbr, wo = w_branch[l].astype(BF16), w_out[l].astype(BF16)
        g1, b1 = ln1_g[l].reshape(1, D_MODEL), ln1_b[l].reshape(1, D_MODEL)
        g2, b2 = ln2_g[l].reshape(1, D_MODEL), ln2_b[l].reshape(1, D_MODEL)

        ncols = [(0, 512), (512, 128), (640, 512), (1152, 512), (1664, 512), (2176, 512), (2688, 128), (2816, 128)]
        tcols = [(0, 128, ("full",)), (128, 128, ("full",)), (256, 128, ("full", "blk")), (384, 128, ("full", "blk")),
                 (512, 128, ("full", "blk")), (640, 128, ("full", "blk")), (768, 512, ("full", "blk")),
                 (1280, 512, ("full", "blk"))]
        (nq, ng, lx, lg, sq, mq, ck_tok, cv_tok, ck_t, cv_t, sk_t, sk_b, sv_t, sv_b, wk_t, wk_b, wv_t, wv_b,
         sbk_t, sbk_b, sbv_t, sbv_b) = _proj(xp, wn, ncols, wtr, tcols)
        ckc, cvc = _compress(ck_tok.reshape(B, n_cmp_p, CMP_STRIDE * KV_W), cv_tok.reshape(B, n_cmp_p, CMP_STRIDE * KV_W),
                             cw_feat[0], cw_feat[1], pe2)
        o_nsa = _nsa_prompt(nq, ng, ckc, cvc, sk_b, sv_b, wk_b, wv_b, ovl_p)
        o_lru, h_new, conv_new = _rglru(lx, lg, jnp.zeros((B, CONV_W - 1, D_RNN), F32), jnp.zeros((B, 1, D_RNN), F32),
                                        conv_w, conv_b, wbd, gbias, lam)
        o_sb = _sb_prompt(sq, sbk_b, sbv_b, u_pair)
        mk, mv = _proj(mem2, wmem, [(0, 512), (512, 512)])
        mk, mv = mk.reshape(B, n_mem, 512), mv.reshape(B, n_mem, 512)
        o_mem = _mem_attn(mq, mk, mv)
        M = B * T
        x1 = _merge_ln(xp.reshape(M, D_MODEL), [o.reshape(M, 512) for o in (o_nsa, o_lru, o_sb, o_mem)],
                       wmg, wbr, wo, g1, b1, alpha)
        p_layers.append((ck_t, cv_t, sk_t, sv_t, wk_t[:, :, T - min(WINDOW, T):], wv_t[:, :, T - min(WINDOW, T):],
                         sbk_t, sbv_t, h_new.reshape(B, D_RNN), conv_new,
                         mk.reshape(B, n_mem, MEM_HEADS, MEM_HEAD_DIM), mv.reshape(B, n_mem, MEM_HEADS, MEM_HEAD_DIM)))

        ms = Bd * Td
        scols = [(0, 512), (512, 128), (640, 128), (768, 128), (896, 128), (1024, 128), (1152, 128), (1280, 128),
                 (1408, 512), (1920, 512), (2432, 512), (2944, 512), (3456, 512), (3968, 512)]
        (nq_s, ck_s, cv_s, sk_s, sv_s, wk_s, wv_s, ng_s, lx_s, lg_s, sq_s, sbk_s, sbv_s, mq_s) = [
            a.reshape(Bd, Td, a.shape[-1]) for a in _proj(xs, w_all, scols)]
        ak, bk, av, bv = _compress_paged(cmp_kc, cmp_vc, l, page_table, cw_tok[0], cw_tok[1], pe2, P_cmp)
        q8 = nq_s.reshape(Bd, Td, NSA_HEADS, HEAD_DIM).transpose(0, 2, 1, 3)
        gsel = jnp.asarray(np.arange(NSA_HEADS)[:, None] // NSA_GROUP == np.arange(NSA_KV)[None, :], dtype=F32)
        qz = (q8[:, :, :, None, :] * gsel[None, :, None, :, None]).reshape(Bd, rows, KV_W)
        hsel = jnp.eye(SB_HEADS, dtype=F32)
        s8 = sq_s.reshape(Bd, Td, SB_HEADS, HEAD_DIM).transpose(0, 2, 1, 3)
        qbd = (s8[:, :, :, None, :] * hsel[None, :, None, :, None]).reshape(Bd, rows, SB_W)
        graw = jnp.pad(ng_s[:, :, :24].reshape(Bd, Td, NSA_HEADS, 3).transpose(0, 2, 1, 3).reshape(Bd, rows, 3),
                       ((0, 0), (0, 0), (0, LANES - 3)))
        new_fm = lambda a: jnp.pad(jnp.swapaxes(a, 1, 2), ((0, 0), (0, 0), (0, LANES - Td)))
        ocmp, sel = _nsa_s_cmp(qz, ak, bk, av, bv, ovl_s, hs, Td, past, n_slc_s)
        wkx = jnp.concatenate([win_kc[l], new_fm(wk_s)], axis=2)
        wvx = jnp.concatenate([win_vc[l], new_fm(wv_s)], axis=2)
        o_rows = _nsa_s_slc(page_table, l, slc_kc, slc_vc, qz, graw, ocmp, sel, new_fm(sk_s), new_fm(sv_s),
                            wkx, wvx, Td, past, P)
        o6 = o_rows.reshape(Bd, NSA_KV, NSA_GROUP, Td, NSA_KV, HEAD_DIM)
        o_nsa_s = jnp.stack([o6[:, g, :, :, g, :] for g in range(NSA_KV)], axis=1)
        o_nsa_s = o_nsa_s.reshape(Bd, NSA_HEADS, Td, HEAD_DIM).transpose(0, 2, 1, 3).reshape(Bd, Td, NSA_HEADS * HEAD_DIM)
        o_lru_s, h_new_s, conv_new_s = _rglru(lx_s, lg_s, state_lru_conv[l], state_lru_h[l].reshape(Bd, 1, D_RNN),
                                              conv_w, conv_b, wbd, gbias, lam)
        sb_rows = _sb_sample(page_table, l, sb_kc, sb_vc, qbd, new_fm(sbk_s), new_fm(sbv_s), u_blk, Td, P)
        sb5 = sb_rows.reshape(Bd, SB_HEADS, Td, SB_HEADS, HEAD_DIM)
        o_sb_s = jnp.stack([sb5[:, h, :, h, :] for h in range(SB_HEADS)], axis=2).reshape(Bd, Td, SB_W)
        o_mem_s = _mem_attn(mq_s, cache_mem_k[l].reshape(Bd, n_mem, 512), cache_mem_v[l].reshape(Bd, n_mem, 512))
        x1s = _merge_ln(xs.reshape(ms, D_MODEL), [o.reshape(ms, 512) for o in (o_nsa_s, o_lru_s, o_sb_s, o_mem_s)],
                        wmg, wbr, wo, g1, b1, alpha)
        kv4 = lambda a: a.reshape(Bd, Td, NSA_KV, HEAD_DIM)
        s_layers.append((kv4(ck_s), kv4(cv_s), kv4(sk_s), kv4(sv_s), kv4(wk_s), kv4(wv_s),
                         sbk_s.reshape(Bd, Td, SB_HEADS, HEAD_DIM), sbv_s.reshape(Bd, Td, SB_HEADS, HEAD_DIM),
                         h_new_s.reshape(Bd, D_RNN), conv_new_s))

        if l % 2 == 0:
            w13, w2 = ffn_w13[l // 2].astype(BF16), ffn_w2[l // 2].astype(BF16)
            x2 = _ffn_ln(x1, w13, w2, g2, b2, alpha)
            x2s = _ffn_ln(x1s, w13, w2, g2, b2, alpha)
        else:
            rw = jnp.pad(moe_router_w[l // 2], ((0, 0), (0, LANES - N_EXP)))
            rwh = rw.astype(BF16)
            rwl = (rw - rwh.astype(F32)).astype(BF16)
            rb = jnp.pad(moe_router_b[l // 2].reshape(1, N_EXP), ((0, 0), (0, LANES - N_EXP)))
            w13, w2 = moe_w13[l // 2].astype(BF16), moe_w2[l // 2].astype(BF16)
            x2 = _moe_ln(x1, rwh, rwl, rb, w13, w2, g2, b2, alpha)
            x2s = _moe_ln(x1s, rwh, rwl, rb, w13, w2, g2, b2, alpha)
        xp = x2.reshape(B, T, D_MODEL)
        xs = x2s.reshape(1, ms, D_MODEL)

    pst = [jnp.stack(r, axis=0) for r in zip(*p_layers)]
    prompt_state = ([_feature_major_state(a, NSA_KV) for a in pst[:6]] + [_feature_major_state(a, SB_HEADS) for a in pst[6:8]]
                    + pst[8:])
    sample_state = [jnp.stack(r, axis=0) for r in zip(*s_layers)]
    return (xp, xs.reshape(Bd, Td, D_MODEL), *prompt_state, *sample_state)
```

```python
import functools

import numpy as np
import jax
import jax.numpy as jnp
from jax import lax
from jax.experimental import pallas as pl
from jax.experimental.pallas import tpu as pltpu

F32 = jnp.float32
BF16 = jnp.bfloat16

D_MODEL = 1024
HEAD_DIM = 64
NSA_HEADS = 8
NSA_KV = 2
NSA_GROUP = NSA_HEADS // NSA_KV
CMP_STRIDE = 16
CMP_BLOCK = 32
SLC_BLOCK = 64
SLC_TOP = 16
WINDOW = 512
FORCE_SCORE = 1.0e4
SB_HEADS = 8
D_RNN = 512
LRU_BLOCKS = 8
CONV_W = 4
LRU_C = 8.0
MEM_HEADS = 4
MEM_HEAD_DIM = 128
N_EXP = 8
LN_EPS = 1e-5
NEG = -1.0e30
TINY = 1.0e-30
SB_DEAD_LOG = -106.0
KV_W = NSA_KV * HEAD_DIM
SB_W = SB_HEADS * HEAD_DIM
LANES = 128
VMEM_LIMIT = 56 * 1024 * 1024
SLOPES = tuple(float(2.0 ** (-8.0 * (h + 1) / NSA_HEADS)) for h in range(NSA_HEADS))


def _cparams(sem, big=False):
    return pltpu.CompilerParams(dimension_semantics=sem, vmem_limit_bytes=VMEM_LIMIT if big else None)


def _dot(a, b):
    return jnp.dot(a, b, preferred_element_type=F32)


def _dot_nt(a, b):
    return lax.dot_general(a, b, (((1,), (1,)), ((), ())), preferred_element_type=F32)


def _split2(x):
    hi = x.astype(BF16)
    lo = (x - hi.astype(F32)).astype(BF16)
    return hi, lo


def _split3(x):
    hi = x.astype(BF16)
    r = x - hi.astype(F32)
    mid = r.astype(BF16)
    lo = (r - mid.astype(F32)).astype(BF16)
    return hi, mid, lo


def _softplus(x):
    return jnp.maximum(x, 0.0) + jnp.log1p(jnp.exp(-jnp.abs(x)))


def _layer_norm(z, g, b):
    mu = jnp.mean(z, axis=-1, keepdims=True)
    zc = z - mu
    var = jnp.mean(zc * zc, axis=-1, keepdims=True)
    return zc * lax.rsqrt(var + LN_EPS) * g + b


def _masked_softmax(s, mask):
    s = jnp.where(mask, s, NEG)
    m = jnp.max(s, axis=-1, keepdims=True)
    e = jnp.where(mask, jnp.exp(s - m), 0.0)
    return e / jnp.maximum(jnp.sum(e, axis=-1, keepdims=True), TINY)


def _topk_mask(score, lanef, k):
    sel = jnp.zeros_like(score)
    s = score
    for _ in range(k):
        m = jnp.max(s, axis=-1, keepdims=True)
        jm = jnp.min(jnp.where(s == m, lanef, 1.0e9), axis=-1, keepdims=True)
        hit = lanef == jm
        sel = jnp.where(hit, 1.0, sel)
        s = jnp.where(hit, -3.0, s)
    return sel


def _topk_rank(score_t, k):
    R, N = score_t.shape
    nv = R // 8
    groups = [score_t[8 * v:8 * v + 8] for v in range(nv)]
    sub = lax.broadcasted_iota(jnp.int32, (8, N), 0)
    ranks = [jnp.zeros((8, N), F32) for _ in range(nv)]
    for jp in range(R):
        sj = score_t[jp:jp + 1, :]
        vj, rj = divmod(jp, 8)
        for v in range(nv):
            if v > vj:
                beats = sj >= groups[v]
            elif v < vj:
                beats = sj > groups[v]
            else:
                beats = (sj > groups[v]) | ((sj == groups[v]) & (sub > rj))
            ranks[v] = ranks[v] + jnp.where(beats, 1.0, 0.0)
    return jnp.concatenate([jnp.where(r < k, 1.0, 0.0) for r in ranks], axis=0)


def _proj_body(*refs, ncols, tcols, has_t):
    x_ref, wn_ref = refs[0], refs[1]
    o_refs = refs[3:] if has_t else refs[2:]
    xb = x_ref[...].astype(BF16)
    k = 0
    for c0, wd in ncols:
        o_refs[k][...] = _dot(xb, wn_ref[:, c0:c0 + wd]).astype(o_refs[k].dtype)
        k += 1
    if has_t:
        wt_ref = refs[2]
        for c0, wd, kinds in tcols:
            r = _dot_nt(wt_ref[c0:c0 + wd, :], xb)
            for kind in kinds:
                if kind == "full":
                    o_refs[k][...] = r
                else:
                    rb = r.astype(BF16)
                    for c in range(rb.shape[1] // LANES):
                        o_refs[k][c] = rb[:, c * LANES:(c + 1) * LANES]
                k += 1


def _proj(x, wn, ncols, wt=None, tcols=(), tm=512):
    B, T, K = x.shape
    tm = min(tm, T)
    has_t = wt is not None
    in_specs = [pl.BlockSpec((None, tm, K), lambda b, t: (b, t, 0)),
                pl.BlockSpec(wn.shape, lambda b, t: (0, 0))]
    args = [x, wn]
    if has_t:
        in_specs.append(pl.BlockSpec(wt.shape, lambda b, t: (0, 0)))
        args.append(wt)
    out_specs, out_shape = [], []
    for _, wd in ncols:
        out_specs.append(pl.BlockSpec((None, tm, wd), lambda b, t: (b, t, 0)))
        out_shape.append(jax.ShapeDtypeStruct((B, T, wd), F32))
    for _, wd, kinds in tcols:
        for kind in kinds:
            if kind == "full":
                out_specs.append(pl.BlockSpec((None, wd, tm), lambda b, t: (b, 0, t)))
                out_shape.append(jax.ShapeDtypeStruct((B, wd, T), F32))
            else:
                nb = tm // LANES
                out_specs.append(pl.BlockSpec((None, nb, wd, LANES), lambda b, t: (b, t, 0, 0)))
                out_shape.append(jax.ShapeDtypeStruct((B, T // LANES, wd, LANES), BF16))
    return pl.pallas_call(
        functools.partial(_proj_body, ncols=tuple(ncols), tcols=tuple(tcols), has_t=has_t),
        grid=(B, T // tm), in_specs=in_specs, out_specs=out_specs, out_shape=out_shape,
        compiler_params=_cparams(("parallel", "parallel"), big=True), name="proj")(*args)


def _compress_body(xk_ref, xv_ref, wk_ref, wv_ref, pe_ref, ok_ref, ov_ref):
    n = xk_ref.shape[0]
    colid = lax.broadcasted_iota(jnp.int32, (KV_W, n), 1)
    for x_ref, w_ref, o_ref in ((xk_ref, wk_ref, ok_ref), (xv_ref, wv_ref, ov_ref)):
        x = x_ref[...]
        a = _dot_nt(w_ref[0], (x + pe_ref[0]).astype(BF16))
        b = _dot_nt(w_ref[1], (x + pe_ref[1]).astype(BF16))
        o_ref[...] = jnp.where(colid < n - 1, a + pltpu.roll(b, n - 1, 1), 0.0)


def _compress(xk, xv, wk, wv, pe):
    B, n, cw = xk.shape
    xspec = pl.BlockSpec((None, n, cw), lambda b: (b, 0, 0))
    wspec = pl.BlockSpec(wk.shape, lambda b: (0, 0, 0))
    ospec = pl.BlockSpec((None, KV_W, n), lambda b: (b, 0, 0))
    return pl.pallas_call(
        _compress_body, grid=(B,),
        in_specs=[xspec, xspec, wspec, wspec, pl.BlockSpec(pe.shape, lambda b: (0, 0, 0))],
        out_specs=[ospec, ospec], out_shape=[jax.ShapeDtypeStruct((B, KV_W, n), F32)] * 2,
        compiler_params=_cparams(("parallel",), big=True), name="compress")(xk, xv, wk, wv, pe)


def _compress_paged_body(pt_ref, *refs, P):
    k_refs, v_refs = refs[:P], refs[P:2 * P]
    wk_ref, wv_ref, pe_ref, ak_ref, bk_ref, av_ref, bv_ref, tok_k, tok_v = refs[2 * P:]
    page = k_refs[0].shape[1]
    nch = P * page // CMP_STRIDE
    for i in range(P):
        tok_k[i * page:(i + 1) * page, :] = k_refs[i][...].T
        tok_v[i * page:(i + 1) * page, :] = v_refs[i][...].T
    xs = [jnp.concatenate([tok[pl.ds(c, nch, stride=CMP_STRIDE), :] for c in range(CMP_STRIDE)], axis=1)
          for tok in (tok_k, tok_v)]
    lhs = [(x + pe_ref[r]).astype(BF16) for x in xs for r in range(2)]
    outs = [_dot(lhs[0], wk_ref[0]), _dot(lhs[1], wk_ref[1]), _dot(lhs[2], wv_ref[0]), _dot(lhs[3], wv_ref[1])]
    for o_ref, o in zip((ak_ref, bk_ref, av_ref, bv_ref), outs):
        o_ref[...] = o


def _compress_paged(cache_k, cache_v, layer, page_table, wk, wv, pe, P):
    Bd, NP = page_table.shape
    page = cache_k.shape[-1]
    nch = P * page // CMP_STRIDE
    pspec = [pl.BlockSpec((None, None, KV_W, page), lambda b, s, pt, i=i: (layer, pt[b, s * P + i], 0, 0))
             for i in range(P)]
    const = lambda a: pl.BlockSpec(a.shape, lambda b, s, pt: (0, 0, 0))
    ospec = pl.BlockSpec((None, nch, KV_W), lambda b, s, pt: (b, s, 0))
    grid_spec = pltpu.PrefetchScalarGridSpec(num_scalar_prefetch=1, grid=(Bd, NP // P),
                                             in_specs=pspec + pspec + [const(wk), const(wv), const(pe)],
                                             out_specs=[ospec] * 4,
                                             scratch_shapes=[pltpu.VMEM((P * page, KV_W), F32)] * 2)
    return pl.pallas_call(
        functools.partial(_compress_paged_body, P=P), grid_spec=grid_spec,
        out_shape=[jax.ShapeDtypeStruct((Bd, NP * page // CMP_STRIDE, KV_W), F32)] * 4,
        compiler_params=_cparams(("parallel", "parallel"), big=True), name="compress_paged")(
            page_table, *([cache_k] * P), *([cache_v] * P), wk, wv, pe)


def _nsa_prompt_body(q_ref, ng_ref, ckt_ref, cvt_ref, sk_ref, sv_ref, wk_ref, wv_ref, ovl_ref, o_ref,
                     *, Q, KBN, n_slc):
    i = pl.program_id(1)
    q0 = i * Q
    KB = KBN * LANES
    QB = Q // LANES
    lane = lax.broadcasted_iota(jnp.int32, (Q, LANES), 1)
    lo_half = lane < HEAD_DIM
    qpos = q0 + lax.broadcasted_iota(jnp.int32, (Q, 1), 0)
    gate = jax.nn.sigmoid(ng_ref[...])
    slabs = [q_ref[:, LANES * s:LANES * (s + 1)] * (HEAD_DIM ** -0.5) for s in range(4)]

    def placed(h, g):
        sl = slabs[h // 2]
        if (h % 2) != g:
            sl = pltpu.roll(sl, HEAD_DIM, 1)
        return jnp.where(lo_half if g == 0 else jnp.logical_not(lo_half), sl, 0.0)

    ckt = ckt_ref[...].astype(BF16)
    cvt = cvt_ref[...].astype(BF16)
    NC = ckt.shape[1]
    cend = CMP_STRIDE * lax.broadcasted_iota(jnp.int32, (1, NC), 1) + (CMP_BLOCK - 1)
    dist_c = qpos - cend
    vis_c = dist_c >= 0
    dist_cf = dist_c.astype(F32)
    blk_t = lax.broadcasted_iota(jnp.int32, (n_slc, NSA_KV * Q), 0)
    col_t = lax.broadcasted_iota(jnp.int32, (n_slc, NSA_KV * Q), 1)
    qpos_t = q0 + (col_t & (Q - 1))
    cur_t = qpos_t >> 6
    valid_t = blk_t * SLC_BLOCK <= qpos_t
    forced_t = (blk_t == 0) | (blk_t == cur_t) | (blk_t == cur_t - 1)
    rows_of = [slice(h * Q, (h + 1) * Q) for h in range(NSA_HEADS)]
    group_of = [h // NSA_GROUP for h in range(NSA_HEADS)]
    qall = jnp.concatenate([placed(h, group_of[h]) for h in range(NSA_HEADS)], axis=0).astype(BF16)

    def softmax_rows(sd, distf, mask):
        ps = []
        for h in range(NSA_HEADS):
            s = jnp.where(mask, sd[rows_of[h]] - SLOPES[h] * distf, NEG)
            e = jnp.where(mask, jnp.exp(s - jnp.max(s, axis=-1, keepdims=True)), 0.0)
            ps.append(e / jnp.maximum(jnp.sum(e, axis=-1, keepdims=True), TINY))
        return ps

    sb0 = jnp.maximum(i * QB - WINDOW // LANES, 0)
    nwb = WINDOW // LANES + QB
    kt_w = jnp.concatenate([wk_ref[sb0 + c] for c in range(nwb)], axis=1)
    vt_w = jnp.concatenate([wv_ref[sb0 + c] for c in range(nwb)], axis=1)
    dist_w = qpos - (sb0 * LANES + lax.broadcasted_iota(jnp.int32, (1, nwb * LANES), 1))
    sd_c = _dot(qall, ckt)
    sd_w = _dot(qall, kt_w)
    ps = softmax_rows(sd_c, dist_cf, vis_c)
    pw = softmax_rows(sd_w, dist_w.astype(F32), (dist_w >= 0) & (dist_w <= WINDOW))
    o_cmp = _dot_nt(jnp.concatenate(ps, axis=0).astype(BF16), cvt)
    o_win = _dot_nt(jnp.concatenate(pw, axis=0).astype(BF16), vt_w)
    psum = jnp.concatenate([(ps[NSA_GROUP * g] + ps[NSA_GROUP * g + 1]) + (ps[NSA_GROUP * g + 2] + ps[NSA_GROUP * g + 3])
                            for g in range(NSA_KV)], axis=0)
    ph, pl_ = _split2(psum)
    imp = _dot(ph, ovl_ref[...]) + _dot(pl_, ovl_ref[...])
    score_t = jnp.where(valid_t, jnp.where(forced_t, FORCE_SCORE, imp.T[:n_slc]), -1.0)
    sel_t = _topk_rank(score_t, SLC_TOP)
    if n_slc < LANES:
        sel_t = jnp.concatenate([sel_t, jnp.zeros((LANES - n_slc, NSA_KV * Q), F32)], axis=0)
    sel_b = sel_t.T.astype(BF16)
    blk_row = lax.broadcasted_iota(jnp.int32, (LANES, KB), 0)

    half_rows = lax.broadcasted_iota(jnp.int32, (LANES, KB), 0) < HEAD_DIM

    def kt_body(t, carry):
        m, acc = carry
        kt = jnp.concatenate([sk_ref[t * KBN + c] for c in range(KBN)], axis=1)
        vt = jnp.concatenate([sv_ref[t * KBN + c] for c in range(KBN)], axis=1)
        one = jnp.ones_like(vt)
        vts = [jnp.where(half_rows, vt, one), jnp.where(half_rows, one, vt)]
        kpos = t * KB + lax.broadcasted_iota(jnp.int32, (1, KB), 1)
        expand = ((kpos >> 6) == blk_row).astype(BF16)
        sd = _dot(qall, kt)
        selx = _dot(sel_b, expand)
        dist = qpos - kpos
        causal = dist >= 0
        masks = [(selx[g * Q:(g + 1) * Q] > 0.5) & causal for g in range(NSA_KV)]
        distf = dist.astype(F32)
        es, alphas, ms = [], [], []
        for h in range(NSA_HEADS):
            mask = masks[group_of[h]]
            s = jnp.where(mask, sd[rows_of[h]] - SLOPES[h] * distf, NEG)
            m_old = m[rows_of[h]]
            m_new = jnp.maximum(m_old, jnp.max(s, axis=-1, keepdims=True))
            alphas.append(jnp.exp(m_old - m_new))
            ms.append(m_new)
            es.append(jnp.where(mask, jnp.exp(s - m_new), 0.0).astype(BF16))
        pv = jnp.concatenate([_dot_nt(jnp.concatenate(es[NSA_GROUP * g:NSA_GROUP * (g + 1)], axis=0), vts[g])
                              for g in range(NSA_KV)], axis=0)
        return jnp.concatenate(ms, axis=0), jnp.concatenate(alphas, axis=0) * acc + pv

    nkt = (q0 + Q + KB - 1) // KB
    init = (jnp.full((NSA_HEADS * Q, 1), NEG, F32), jnp.zeros((NSA_HEADS * Q, LANES), F32))
    _, acc_s = lax.fori_loop(0, nkt, kt_body, init)
    o_slc = acc_s / jnp.maximum(pltpu.roll(acc_s, HEAD_DIM, 1), TINY)
    outs = [None] * NSA_HEADS
    for h in range(NSA_HEADS):
        rows = rows_of[h]
        oh = (gate[:, 3 * h:3 * h + 1] * o_cmp[rows] + gate[:, 3 * h + 1:3 * h + 2] * o_slc[rows]
              + gate[:, 3 * h + 2:3 * h + 3] * o_win[rows])
        if (h % 2) != group_of[h]:
            oh = pltpu.roll(oh, HEAD_DIM, 1)
        outs[h] = oh
    for s in range(4):
        o_ref[:, LANES * s:LANES * (s + 1)] = jnp.where(lo_half, outs[2 * s], outs[2 * s + 1])


def _nsa_prompt(q, ng, ckt, cvt, skb, svb, wkb, wvb, ovl, Q=128, KBN=4):
    B, T, _ = q.shape
    n_slc = T // SLC_BLOCK
    NC = ckt.shape[2]
    nblk = T // LANES
    kvspec = pl.BlockSpec((None, nblk, KV_W, LANES), lambda b, i: (b, 0, 0, 0))
    cspec = pl.BlockSpec((None, KV_W, NC), lambda b, i: (b, 0, 0))
    return pl.pallas_call(
        functools.partial(_nsa_prompt_body, Q=Q, KBN=KBN, n_slc=n_slc), grid=(B, T // Q),
        in_specs=[pl.BlockSpec((None, Q, NSA_HEADS * HEAD_DIM), lambda b, i: (b, i, 0)),
                  pl.BlockSpec((None, Q, LANES), lambda b, i: (b, i, 0)),
                  cspec, cspec, kvspec, kvspec, kvspec, kvspec,
                  pl.BlockSpec(ovl.shape, lambda b, i: (0, 0))],
        out_specs=pl.BlockSpec((None, Q, NSA_HEADS * HEAD_DIM), lambda b, i: (b, i, 0)),
        out_shape=jax.ShapeDtypeStruct((B, T, NSA_HEADS * HEAD_DIM), F32),
        compiler_params=_cparams(("parallel", "parallel"), big=True), name="nsa_prompt")(
            q, ng, ckt, cvt, skb, svb, wkb, wvb, ovl)


def _sb_block(qh, kt, vt, u, carry, acc, mask):
    n = qh.shape[0]
    s = _dot(qh, kt)
    sp = _softplus(s)
    l1m = -sp if mask is None else jnp.where(mask, -sp, 0.0)
    hi, lo = _split2(l1m)
    r2 = _dot(jnp.concatenate([hi, lo], axis=0), u)
    after = r2[:n] + r2[n:]
    w = jnp.exp((s - sp) + after + carry)
    if mask is not None:
        w = jnp.where(mask, w, 0.0)
    acc = acc + _dot_nt(w.astype(BF16), vt)
    carry = carry + jnp.sum(l1m, axis=-1, keepdims=True)
    return carry, acc


def _sb_pairs_step(qs, kts, vts, u2_ref, row_lo, carries, accs, mask2):
    n = qs[0].shape[0]
    zero = jnp.zeros_like(kts[0])
    bd = lambda t: jnp.concatenate([jnp.where(row_lo, t, zero), jnp.where(row_lo, zero, t)], axis=1)
    ss = [_dot(q, bd(kt)) for q, kt in zip(qs, kts)]
    sps, cats = [], []
    for s in ss:
        sp = jnp.maximum(s, 0.0) + jnp.log(1.0 + jnp.exp(-jnp.abs(s)))
        l1m = -sp if mask2 is None else jnp.where(mask2, -sp, 0.0)
        hi, lo = _split2(l1m)
        sps.append(sp)
        cats.append(jnp.concatenate([hi, lo], axis=0))
    rs = [_dot(c, u2_ref[...]) for c in cats]
    ws, new_c = [], []
    for s, sp, r, carry2 in zip(ss, sps, rs, carries):
        r = r[:n] + r[n:]
        kb2 = s.shape[1]
        w = jnp.exp((s - sp) + r[:, :kb2] + carry2)
        if mask2 is not None:
            w = jnp.where(mask2, w, 0.0)
        ws.append(w.astype(BF16))
        new_c.append(carry2 + r[:, kb2:])
    return new_c, [acc + _dot_nt(w, bd(vt)) for acc, w, vt in zip(accs, ws, vts)]


def _sb_prompt_body(q_ref, kt_ref, vt_ref, u2_ref, o_ref, carry_s, acc_s, *, Q):
    i = pl.program_id(1)
    npair = SB_W // LANES
    tri = lax.broadcasted_iota(jnp.int32, (Q, Q), 1) < lax.broadcasted_iota(jnp.int32, (Q, Q), 0)
    mask2 = jnp.concatenate([tri, tri], axis=1)
    row_lo = lax.broadcasted_iota(jnp.int32, (LANES, Q), 0) < HEAD_DIM
    slab = [slice(LANES * p, LANES * (p + 1)) for p in range(npair)]

    def step(j, carries, accs, mask):
        qs = [(q_ref[:, slab[p]] * (HEAD_DIM ** -0.5)).astype(BF16) for p in range(npair)]
        new_c, new_a = _sb_pairs_step(qs, [kt_ref[j, slab[p], :] for p in range(npair)],
                                      [vt_ref[j, slab[p], :] for p in range(npair)], u2_ref, row_lo,
                                      carries, accs, mask)
        for p in range(npair):
            carry_s[p] = new_c[p]
            acc_s[p] = new_a[p]

    step(i, [jnp.zeros((Q, 2 * Q), F32)] * npair, [jnp.zeros((Q, LANES), F32)] * npair, mask2)

    def alive():
        mx = carry_s[0]
        for p in range(1, npair):
            mx = jnp.maximum(mx, carry_s[p])
        return (jnp.max(mx) > SB_DEAD_LOG).astype(jnp.int32)

    def body(st):
        jj, _ = st
        step(i - 1 - jj, [carry_s[p] for p in range(npair)], [acc_s[p] for p in range(npair)], None)
        return jj + 1, alive()

    lax.while_loop(lambda st: (st[0] < i) & (st[1] > 0), body, (jnp.int32(0), alive()))
    for p in range(npair):
        o_ref[:, slab[p]] = acc_s[p]


def _sb_prompt(q, ktb, vtb, u2):
    B, T, _ = q.shape
    Q = LANES
    nblk = T // LANES
    kvspec = pl.BlockSpec((None, nblk, SB_W, LANES), lambda b, i: (b, 0, 0, 0))
    return pl.pallas_call(
        functools.partial(_sb_prompt_body, Q=Q), grid=(B, T // Q),
        in_specs=[pl.BlockSpec((None, Q, SB_W), lambda b, i: (b, i, 0)), kvspec, kvspec,
                  pl.BlockSpec(u2.shape, lambda b, i: (0, 0))],
        out_specs=pl.BlockSpec((None, Q, SB_W), lambda b, i: (b, i, 0)),
        out_shape=jax.ShapeDtypeStruct((B, T, SB_W), F32),
        scratch_shapes=[pltpu.VMEM((SB_W // LANES, Q, 2 * Q), F32), pltpu.VMEM((SB_W // LANES, Q, LANES), F32)],
        compiler_params=_cparams(("parallel", "parallel"), big=True), name="sb_prompt")(q, ktb, vtb, u2)


def _mem_body(q_ref, k_ref, v_ref, o_ref):
    for h in range(MEM_HEADS):
        sl = slice(MEM_HEAD_DIM * h, MEM_HEAD_DIM * (h + 1))
        s = _dot_nt(q_ref[:, sl].astype(BF16), k_ref[:, sl].astype(BF16)) * (MEM_HEAD_DIM ** -0.5)
        e = jnp.exp(s - jnp.max(s, axis=-1, keepdims=True))
        p = e / jnp.sum(e, axis=-1, keepdims=True)
        o_ref[:, sl] = _dot(p.astype(BF16), v_ref[:, sl].astype(BF16))


def _mem_attn(q, mk, mv, tq=512):
    B, T, W = q.shape
    tq = min(tq, T)
    N = mk.shape[1]
    kvspec = pl.BlockSpec((None, N, W), lambda b, i: (b, 0, 0))
    return pl.pallas_call(
        _mem_body, grid=(B, T // tq),
        in_specs=[pl.BlockSpec((None, tq, W), lambda b, i: (b, i, 0)), kvspec, kvspec],
        out_specs=pl.BlockSpec((None, tq, W), lambda b, i: (b, i, 0)),
        out_shape=jax.ShapeDtypeStruct((B, T, W), F32),
        compiler_params=_cparams(("parallel", "parallel")), name="mem_attn")(q, mk, mv)


def _rglru_body(x_ref, g_ref, cb_ref, h0_ref, cw_ref, cbias_ref, wbd_ref, gb_ref, lam_ref,
                y_ref, hn_ref, cn_ref, xbuf, a_s, b_s, h_s, hc, *, n_t):
    t = pl.program_id(1)
    tt = x_ref.shape[0]
    hist = CONV_W - 1

    @pl.when(t == 0)
    def _():
        xbuf[8 - hist:8, :] = cb_ref[...]
        hc[...] = h0_ref[...]

    xbuf[8:8 + tt, :] = x_ref[...]
    xc = cw_ref[0:1, :] * xbuf[8 - hist:8 - hist + tt, :]
    for k in range(1, CONV_W):
        xc = xc + cw_ref[k:k + 1, :] * xbuf[8 - hist + k:8 - hist + k + tt, :]
    xc = xc + cbias_ref[...]
    gates = _dot(xc.astype(BF16), wbd_ref[...]) + gb_ref[...]
    r = jax.nn.sigmoid(gates[:, :D_RNN])
    ig = jax.nn.sigmoid(gates[:, D_RNN:])
    log_a = (-LRU_C * _softplus(-lam_ref[...])) * r
    a = jnp.exp(log_a)
    a_s[...] = a
    b_s[...] = jnp.sqrt(-jnp.tanh(log_a) * (a * a + 1.0)) * (ig * xc)

    def step(k, h):
        h = a_s[pl.ds(k, 1), :] * h + b_s[pl.ds(k, 1), :]
        h_s[pl.ds(k, 1), :] = h
        return h

    h = lax.fori_loop(0, tt, step, hc[...], unroll=8 if tt % 8 == 0 else tt)
    hc[...] = h
    y_ref[...] = h_s[...] * jax.nn.gelu(g_ref[...])
    hn_ref[...] = h
    cn_ref[...] = xbuf[8 + tt - hist:8 + tt, :]
    if n_t > 1:
        xbuf[0:8, :] = xbuf[tt:tt + 8, :]


def _rglru(x, gate_in, conv_buf, h0, conv_w, conv_b, wbd, gbias, lam, tt=256):
    B, T, W = x.shape
    tt = min(tt, T)
    n_t = T // tt
    xspec = pl.BlockSpec((None, tt, W), lambda b, t: (b, t, 0))
    const = lambda a: pl.BlockSpec(a.shape, lambda b, t: (0,) * a.ndim)
    return pl.pallas_call(
        functools.partial(_rglru_body, n_t=n_t), grid=(B, n_t),
        in_specs=[xspec, xspec, pl.BlockSpec((None, CONV_W - 1, W), lambda b, t: (b, 0, 0)),
                  pl.BlockSpec((None, 1, W), lambda b, t: (b, 0, 0)),
                  const(conv_w), const(conv_b), const(wbd), const(gbias), const(lam)],
        out_specs=[xspec, pl.BlockSpec((None, 1, W), lambda b, t: (b, 0, 0)),
                   pl.BlockSpec((None, CONV_W - 1, W), lambda b, t: (b, 0, 0))],
        out_shape=[jax.ShapeDtypeStruct((B, T, W), F32), jax.ShapeDtypeStruct((B, 1, W), F32),
                   jax.ShapeDtypeStruct((B, CONV_W - 1, W), F32)],
        scratch_shapes=[pltpu.VMEM((tt + 8, W), F32), pltpu.VMEM((tt, W), F32), pltpu.VMEM((tt, W), F32),
                        pltpu.VMEM((tt, W), F32), pltpu.VMEM((1, W), F32)],
        compiler_params=_cparams(("parallel", "arbitrary")), name="rglru")(
            x, gate_in, conv_buf, h0, conv_w, conv_b, wbd, gbias, lam)


def _merge_body(x_ref, b0, b1, b2, b3, wmg_ref, wbr_ref, wo_ref, g_ref, be_ref, o_ref, *, alpha):
    x = x_ref[...]
    xb = x.astype(BF16)
    acc = None
    off = 0
    for n, br in enumerate((b0, b1, b2, b3)):
        wdt = br.shape[-1]
        gate = jax.nn.sigmoid(_dot(xb, wmg_ref[:, n * D_MODEL:(n + 1) * D_MODEL]))
        term = gate * _dot(br[...].astype(BF16), wbr_ref[off:off + wdt, :])
        acc = term if acc is None else acc + term
        off += wdt
    mix = _dot(acc.astype(BF16), wo_ref[...])
    o_ref[...] = _layer_norm(alpha * x + mix, g_ref[...], be_ref[...])


def _merge_ln(x, branches, wmg, wbr, wo, g, be, alpha, tm=256):
    M = x.shape[0]
    tm = min(tm, M)
    row = lambda w: pl.BlockSpec((tm, w), lambda i: (i, 0))
    const = lambda a: pl.BlockSpec(a.shape, lambda i: (0, 0))
    return pl.pallas_call(
        functools.partial(_merge_body, alpha=alpha), grid=(M // tm,),
        in_specs=[row(D_MODEL)] + [row(b.shape[1]) for b in branches] + [const(wmg), const(wbr), const(wo),
                                                                         const(g), const(be)],
        out_specs=row(D_MODEL), out_shape=jax.ShapeDtypeStruct((M, D_MODEL), F32),
        compiler_params=_cparams(("parallel",), big=True), name="merge_ln")(x, *branches, wmg, wbr, wo, g, be)


def _ffn_body(x_ref, wa_ref, wb_ref, w2_ref, g_ref, be_ref, o_ref, acc, *, alpha, nf):
    j = pl.program_id(1)
    x = x_ref[...]
    xb = x.astype(BF16)
    a = _dot(xb, wa_ref[...])
    b = _dot(xb, wb_ref[...])
    part = _dot(((a * jax.nn.sigmoid(a)) * b).astype(BF16), w2_ref[...])

    @pl.when(j == 0)
    def _():
        acc[...] = part

    @pl.when(j > 0)
    def _():
        acc[...] += part

    @pl.when(j == nf - 1)
    def _():
        o_ref[...] = _layer_norm(alpha * x + acc[...], g_ref[...], be_ref[...])


def _ffn_ln(x, w13, w2, g, be, alpha, tm=512, tf=1408):
    M = x.shape[0]
    tm = min(tm, M)
    F = w2.shape[0]
    nf = F // tf
    const = lambda a: pl.BlockSpec(a.shape, lambda i, j: (0, 0))
    return pl.pallas_call(
        functools.partial(_ffn_body, alpha=alpha, nf=nf), grid=(M // tm, nf),
        in_specs=[pl.BlockSpec((tm, D_MODEL), lambda i, j: (i, 0)),
                  pl.BlockSpec((D_MODEL, tf), lambda i, j: (0, j)),
                  pl.BlockSpec((D_MODEL, tf), lambda i, j: (0, j + nf)),
                  pl.BlockSpec((tf, D_MODEL), lambda i, j: (j, 0)), const(g), const(be)],
        out_specs=pl.BlockSpec((tm, D_MODEL), lambda i, j: (i, 0)),
        out_shape=jax.ShapeDtypeStruct((M, D_MODEL), F32),
        scratch_shapes=[pltpu.VMEM((tm, D_MODEL), F32)],
        compiler_params=_cparams(("parallel", "arbitrary"), big=True), name="ffn_ln")(x, w13, w13, w2, g, be)


def _moe_body(x_ref, rwh_ref, rwl_ref, rb_ref, low_ref, wa_ref, wb_ref, w2_ref, g_ref, be_ref, o_ref,
              acc, comb, ind_c, pos_c, ind_r, pos_r, cnt_s, *, alpha, n_exp, chunk):
    e = pl.program_id(1)
    x = x_ref[...]
    xh = x.astype(BF16)
    tm = x.shape[0]
    lane = lax.broadcasted_iota(jnp.int32, (tm, LANES), 1)

    @pl.when(e == 0)
    def _():
        xl = (x - xh.astype(F32)).astype(BF16)
        logits = (_dot(xh, rwh_ref[...]) + _dot(xh, rwl_ref[...])) + _dot(xl, rwh_ref[...]) + rb_ref[...]
        valid = lane < n_exp
        lanef = lane.astype(F32)
        logits = jnp.where(valid, logits, NEG)
        ex = jnp.where(valid, jnp.exp(logits - jnp.max(logits, axis=-1, keepdims=True)), 0.0)
        p = ex / jnp.sum(ex, axis=-1, keepdims=True)
        p = jnp.where(valid, p, -1.0)
        m1 = jnp.max(p, axis=-1, keepdims=True)
        hit1 = lanef == jnp.min(jnp.where(p == m1, lanef, 1.0e9), axis=-1, keepdims=True)
        p2 = jnp.where(hit1, -1.0, p)
        m2 = jnp.max(p2, axis=-1, keepdims=True)
        hit2 = lanef == jnp.min(jnp.where(p2 == m2, lanef, 1.0e9), axis=-1, keepdims=True)
        den = m1 + m2
        comb[...] = jnp.where(hit1, m1 / den, 0.0) + jnp.where(hit2, m2 / den, 0.0)
        acc[...] = jnp.zeros_like(acc)
        ind = jnp.where(hit1 | hit2, 1.0, 0.0)
        pos = _dot(low_ref[...], ind.astype(BF16))
        ind_c[...] = ind
        pos_c[...] = pos
        ind_r[...] = ind.T
        pos_r[...] = pos.T
        for k in range(n_exp):
            cnt_s[k] = jnp.sum(ind[:, k:k + 1]).astype(jnp.int32)

    sel_lane = lane == e
    w_e = jnp.sum(jnp.where(sel_lane, comb[...], 0.0), axis=-1, keepdims=True)
    used_c = jnp.sum(jnp.where(sel_lane, ind_c[...], 0.0), axis=-1, keepdims=True) > 0.5
    slot_c = jnp.sum(jnp.where(sel_lane, pos_c[...], 0.0), axis=-1, keepdims=True)
    used_r = ind_r[pl.ds(e, 1), :] > 0.5
    slot_r = pos_r[pl.ds(e, 1), :]
    def chunk_runner(rows):
        row_id = lax.broadcasted_iota(jnp.int32, (rows, tm), 0).astype(F32)
        col_id = lax.broadcasted_iota(jnp.int32, (tm, rows), 1).astype(F32)

        def run_chunk(c, _):
            base = (c * rows).astype(F32)
            gather = jnp.where(used_r & (slot_r == base + row_id), 1.0, 0.0).astype(BF16)
            xs = _dot(gather, xh).astype(BF16)
            a = _dot(xs, wa_ref[...])
            b = _dot(xs, wb_ref[...])
            o = _dot(((a * jax.nn.sigmoid(a)) * b).astype(BF16), w2_ref[...])
            scatter = jnp.where(used_c & (slot_c == base + col_id), 1.0, 0.0).astype(BF16)
            oh, ol = _split2(o)
            acc[...] += w_e * (_dot(scatter, oh) + _dot(scatter, ol))
            return 0

        return run_chunk

    cnt = cnt_s[e]
    if tm >= 2 * chunk:
        one_big = (cnt > chunk) & (cnt <= 2 * chunk)

        @pl.when(one_big)
        def _():
            chunk_runner(2 * chunk)(jnp.int32(0), 0)

        @pl.when(jnp.logical_not(one_big))
        def _():
            lax.fori_loop(0, (cnt + chunk - 1) // chunk, chunk_runner(chunk), 0)
    else:
        lax.fori_loop(0, (cnt + chunk - 1) // chunk, chunk_runner(chunk), 0)

    @pl.when(e == n_exp - 1)
    def _():
        o_ref[...] = _layer_norm(alpha * x + acc[...], g_ref[...], be_ref[...])


def _moe_ln(x, rwh, rwl, rb, w13, w2, g, be, alpha, tm=512):
    M = x.shape[0]
    tm = min(tm, M)
    chunk = min(LANES, tm)
    n_exp, de, _ = w2.shape
    low = jnp.asarray(np.arange(tm)[:, None] > np.arange(tm)[None, :], dtype=BF16)
    const = lambda a: pl.BlockSpec(a.shape, lambda i, e: (0, 0))
    return pl.pallas_call(
        functools.partial(_moe_body, alpha=alpha, n_exp=n_exp, chunk=chunk), grid=(M // tm, n_exp),
        in_specs=[pl.BlockSpec((tm, D_MODEL), lambda i, e: (i, 0)), const(rwh), const(rwl), const(rb), const(low),
                  pl.BlockSpec((None, D_MODEL, de), lambda i, e: (e, 0, 0)),
                  pl.BlockSpec((None, D_MODEL, de), lambda i, e: (e, 0, 1)),
                  pl.BlockSpec((None, de, D_MODEL), lambda i, e: (e, 0, 0)), const(g), const(be)],
        out_specs=pl.BlockSpec((tm, D_MODEL), lambda i, e: (i, 0)),
        out_shape=jax.ShapeDtypeStruct((M, D_MODEL), F32),
        scratch_shapes=[pltpu.VMEM((tm, D_MODEL), F32), pltpu.VMEM((tm, LANES), F32), pltpu.VMEM((tm, LANES), F32),
                        pltpu.VMEM((tm, LANES), F32), pltpu.VMEM((LANES, tm), F32), pltpu.VMEM((LANES, tm), F32),
                        pltpu.SMEM((n_exp,), jnp.int32)],
        compiler_params=_cparams(("parallel", "arbitrary"), big=True), name="moe_ln")(
            x, rwh, rwl, rb, low, w13, w13, w2, g, be)


def _row_consts(rows, Td, past):
    row = lax.broadcasted_iota(jnp.int32, (rows, 1), 0)
    head = row // Td
    qpos = past + (row - head * Td)
    slope = jnp.zeros((rows, 1), F32)
    for h in range(NSA_HEADS):
        slope = jnp.where(head == h, SLOPES[h], slope)
    return qpos, slope


def _nsa_s_cmp_body(qz_ref, ak_ref, bk_ref, av_ref, bv_ref, ovl_ref, hs_ref, ocmp_ref, sel_ref, *, Td, past, n_slc):
    rows = qz_ref.shape[0]
    NC = ak_ref.shape[0]
    qpos, slope = _row_consts(rows, Td, past)
    ck = (ak_ref[...] + pltpu.roll(bk_ref[...], NC - 1, 0)).astype(BF16)
    cv = (av_ref[...] + pltpu.roll(bv_ref[...], NC - 1, 0)).astype(BF16)
    qz = (qz_ref[...] * (HEAD_DIM ** -0.5)).astype(BF16)
    cend = CMP_STRIDE * lax.broadcasted_iota(jnp.int32, (1, NC), 1) + (CMP_BLOCK - 1)
    dist = qpos - cend
    p = _masked_softmax(_dot_nt(qz, ck) - slope * dist.astype(F32), dist >= 0)
    ocmp_ref[...] = _dot(p.astype(BF16), cv)
    ph, pl_ = _split2(p)
    imp_h = _dot(ph, ovl_ref[...]) + _dot(pl_, ovl_ref[...])
    i1, i2, i3 = _split3(imp_h)
    hs = hs_ref[...]
    imp = (_dot(hs, i1) + _dot(hs, i2)) + _dot(hs, i3)
    NSP = imp.shape[1]
    lane = lax.broadcasted_iota(jnp.int32, (rows, NSP), 1)
    cur = qpos >> 6
    valid = (lane * SLC_BLOCK <= qpos) & (lane < n_slc)
    forced = (lane == 0) | (lane == cur) | (lane == cur - 1)
    score = jnp.where(valid, jnp.where(forced, FORCE_SCORE, imp), jnp.where(lane < n_slc, -1.0, -2.0))
    sel_ref[...] = _topk_mask(score, lane.astype(F32), SLC_TOP)


def _nsa_s_cmp(qz, ak, bk, av, bv, ovl, hs, Td, past, n_slc):
    Bd, rows, _ = qz.shape
    NC = ak.shape[1]
    NSP = ovl.shape[1]
    cspec = pl.BlockSpec((None, NC, KV_W), lambda b: (b, 0, 0))
    return pl.pallas_call(
        functools.partial(_nsa_s_cmp_body, Td=Td, past=past, n_slc=n_slc), grid=(Bd,),
        in_specs=[pl.BlockSpec((None, rows, KV_W), lambda b: (b, 0, 0)), cspec, cspec, cspec, cspec,
                  pl.BlockSpec(ovl.shape, lambda b: (0, 0)), pl.BlockSpec(hs.shape, lambda b: (0, 0))],
        out_specs=[pl.BlockSpec((None, rows, KV_W), lambda b: (b, 0, 0)),
                   pl.BlockSpec((None, rows, NSP), lambda b: (b, 0, 0))],
        out_shape=[jax.ShapeDtypeStruct((Bd, rows, KV_W), F32), jax.ShapeDtypeStruct((Bd, rows, NSP), F32)],
        compiler_params=_cparams(("parallel",), big=True), name="nsa_sample_cmp")(qz, ak, bk, av, bv, ovl, hs)


def _nsa_s_slc_body(pt_ref, plist_ref, nneed_ref, *refs, P, Td, past, n_steps):
    kp = refs[:P]
    vp = refs[P:2 * P]
    (qz_ref, g_ref, ocmp_ref, sel_ref, nk_ref, nv_ref, wk_ref, wv_ref, o_ref, m_s, l_s, acc_s) = refs[2 * P:]
    b_id = pl.program_id(0)
    s_id = pl.program_id(1)
    rows = qz_ref.shape[0]
    NSP = sel_ref.shape[1]
    qpos, slope = _row_consts(rows, Td, past)
    qz = (qz_ref[...] * (HEAD_DIM ** -0.5)).astype(BF16)
    sel_b = sel_ref[...].astype(BF16)

    @pl.when(s_id == 0)
    def _():
        m_s[...] = jnp.full_like(m_s, NEG)
        l_s[...] = jnp.zeros_like(l_s)
        acc_s[...] = jnp.zeros_like(acc_s)

    def tile(kt, vt, kpos, extra):
        nk = kt.shape[1]
        expand = ((kpos >> 6) == lax.broadcasted_iota(jnp.int32, (NSP, nk), 0)).astype(BF16)
        dist = qpos - kpos
        mask = (_dot(sel_b, expand) > 0.5) & (dist >= 0)
        if extra is not None:
            mask = mask & extra
        s = jnp.where(mask, _dot(qz, kt) - slope * dist.astype(F32), NEG)
        m_old = m_s[...]
        m_new = jnp.maximum(m_old, jnp.max(s, axis=-1, keepdims=True))
        alpha = jnp.exp(m_old - m_new)
        e = jnp.where(mask, jnp.exp(s - m_new), 0.0)
        l_s[...] = alpha * l_s[...] + jnp.sum(e, axis=-1, keepdims=True)
        acc_s[...] = alpha * acc_s[...] + _dot_nt(e.astype(BF16), vt)
        m_s[...] = m_new

    nneed = nneed_ref[b_id]

    @pl.when(s_id * P < nneed)
    def _():
        kt = jnp.concatenate([r[...] for r in kp], axis=1).astype(BF16)
        vt = jnp.concatenate([r[...] for r in vp], axis=1).astype(BF16)
        lane = lax.broadcasted_iota(jnp.int32, (1, LANES), 1)
        kpos = jnp.concatenate([plist_ref[b_id, s_id * P + i] * LANES + lane for i in range(P)], axis=1)
        live = jnp.concatenate([jnp.full((1, LANES), s_id * P + i, jnp.int32) for i in range(P)], axis=1) < nneed
        tile(kt, vt, kpos, live)

    @pl.when(s_id == n_steps - 1)
    def _():
        nl = lax.broadcasted_iota(jnp.int32, (1, LANES), 1)
        tile(nk_ref[...].astype(BF16), nv_ref[...].astype(BF16), past + nl, nl < Td)
        o_slc = acc_s[...] / jnp.maximum(l_s[...], TINY)
        nw = wk_ref.shape[1]
        wl = lax.broadcasted_iota(jnp.int32, (1, nw), 1)
        dist = qpos - (past - (nw - LANES) + wl)
        okw = (wl < nw - LANES + Td) & (dist >= 0) & (dist <= WINDOW)
        pw = _masked_softmax(_dot(qz, wk_ref[...].astype(BF16)) - slope * dist.astype(F32), okw)
        o_win = _dot_nt(pw.astype(BF16), wv_ref[...].astype(BF16))
        gate = jax.nn.sigmoid(g_ref[...])
        o_ref[...] = gate[:, 0:1] * ocmp_ref[...] + gate[:, 1:2] * o_slc + gate[:, 2:3] * o_win


def _nsa_s_slc(page_table, layer, skc, svc, qz, graw, ocmp, sel, nks, nvs, wkx, wvx, Td, past, P):
    Bd, NP = page_table.shape
    rows = qz.shape[1]
    n_steps = NP // P
    blocks_per_page = LANES // SLC_BLOCK
    need_blk = jnp.any(sel[:, :, :NP * blocks_per_page] > 0.5, axis=1)
    need_page = jnp.any(need_blk.reshape(Bd, NP, blocks_per_page), axis=2)
    nneed = jnp.sum(need_page, axis=1).astype(jnp.int32)
    order = jnp.argsort(jnp.logical_not(need_page), axis=1, stable=True).astype(jnp.int32)
    slot = jnp.minimum(jnp.arange(NP, dtype=jnp.int32)[None, :], nneed[:, None] - 1)
    plist = jnp.take_along_axis(order, slot, axis=1)
    pspec = [pl.BlockSpec((None, None, KV_W, LANES),
                          lambda b, s, pt, pls, nn, i=i: (layer, pt[b, pls[b, s * P + i]], 0, 0)) for i in range(P)]
    per_b = lambda a: pl.BlockSpec((None,) + a.shape[1:], lambda b, s, pt, pls, nn: (b, 0, 0))
    grid_spec = pltpu.PrefetchScalarGridSpec(
        num_scalar_prefetch=3, grid=(Bd, n_steps),
        in_specs=pspec + pspec + [per_b(a) for a in (qz, graw, ocmp, sel, nks, nvs, wkx, wvx)],
        out_specs=pl.BlockSpec((None, rows, KV_W), lambda b, s, pt, pls, nn: (b, 0, 0)),
        scratch_shapes=[pltpu.VMEM((rows, 1), F32), pltpu.VMEM((rows, 1), F32), pltpu.VMEM((rows, KV_W), F32)])
    return pl.pallas_call(
        functools.partial(_nsa_s_slc_body, P=P, Td=Td, past=past, n_steps=n_steps), grid_spec=grid_spec,
        out_shape=jax.ShapeDtypeStruct((Bd, rows, KV_W), F32),
        compiler_params=_cparams(("parallel", "arbitrary"), big=True), name="nsa_sample_slc")(
            page_table, plist, nneed, *([skc] * P), *([svc] * P), qz, graw, ocmp, sel, nks, nvs, wkx, wvx)


def _sb_s_body(pt_ref, need_ref, *refs, P, Td, n_steps, first):
    kp = refs[:P]
    vp = refs[P:2 * P]
    q_ref, in0_ref, in1_ref, u_ref, co_ref, o_ref, carry_s, acc_s, alive_s = refs[2 * P:]
    s_id = pl.program_id(1)
    rows = q_ref.shape[0]
    qb = (q_ref[...] * (HEAD_DIM ** -0.5)).astype(BF16)
    u = u_ref[...]

    @pl.when(s_id == 0)
    def _():
        if first:
            row = lax.broadcasted_iota(jnp.int32, (rows, LANES), 0)
            lane = lax.broadcasted_iota(jnp.int32, (rows, LANES), 1)
            newer = lane < (row - (row // Td) * Td)
            carry, acc = _sb_block(qb, in0_ref[...].astype(BF16), in1_ref[...].astype(BF16), u,
                                   jnp.zeros((rows, 1), F32), jnp.zeros((rows, SB_W), F32), newer)
        else:
            carry, acc = in0_ref[:, 0:1], in1_ref[...]
        carry_s[...] = carry
        acc_s[...] = acc
        alive_s[0] = (jnp.max(carry) > SB_DEAD_LOG).astype(jnp.int32)

    @pl.when(alive_s[0] > 0)
    def _():
        carry, acc = carry_s[...], acc_s[...]
        order = range(P - 1, -1, -1)
        ss = [_dot(qb, kp[i][...].astype(BF16)) for i in order]
        sps = [_softplus(s) for s in ss]
        cats = [jnp.concatenate(_split2(-sp), axis=0) for sp in sps]
        afters = [_dot(c, u) for c in cats]
        ws = []
        for s, sp, r in zip(ss, sps, afters):
            ws.append(jnp.exp((s - sp) + (r[:rows] + r[rows:]) + carry).astype(BF16))
            carry = carry - jnp.sum(sp, axis=-1, keepdims=True)
        for w, i in zip(ws, order):
            acc = acc + _dot_nt(w, vp[i][...].astype(BF16))
        carry_s[...] = carry
        acc_s[...] = acc
        alive_s[0] = (jnp.max(carry) > SB_DEAD_LOG).astype(jnp.int32)

    @pl.when(s_id == n_steps - 1)
    def _():
        co_ref[...] = jnp.broadcast_to(carry_s[...], co_ref.shape)
        o_ref[...] = acc_s[...]


def _sb_sample_part(page_table, need, layer, kc, vc, qbd, in0, in1, u, Td, P, first, s0, n_steps):
    Bd, NP = page_table.shape
    rows = qbd.shape[1]
    pspec = [pl.BlockSpec((None, None, SB_W, LANES),
                          lambda b, s, pt, nd, i=i: (layer, pt[b, NP - (s0 + s * nd[b] + 1) * P + i], 0, 0))
             for i in range(P)]
    per_b = lambda a: pl.BlockSpec((None,) + a.shape[1:], lambda b, s, pt, nd: (b, 0, 0))
    grid_spec = pltpu.PrefetchScalarGridSpec(
        num_scalar_prefetch=2, grid=(Bd, n_steps),
        in_specs=pspec + pspec + [per_b(qbd), per_b(in0), per_b(in1),
                                  pl.BlockSpec(u.shape, lambda b, s, pt, nd: (0, 0))],
        out_specs=[pl.BlockSpec((None, rows, LANES), lambda b, s, pt, nd: (b, 0, 0)),
                   pl.BlockSpec((None, rows, SB_W), lambda b, s, pt, nd: (b, 0, 0))],
        scratch_shapes=[pltpu.VMEM((rows, 1), F32), pltpu.VMEM((rows, SB_W), F32), pltpu.SMEM((1,), jnp.int32)])
    return pl.pallas_call(
        functools.partial(_sb_s_body, P=P, Td=Td, n_steps=n_steps, first=first), grid_spec=grid_spec,
        out_shape=[jax.ShapeDtypeStruct((Bd, rows, LANES), F32), jax.ShapeDtypeStruct((Bd, rows, SB_W), F32)],
        compiler_params=_cparams(("parallel", "arbitrary"), big=True), name="sb_sample")(
            page_table, need, *([kc] * P), *([vc] * P), qbd, in0, in1, u)


def _sb_sample(page_table, layer, kc, vc, qbd, nk, nv, u, Td, P):
    Bd, NP = page_table.shape
    carry, acc = _sb_sample_part(page_table, jnp.ones((Bd,), jnp.int32), layer, kc, vc, qbd, nk, nv, u, Td, P,
                                 True, 0, 1)
    if NP // P == 1:
        return acc
    need = (jnp.max(carry[:, :, 0], axis=1) > SB_DEAD_LOG).astype(jnp.int32)
    older = lambda: _sb_sample_part(page_table, need, layer, kc, vc, qbd, carry, acc, u, Td, P, False, 1,
                                    NP // P - 1)[1]
    return lax.cond(jnp.any(need > 0), older, lambda: acc)


def _cmp_weights(w, pe):
    ratio = CMP_BLOCK // CMP_STRIDE
    w5 = w.reshape(2, ratio, CMP_STRIDE, HEAD_DIM, HEAD_DIM)
    eye = jnp.eye(NSA_KV, dtype=w.dtype)
    wbd = jnp.einsum('krcde,gh->krcgdhe', w5, eye).reshape(2, ratio, CMP_STRIDE * KV_W, KV_W)
    pe2 = jnp.broadcast_to(pe.reshape(ratio, CMP_STRIDE, 1, HEAD_DIM), (ratio, CMP_STRIDE, NSA_KV, HEAD_DIM))
    return wbd.astype(BF16), jnp.swapaxes(wbd, 2, 3).astype(BF16), pe2.reshape(ratio, 1, CMP_STRIDE * KV_W)


def _overlap(n_cmp_pad, n_slc_pad):
    n = np.arange(n_cmp_pad)[:, None] * CMP_STRIDE
    j = np.arange(n_slc_pad)[None, :] * SLC_BLOCK
    return jnp.asarray((n < j + SLC_BLOCK) & (n + CMP_BLOCK > j), dtype=BF16)


def _suffix_matrix(n):
    return jnp.asarray(np.arange(n)[:, None] > np.arange(n)[None, :], dtype=BF16)


def _pair_suffix_matrix(n):
    u = (np.arange(n)[:, None] > np.arange(n)[None, :]).astype(np.float32)
    one, z = np.ones((n, n), np.float32), np.zeros((n, n), np.float32)
    return jnp.asarray(np.block([[u, z, one, z], [z, u, z, one]]), dtype=BF16)


def _feature_major_state(xt, heads):
    d, b, _, t = xt.shape
    return xt.reshape(d, b, heads, HEAD_DIM, t).transpose(0, 1, 4, 2, 3)


def kernel(x_prompt, x_sample, mem_prompt, cache_nsa_cmp_k, cache_nsa_cmp_v, cache_nsa_slc_k, cache_nsa_slc_v, cache_nsa_win_k, cache_nsa_win_v, cache_sb_k, cache_sb_v, cache_mem_k, cache_mem_v, state_lru_h, state_lru_conv, page_table, w_in, nsa_cmp_w, nsa_cmp_pe, lru_conv_w, lru_conv_b, lru_wa, lru_ba, lru_wx, lru_bx, lru_lambda, w_mem_kv, w_branch, w_out, ln1_g, ln1_b, ln2_g, ln2_b, ffn_w13, ffn_w2, moe_router_w, moe_router_b, moe_w13, moe_w2):
    depth = w_in.shape[0]
    B, T, _ = x_prompt.shape
    Bd, Td, _ = x_sample.shape
    NP = page_table.shape[1]
    page = cache_nsa_cmp_k.shape[2]
    past = NP * page
    win_buf = cache_nsa_win_k.shape[2]
    n_mem = mem_prompt.shape[1]
    alpha = (2.0 * depth) ** 0.25
    assert page == LANES and T % 512 == 0 and T // SLC_BLOCK >= SLC_TOP and T // SLC_BLOCK <= LANES
    assert win_buf == WINDOW and Td < CMP_STRIDE and NP % 8 == 0
    P = 8
    rows = NSA_HEADS * Td

    fm = lambda c: c.transpose(0, 1, 3, 4, 2).reshape(c.shape[0], c.shape[1], c.shape[3] * c.shape[4], c.shape[2])
    slc_kc, slc_vc, sb_kc, sb_vc = fm(cache_nsa_slc_k), fm(cache_nsa_slc_v), fm(cache_sb_k), fm(cache_sb_v)
    win_kc, win_vc = fm(cache_nsa_win_k), fm(cache_nsa_win_v)
    cmp_kc, cmp_vc = fm(cache_nsa_cmp_k), fm(cache_nsa_cmp_v)
    P_cmp = 16 if NP % 16 == 0 else 8

    n_cmp_p = T // CMP_STRIDE
    ovl_p = _overlap(n_cmp_p, LANES)
    n_slc_s = -(-(past + Td) // SLC_BLOCK)
    nsp_s = -(-n_slc_s // LANES) * LANES
    n_cmp_s = past // CMP_STRIDE
    ovl_s = _overlap(n_cmp_s, nsp_s)
    u_blk = _suffix_matrix(LANES)
    u_pair = _pair_suffix_matrix(LANES)
    hs = jnp.asarray((np.arange(rows)[:, None] // (NSA_GROUP * Td) == np.arange(rows)[None, :] // (NSA_GROUP * Td))
                     & (np.arange(rows)[:, None] % Td == np.arange(rows)[None, :] % Td), dtype=BF16)

    xp = x_prompt
    xs = x_sample.reshape(1, Bd * Td, D_MODEL)
    mem2 = mem_prompt.reshape(1, B * n_mem, D_MODEL)
    p_layers, s_layers = [], []
    for l in range(depth):
        wt = jnp.swapaxes(w_in[l], 0, 1)
        c = np.cumsum([0, 512, 128, 128, 128, 128, 128, 128, 24, 512, 512, 512, 512, 512, 512, 4096])
        seg = lambda a, b: wt[c[a]:c[b]]
        ng_w = jnp.pad(seg(7, 8), ((0, LANES - 24), (0, 0)))
        wn = jnp.concatenate([seg(0, 1), ng_w, seg(8, 10), seg(10, 11), seg(13, 14), seg(1, 3)], axis=0)
        wn = jnp.swapaxes(wn, 0, 1).astype(BF16)
        wtr = jnp.concatenate([seg(1, 7), seg(11, 13)], axis=0).astype(BF16)
        w_all = jnp.swapaxes(jnp.concatenate([seg(0, 7), ng_w, seg(8, 14)], axis=0), 0, 1).astype(BF16)
        wmg = jnp.swapaxes(seg(14, 15), 0, 1).astype(BF16)
        cw_tok, cw_feat, pe2 = _cmp_weights(nsa_cmp_w[l], nsa_cmp_pe[l])
        eye8 = jnp.eye(LRU_BLOCKS, dtype=F32)
        bd = lambda w: jnp.einsum('ncd,nm->ncmd', w, eye8).reshape(D_RNN, D_RNN)
        wbd = jnp.concatenate([bd(lru_wa[l]), bd(lru_wx[l])], axis=1).astype(BF16)
        gbias = jnp.concatenate([lru_ba[l].reshape(1, D_RNN), lru_bx[l].reshape(1, D_RNN)], axis=1)
        lam = lru_lambda[l].reshape(1, D_RNN)
        conv_w, conv_b = lru_conv_w[l], lru_conv_b[l].reshape(1, D_RNN)
        wmem = w_mem_kv[l].astype(BF16)
        wbr, wo = w_branch[l].astype(BF16), w_out[l].astype(BF16)
        g1, b1 = ln1_g[l].reshape(1, D_MODEL), ln1_b[l].reshape(1, D_MODEL)
        g2, b2 = ln2_g[l].reshape(1, D_MODEL), ln2_b[l].reshape(1, D_MODEL)

        ncols = [(0, 512), (512, 128), (640, 512), (1152, 512), (1664, 512), (2176, 512), (2688, 128), (2816, 128)]
        tcols = [(0, 128, ("full",)), (128, 128, ("full",)), (256, 128, ("full", "blk")), (384, 128, ("full", "blk")),
                 (512, 128, ("full", "blk")), (640, 128, ("full", "blk")), (768, 512, ("full", "blk")),
                 (1280, 512, ("full", "blk"))]
        (nq, ng, lx, lg, sq, mq, ck_tok, cv_tok, ck_t, cv_t, sk_t, sk_b, sv_t, sv_b, wk_t, wk_b, wv_t, wv_b,
         sbk_t, sbk_b, sbv_t, sbv_b) = _proj(xp, wn, ncols, wtr, tcols)
        ckc, cvc = _compress(ck_tok.reshape(B, n_cmp_p, CMP_STRIDE * KV_W), cv_tok.reshape(B, n_cmp_p, CMP_STRIDE * KV_W),
                             cw_feat[0], cw_feat[1], pe2)
        o_nsa = _nsa_prompt(nq, ng, ckc, cvc, sk_b, sv_b, wk_b, wv_b, ovl_p)
        o_lru, h_new, conv_new = _rglru(lx, lg, jnp.zeros((B, CONV_W - 1, D_RNN), F32), jnp.zeros((B, 1, D_RNN), F32),
                                        conv_w, conv_b, wbd, gbias, lam)
        o_sb = _sb_prompt(sq, sbk_b, sbv_b, u_pair)
        mk, mv = _proj(mem2, wmem, [(0, 512), (512, 512)])
        mk, mv = mk.reshape(B, n_mem, 512), mv.reshape(B, n_mem, 512)
        o_mem = _mem_attn(mq, mk, mv)
        M = B * T
        x1 = _merge_ln(xp.reshape(M, D_MODEL), [o.reshape(M, 512) for o in (o_nsa, o_lru, o_sb, o_mem)],
                       wmg, wbr, wo, g1, b1, alpha)
        p_layers.append((ck_t, cv_t, sk_t, sv_t, wk_t[:, :, T - min(WINDOW, T):], wv_t[:, :, T - min(WINDOW, T):],
                         sbk_t, sbv_t, h_new.reshape(B, D_RNN), conv_new,
                         mk.reshape(B, n_mem, MEM_HEADS, MEM_HEAD_DIM), mv.reshape(B, n_mem, MEM_HEADS, MEM_HEAD_DIM)))

        ms = Bd * Td
        scols = [(0, 512), (512, 128), (640, 128), (768, 128), (896, 128), (1024, 128), (1152, 128), (1280, 128),
                 (1408, 512), (1920, 512), (2432, 512), (2944, 512), (3456, 512), (3968, 512)]
        (nq_s, ck_s, cv_s, sk_s, sv_s, wk_s, wv_s, ng_s, lx_s, lg_s, sq_s, sbk_s, sbv_s, mq_s) = [
            a.reshape(Bd, Td, a.shape[-1]) for a in _proj(xs, w_all, scols)]
        ak, bk, av, bv = _compress_paged(cmp_kc, cmp_vc, l, page_table, cw_tok[0], cw_tok[1], pe2, P_cmp)
        q8 = nq_s.reshape(Bd, Td, NSA_HEADS, HEAD_DIM).transpose(0, 2, 1, 3)
        gsel = jnp.asarray(np.arange(NSA_HEADS)[:, None] // NSA_GROUP == np.arange(NSA_KV)[None, :], dtype=F32)
        qz = (q8[:, :, :, None, :] * gsel[None, :, None, :, None]).reshape(Bd, rows, KV_W)
        hsel = jnp.eye(SB_HEADS, dtype=F32)
        s8 = sq_s.reshape(Bd, Td, SB_HEADS, HEAD_DIM).transpose(0, 2, 1, 3)
        qbd = (s8[:, :, :, None, :] * hsel[None, :, None, :, None]).reshape(Bd, rows, SB_W)
        graw = jnp.pad(ng_s[:, :, :24].reshape(Bd, Td, NSA_HEADS, 3).transpose(0, 2, 1, 3).reshape(Bd, rows, 3),
                       ((0, 0), (0, 0), (0, LANES - 3)))
        new_fm = lambda a: jnp.pad(jnp.swapaxes(a, 1, 2), ((0, 0), (0, 0), (0, LANES - Td)))
        ocmp, sel = _nsa_s_cmp(qz, ak, bk, av, bv, ovl_s, hs, Td, past, n_slc_s)
        wkx = jnp.concatenate([win_kc[l], new_fm(wk_s)], axis=2)
        wvx = jnp.concatenate([win_vc[l], new_fm(wv_s)], axis=2)
        o_rows = _nsa_s_slc(page_table, l, slc_kc, slc_vc, qz, graw, ocmp, sel, new_fm(sk_s), new_fm(sv_s),
                            wkx, wvx, Td, past, P)
        o6 = o_rows.reshape(Bd, NSA_KV, NSA_GROUP, Td, NSA_KV, HEAD_DIM)
        o_nsa_s = jnp.stack([o6[:, g, :, :, g, :] for g in range(NSA_KV)], axis=1)
        o_nsa_s = o_nsa_s.reshape(Bd, NSA_HEADS, Td, HEAD_DIM).transpose(0, 2, 1, 3).reshape(Bd, Td, NSA_HEADS * HEAD_DIM)
        o_lru_s, h_new_s, conv_new_s = _rglru(lx_s, lg_s, state_lru_conv[l], state_lru_h[l].reshape(Bd, 1, D_RNN),
                                              conv_w, conv_b, wbd, gbias, lam)
        sb_rows = _sb_sample(page_table, l, sb_kc, sb_vc, qbd, new_fm(sbk_s), new_fm(sbv_s), u_blk, Td, P)
        sb5 = sb_rows.reshape(Bd, SB_HEADS, Td, SB_HEADS, HEAD_DIM)
        o_sb_s = jnp.stack([sb5[:, h, :, h, :] for h in range(SB_HEADS)], axis=2).reshape(Bd, Td, SB_W)
        o_mem_s = _mem_attn(mq_s, cache_mem_k[l].reshape(Bd, n_mem, 512), cache_mem_v[l].reshape(Bd, n_mem, 512))
        x1s = _merge_ln(xs.reshape(ms, D_MODEL), [o.reshape(ms, 512) for o in (o_nsa_s, o_lru_s, o_sb_s, o_mem_s)],
                        wmg, wbr, wo, g1, b1, alpha)
        kv4 = lambda a: a.reshape(Bd, Td, NSA_KV, HEAD_DIM)
        s_layers.append((kv4(ck_s), kv4(cv_s), kv4(sk_s), kv4(sv_s), kv4(wk_s), kv4(wv_s),
                         sbk_s.reshape(Bd, Td, SB_HEADS, HEAD_DIM), sbv_s.reshape(Bd, Td, SB_HEADS, HEAD_DIM),
                         h_new_s.reshape(Bd, D_RNN), conv_new_s))

        if l % 2 == 0:
            w13, w2 = ffn_w13[l // 2].astype(BF16), ffn_w2[l // 2].astype(BF16)
            x2 = _ffn_ln(x1, w13, w2, g2, b2, alpha)
            x2s = _ffn_ln(x1s, w13, w2, g2, b2, alpha)
        else:
            rw = jnp.pad(moe_router_w[l // 2], ((0, 0), (0, LANES - N_EXP)))
            rwh = rw.astype(BF16)
            rwl = (rw - rwh.astype(F32)).astype(BF16)
            rb = jnp.pad(moe_router_b[l // 2].reshape(1, N_EXP), ((0, 0), (0, LANES - N_EXP)))
            w13, w2 = moe_w13[l // 2].astype(BF16), moe_w2[l // 2].astype(BF16)
            x2 = _moe_ln(x1, rwh, rwl, rb, w13, w2, g2, b2, alpha)
            x2s = _moe_ln(x1s, rwh, rwl, rb, w13, w2, g2, b2, alpha)
        xp = x2.reshape(B, T, D_MODEL)
        xs = x2s.reshape(1, ms, D_MODEL)

    pst = [jnp.stack(r, axis=0) for r in zip(*p_layers)]
    prompt_state = ([_feature_major_state(a, NSA_KV) for a in pst[:6]] + [_feature_major_state(a, SB_HEADS) for a in pst[6:8]]
                    + pst[8:])
    sample_state = [jnp.stack(r, axis=0) for r in zip(*s_layers)]
    return (xp, xs.reshape(Bd, Td, D_MODEL), *prompt_state, *sample_state)
```
